```python
import math
import jax, jax.numpy as jnp
from jax import lax
import numpy as np

D_MODEL = 1024
BATCH = 16
SEQ = 256
DEPTH = 2
DEC_BATCH = 8
DEC_SEQ = 2048
PAST_LEN = 256

GRID_W = 64
N_EVEN = (DEPTH + 1) // 2
N_ODD = DEPTH // 2
MIX_A = D_MODEL // 2
A_GROUP = 16
A_GROUPS = MIX_A // A_GROUP
S5_N = 64
MIX_B = D_MODEL - MIX_A
NA_HEADS = 8
NA_HD = MIX_B // NA_HEADS
NA_KH = 8
NA_KW = 16
QBLK = 128
DN_HEADS = 8
DN_DK = D_MODEL // DN_HEADS
DN_DV = D_MODEL // DN_HEADS
DN_CONV = 5
DN_CHUNK = 64
D_FF = 4 * D_MODEL
EPS = 1e-6

kernel_name = "hybrid_s5_natten_gdn_diffusion_step"


def _rmsnorm(x, g):
    x32 = x.astype(jnp.float32)
    y = x32 * lax.rsqrt(jnp.mean(x32 * x32, axis=-1, keepdims=True) + EPS)
    return (y * g.astype(jnp.float32)).astype(x.dtype)


def _modulation(cond, w_mod, b_mod):
    m = jax.nn.silu(cond) @ w_mod + b_mod
    return jnp.split(m.reshape(-1, 1, 6 * D_MODEL), 6, axis=-1)


def _modulate(h, shift, scale):
    return h * (1 + scale) + shift


def _mlp(h, w1, w2):
    return jnp.square(jax.nn.relu(h @ w1)) @ w2


def _s5_scan(u, lam_re, lam_im, log_dt, b_re, b_im, c_re, c_im, x0):
    f32 = jnp.float32
    lam = lax.complex(lam_re.astype(f32), lam_im.astype(f32))
    lam_bar = jnp.exp(lam * jnp.exp(log_dt.astype(f32))[:, None])
    b_bar = ((lam_bar - 1) / lam)[..., None] * lax.complex(b_re.astype(f32), b_im.astype(f32))
    bu = jnp.einsum('gnp,blgp->blgn', b_bar, u.astype(jnp.complex64))
    bu = bu.at[:, 0].add(lam_bar * x0)
    a = jnp.broadcast_to(lam_bar, bu.shape)

    def comb(left, right):
        return (left[0] * right[0], right[0] * left[1] + right[1])

    _, xs = lax.associative_scan(comb, (a, bu), axis=1)
    c_mat = lax.complex(c_re.astype(f32), c_im.astype(f32))
    y = jnp.einsum('gpn,blgn->blgp', c_mat, xs).real
    return y, xs[:, -1]


def _s5_mixer(u, x0_re, x0_im, lam_re, lam_im, log_dt, b_re, b_im, c_re, c_im, d, w_glu, b_glu):
    bsz, length, _ = u.shape
    ug = u.astype(jnp.float32).reshape(bsz, length, A_GROUPS, A_GROUP)
    x0 = lax.complex(x0_re.astype(jnp.float32), x0_im.astype(jnp.float32))
    y_f, xl_f = _s5_scan(ug, lam_re[0], lam_im[0], log_dt[0], b_re[0], b_im[0], c_re[0], c_im[0], x0[:, 0])
    y_b, xl_b = _s5_scan(ug[:, ::-1], lam_re[1], lam_im[1], log_dt[1], b_re[1], b_im[1], c_re[1], c_im[1], x0[:, 1])
    y = y_f + y_b[:, ::-1] + d.astype(jnp.float32).reshape(A_GROUPS, A_GROUP) * ug
    z = jax.nn.gelu(y.reshape(bsz, length, MIX_A))
    out = z * jax.nn.sigmoid(z @ w_glu.astype(jnp.float32) + b_glu.astype(jnp.float32))
    xl = jnp.stack([xl_f, xl_b], axis=1)
    return out, xl.real, xl.imag


def _context_attention(q, k, v):
    bsz, s_len, h, d = q.shape
    qb = jnp.swapaxes(q.reshape(bsz, s_len // QBLK, QBLK, h, d), 0, 1)

    def blk(qi):
        s = jnp.einsum('bqhd,bkhd->bhqk', qi, k).astype(jnp.float32)
        p = jax.nn.softmax(s, axis=-1).astype(v.dtype)
        return jnp.einsum('bhqk,bkhd->bqhd', p, v)

    o = lax.map(blk, qb)
    return jnp.swapaxes(o, 0, 1).reshape(bsz, s_len, h * d)


def _neighbourhood_attention(q, k, v, ck, cv, rpb):
    bsz, length, h, d = q.shape
    rows = length // GRID_W
    kh = min(NA_KH, rows)
    kw = NA_KW
    qg = q.reshape(bsz, rows, GRID_W, h, d)
    kg = k.reshape(bsz, rows, GRID_W, h, d)
    vg = v.reshape(bsz, rows, GRID_W, h, d)
    col = jnp.arange(GRID_W)
    cs = jnp.clip(col - kw // 2, 0, GRID_W - kw)
    col_idx = cs[:, None] + jnp.arange(kw)[None, :]
    dc = col_idx - col[:, None] + (NA_KW - 1)

    def row_block(r):
        rs = jnp.clip(r - kh // 2, 0, rows - kh)
        q_r = lax.dynamic_index_in_dim(qg, r, axis=1, keepdims=False)
        k_band = lax.dynamic_slice_in_dim(kg, rs, kh, axis=1)
        v_band = lax.dynamic_slice_in_dim(vg, rs, kh, axis=1)
        k_win = k_band[:, :, col_idx]
        v_win = v_band[:, :, col_idx]
        dr = rs + jnp.arange(kh) - r + (NA_KH - 1)
        bias = rpb[:, dr[None, :, None], dc[:, None, :]]
        s_loc = jnp.einsum('bqhd,biqjhd->bhqij', q_r, k_win) + bias[None]
        s_loc = s_loc.reshape(bsz, h, GRID_W, kh * kw)
        s_ctx = jnp.einsum('bqhd,bphd->bhqp', q_r, ck)
        p = jax.nn.softmax(jnp.concatenate([s_loc, s_ctx], axis=-1).astype(jnp.float32), axis=-1)
        p_loc = p[..., :kh * kw].reshape(bsz, h, GRID_W, kh, kw).astype(v.dtype)
        p_ctx = p[..., kh * kw:].astype(cv.dtype)
        return (jnp.einsum('bhqij,biqjhd->bqhd', p_loc, v_win)
                + jnp.einsum('bhqp,bphd->bqhd', p_ctx, cv))

    o = lax.map(row_block, jnp.arange(rows))
    return jnp.moveaxis(o, 0, 1).reshape(bsz, length, h * d)


def _even_split(h, w_in):
    proj = h @ w_in
    u = proj[..., :MIX_A]
    q, k, v = jnp.split(proj[..., MIX_A:], 3, axis=-1)
    shp = h.shape[:2] + (NA_HEADS, NA_HD)
    return u, q.reshape(shp) * (NA_HD ** -0.5), k.reshape(shp), v.reshape(shp)


def _even_mixer_context(h, w_in, w_out, s5p):
    u, q, k, v = _even_split(h, w_in)
    zeros = jnp.zeros((h.shape[0], 2, A_GROUPS, S5_N), jnp.float32)
    a_out, s_re, s_im = _s5_mixer(u, zeros, zeros, *s5p)
    b_out = _context_attention(q, k, v)
    out = jnp.concatenate([a_out.astype(h.dtype), b_out.astype(h.dtype)], axis=-1) @ w_out
    return out, k, v, s_re, s_im


def _even_mixer_latent(h, ck, cv, x0_re, x0_im, w_in, w_out, s5p, rpb):
    u, q, k, v = _even_split(h, w_in)
    a_out, _, _ = _s5_mixer(u, x0_re, x0_im, *s5p)
    b_out = _neighbourhood_attention(q, k, v, ck, cv, rpb)
    return jnp.concatenate([a_out.astype(h.dtype), b_out.astype(h.dtype)], axis=-1) @ w_out


def _dwconv(x, w):
    kk, ch = w.shape
    return lax.conv_general_dilated(x, w.astype(x.dtype)[:, None, :], window_strides=(1,),
                                    padding=[((kk - 1) // 2, kk // 2)],
                                    dimension_numbers=('NWC', 'WIO', 'NWC'),
                                    feature_group_count=ch)


def _chunk(t):
    b_, l_, h_ = t.shape[:3]
    t = t.reshape((b_, l_ // DN_CHUNK, DN_CHUNK, h_) + t.shape[3:])
    return jnp.moveaxis(t, 3, 1)


def _gdn_chunked(q, k, v, g, beta, s0):
    bsz, length, h, _ = q.shape
    dv = v.shape[-1]
    qc, kc, vc = _chunk(q), _chunk(k), _chunk(v)
    gc, bc = _chunk(g), _chunk(beta)
    gcum = jnp.cumsum(gc, axis=-1)
    tri_incl = jnp.tril(jnp.ones((DN_CHUNK, DN_CHUNK), bool))
    tri_strict = jnp.tril(jnp.ones((DN_CHUNK, DN_CHUNK), bool), -1)
    decay = jnp.exp(jnp.where(tri_incl, gcum[..., :, None] - gcum[..., None, :], -jnp.inf))
    kb = kc * bc[..., None]
    a_mat = jnp.where(tri_strict, jnp.einsum('bhncd,bhnsd->bhncs', kb, kc) * decay, 0.0)
    eye = jnp.eye(DN_CHUNK, dtype=jnp.float32)
    t_mat = lax.linalg.triangular_solve(eye + a_mat, jnp.broadcast_to(eye, a_mat.shape),
                                        left_side=True, lower=True)
    u_c = t_mat @ (vc * bc[..., None])
    w_c = t_mat @ (kb * jnp.exp(gcum)[..., None])
    attn = jnp.where(tri_incl, jnp.einsum('bhncd,bhnsd->bhncs', qc, kc) * decay, 0.0)

    def step(s, xs):
        q_i, k_i, u_i, w_i, g_i, at_i = xs
        v_new = u_i - w_i @ s
        o = (q_i * jnp.exp(g_i)[..., None]) @ s + at_i @ v_new
        g_last = g_i[..., -1]
        s = (s * jnp.exp(g_last)[..., None, None]
             + jnp.einsum('bhcd,bhce->bhde', k_i * jnp.exp(g_last[..., None] - g_i)[..., None], v_new))
        return s, o

    xs = tuple(jnp.moveaxis(t, 2, 0) for t in (qc, kc, u_c, w_c, gcum, attn))
    s_last, o = lax.scan(step, s0, xs)
    return jnp.transpose(o, (1, 0, 3, 2, 4)).reshape(bsz, length, h, dv), s_last


def _deltanet_mixer(h, s0, w_in, conv_w, a_log, dt_bias, norm_g, w_out):
    f32 = jnp.float32
    bsz, length, _ = h.shape
    proj = h @ w_in
    qkv = jax.nn.silu(_dwconv(proj[..., :3 * D_MODEL], conv_w)).astype(f32)
    z = proj[..., 3 * D_MODEL:4 * D_MODEL].astype(f32).reshape(bsz, length, DN_HEADS, DN_DV)
    a_in = proj[..., 4 * D_MODEL:4 * D_MODEL + 2 * DN_HEADS].astype(f32).reshape(bsz, length, 2, DN_HEADS)
    b_in = proj[..., 4 * D_MODEL + 2 * DN_HEADS:].astype(f32).reshape(bsz, length, 2, DN_HEADS)
    g = -jnp.exp(a_log.astype(f32)) * jax.nn.softplus(a_in + dt_bias.astype(f32))
    beta = jax.nn.sigmoid(b_in)
    q, k, v = jnp.split(qkv, 3, axis=-1)
    q = q.reshape(bsz, length, DN_HEADS, DN_DK)
    k = k.reshape(bsz, length, DN_HEADS, DN_DK)
    v = v.reshape(bsz, length, DN_HEADS, DN_DV)
    q = q * lax.rsqrt(jnp.sum(q * q, axis=-1, keepdims=True) + EPS) * (DN_DK ** -0.5)
    k = k * lax.rsqrt(jnp.sum(k * k, axis=-1, keepdims=True) + EPS)
    s0 = s0.astype(f32)
    o_f, s_f = _gdn_chunked(q, k, v, g[:, :, 0], beta[:, :, 0], s0[:, 0])
    o_b, s_b = _gdn_chunked(q[:, ::-1], k[:, ::-1], v[:, ::-1], g[:, ::-1, 1], beta[:, ::-1, 1], s0[:, 1])
    o = o_f + o_b[:, ::-1]
    o = _rmsnorm(o, norm_g) * jax.nn.silu(z)
    out = o.reshape(bsz, length, D_MODEL).astype(h.dtype) @ w_out
    return out, jnp.stack([s_f, s_b], axis=1)


def setup_inputs(seed: int = 0) -> dict:
    key = jax.random.key(seed)
    ks = iter(jax.random.split(key, 48))
    f32 = jnp.float32

    def nrm(shape, scale):
        return jax.random.normal(next(ks), shape, f32) * scale

    def unif(shape, lo, hi):
        return jax.random.uniform(next(ks), shape, f32, minval=lo, maxval=hi)

    dn_dt = jnp.exp(unif((N_ODD, 2, DN_HEADS), math.log(1e-3), math.log(1e-1)))
    return {
        "x_prompt": nrm((BATCH, SEQ, D_MODEL), 1.0),
        "x_sample": nrm((DEC_BATCH, DEC_SEQ, D_MODEL), 1.0),
        "c": nrm((DEC_BATCH, D_MODEL), 1.0),
        "cache_na_k": nrm((DEC_BATCH, N_EVEN, PAST_LEN, NA_HEADS, NA_HD), 1.0),
        "cache_na_v": nrm((DEC_BATCH, N_EVEN, PAST_LEN, NA_HEADS, NA_HD), 1.0),
        "state_s5_re": nrm((DEC_BATCH, N_EVEN, 2, A_GROUPS, S5_N), 0.5),
        "state_s5_im": nrm((DEC_BATCH, N_EVEN, 2, A_GROUPS, S5_N), 0.5),
        "state_dn": nrm((DEC_BATCH, N_ODD, 2, DN_HEADS, DN_DK, DN_DV), 0.3),
        "c_ctx": nrm((D_MODEL,), 1.0),
        "norm_mix_g": 1.0 + nrm((DEPTH, D_MODEL), 0.02),
        "norm_ff_g": 1.0 + nrm((DEPTH, D_MODEL), 0.02),
        "w_mod": nrm((DEPTH, D_MODEL, 6 * D_MODEL), D_MODEL ** -0.5),
        "b_mod": nrm((DEPTH, 6 * D_MODEL), 0.02),
        "w_ff1": nrm((DEPTH, D_MODEL, D_FF), D_MODEL ** -0.5),
        "w_ff2": nrm((DEPTH, D_FF, D_MODEL), D_FF ** -0.5),
        "w_in_e": nrm((N_EVEN, D_MODEL, MIX_A + 3 * MIX_B), D_MODEL ** -0.5),
        "w_out_e": nrm((N_EVEN, D_MODEL, D_MODEL), D_MODEL ** -0.5),
        "s5_lam_re": -0.5 + nrm((N_EVEN, 2, A_GROUPS, S5_N), 0.01),
        "s5_lam_im": math.pi * jnp.arange(S5_N, dtype=f32) + nrm((N_EVEN, 2, A_GROUPS, S5_N), 0.01),
        "s5_log_dt": unif((N_EVEN, 2, A_GROUPS), math.log(1e-3), math.log(1e-1)),
        "s5_b_re": nrm((N_EVEN, 2, A_GROUPS, S5_N, A_GROUP), (2 * A_GROUP) ** -0.5),
        "s5_b_im": nrm((N_EVEN, 2, A_GROUPS, S5_N, A_GROUP), (2 * A_GROUP) ** -0.5),
        "s5_c_re": nrm((N_EVEN, 2, A_GROUPS, A_GROUP, S5_N), S5_N ** -0.5),
        "s5_c_im": nrm((N_EVEN, 2, A_GROUPS, A_GROUP, S5_N), S5_N ** -0.5),
        "s5_d": nrm((N_EVEN, MIX_A), 1.0),
        "s5_w_glu": nrm((N_EVEN, MIX_A, MIX_A), MIX_A ** -0.5),
        "s5_b_glu": nrm((N_EVEN, MIX_A), 0.02),
        "na_rpb": nrm((N_EVEN, NA_HEADS, 2 * NA_KH - 1, 2 * NA_KW - 1), 0.1),
        "w_in_o": nrm((N_ODD, D_MODEL, 4 * D_MODEL + 4 * DN_HEADS), D_MODEL ** -0.5),
        "dn_conv_w": nrm((N_ODD, DN_CONV, 3 * D_MODEL), DN_CONV ** -0.5),
        "dn_a_log": jnp.log(unif((N_ODD, 2, DN_HEADS), 1.0, 16.0)),
        "dn_dt_bias": dn_dt + jnp.log(-jnp.expm1(-dn_dt)),
        "dn_norm_g": 1.0 + nrm((N_ODD, DN_DV), 0.02),
        "w_out_o": nrm((N_ODD, D_MODEL, D_MODEL), D_MODEL ** -0.5),
        "final_norm_g": 1.0 + nrm((D_MODEL,), 0.02),
    }


def reference(x_prompt, x_sample, c, cache_na_k, cache_na_v, state_s5_re, state_s5_im, state_dn,
              c_ctx, norm_mix_g, norm_ff_g, w_mod, b_mod, w_ff1, w_ff2, w_in_e, w_out_e,
              s5_lam_re, s5_lam_im, s5_log_dt, s5_b_re, s5_b_im, s5_c_re, s5_c_im, s5_d,
              s5_w_glu, s5_b_glu, na_rpb, w_in_o, dn_conv_w, dn_a_log, dn_dt_bias, dn_norm_g,
              w_out_o, final_norm_g):
    xp, xs = x_prompt, x_sample
    nk, nv, ns_re, ns_im, nd = [], [], [], [], []
    for l in range(DEPTH):
        mp = _modulation(c_ctx, w_mod[l], b_mod[l])
        ms = _modulation(c, w_mod[l], b_mod[l])
        hp = _modulate(_rmsnorm(xp, norm_mix_g[l]), mp[0], mp[1])
        hs = _modulate(_rmsnorm(xs, norm_mix_g[l]), ms[0], ms[1])
        if l % 2 == 0:
            e = l // 2
            s5p = (s5_lam_re[e], s5_lam_im[e], s5_log_dt[e], s5_b_re[e], s5_b_im[e],
                   s5_c_re[e], s5_c_im[e], s5_d[e], s5_w_glu[e], s5_b_glu[e])
            op, kp, vp, sre, sim = _even_mixer_context(hp, w_in_e[e], w_out_e[e], s5p)
            os_ = _even_mixer_latent(hs, cache_na_k[:, e], cache_na_v[:, e], state_s5_re[:, e],
                                     state_s5_im[:, e], w_in_e[e], w_out_e[e], s5p, na_rpb[e])
            nk.append(kp)
            nv.append(vp)
            ns_re.append(sre)
            ns_im.append(sim)
        else:
            o = l // 2
            zeros = jnp.zeros((xp.shape[0], 2, DN_HEADS, DN_DK, DN_DV), jnp.float32)
            op, sdn = _deltanet_mixer(hp, zeros, w_in_o[o], dn_conv_w[o], dn_a_log[o],
                                      dn_dt_bias[o], dn_norm_g[o], w_out_o[o])
            os_, _ = _deltanet_mixer(hs, state_dn[:, o], w_in_o[o], dn_conv_w[o], dn_a_log[o],
                                     dn_dt_bias[o], dn_norm_g[o], w_out_o[o])
            nd.append(sdn)
        xp = xp + mp[2] * op
        xs = xs + ms[2] * os_
        xp = xp + mp[5] * _mlp(_modulate(_rmsnorm(xp, norm_ff_g[l]), mp[3], mp[4]), w_ff1[l], w_ff2[l])
        xs = xs + ms[5] * _mlp(_modulate(_rmsnorm(xs, norm_ff_g[l]), ms[3], ms[4]), w_ff1[l], w_ff2[l])
    y_prompt = _rmsnorm(xp, final_norm_g)
    y_sample = _rmsnorm(xs, final_norm_g)
    new_cache_na_k = jnp.stack(nk, axis=1)
    new_cache_na_v = jnp.stack(nv, axis=1)
    new_state_s5_re = jnp.stack(ns_re, axis=1)
    new_state_s5_im = jnp.stack(ns_im, axis=1)
    new_state_dn = jnp.stack(nd, axis=1)
    return (y_prompt, y_sample, new_cache_na_k, new_cache_na_v, new_state_s5_re, new_state_s5_im, new_state_dn)
```

```python
import functools
import math

import jax
import jax.numpy as jnp
from jax import lax
from jax.experimental import pallas as pl
from jax.experimental.pallas import tpu as pltpu

F32 = jnp.float32
BF16 = jnp.bfloat16

D_MODEL = 1024
DEPTH = 2
GRID_W = 64
MIX_A = D_MODEL // 2
A_GROUP = 16
A_GROUPS = MIX_A // A_GROUP
S5_N = 64
MIX_B = D_MODEL - MIX_A
NA_HEADS = 8
NA_HD = MIX_B // NA_HEADS
NA_KH = 8
NA_KW = 16
DN_HEADS = 8
DN_DK = D_MODEL // DN_HEADS
DN_DV = D_MODEL // DN_HEADS
DN_CONV = 5
DN_CHUNK = 64
D_FF = 4 * D_MODEL
EPS = 1e-6

LANES = 128
SUBLANES = 8
VMEM_LIMIT = 56 * 1024 * 1024
NEG_BIG = -1e30

S5_T = 16
S5_BG = SUBLANES
S5_JB = MIX_A // LANES
S5_GPB = LANES // A_GROUP
S5_SW = S5_GPB * S5_N
GDN_TILE = 256


def _cp(sem, vmem=VMEM_LIMIT):
    return pltpu.CompilerParams(dimension_semantics=sem, vmem_limit_bytes=vmem)


def _rms(x, g):
    return x * lax.rsqrt(jnp.mean(x * x, axis=-1, keepdims=True) + EPS) * g


def _sigmoid(x):
    return 1.0 / (1.0 + jnp.exp(-x))


def _silu(x):
    return x * _sigmoid(x)


def _gelu_tanh(x):
    return 0.5 * x * (1.0 + jnp.tanh(math.sqrt(2.0 / math.pi) * (x + 0.044715 * (x * x * x))))


def _dot(a, b):
    return jnp.dot(a, b, preferred_element_type=F32)


def _dot_nt(a, b):
    return lax.dot_general(a, b, (((1,), (1,)), ((), ())), preferred_element_type=F32)


def _dot_tn(a, b):
    return lax.dot_general(a, b, (((0,), (0,)), ((), ())), preferred_element_type=F32)


def _mod_body(c_ref, w_ref, b_ref, o_ref):
    c = c_ref[...]
    o_ref[...] = _dot(_silu(c).astype(BF16), w_ref[...]) + b_ref[...]


def _modulation(cond16, w_mod_bf, b_mod):
    tn = 1536
    n = 6 * D_MODEL
    return pl.pallas_call(
        _mod_body,
        grid=(DEPTH, n // tn),
        in_specs=[
            pl.BlockSpec((16, D_MODEL), lambda l, j: (0, 0)),
            pl.BlockSpec((None, D_MODEL, tn), lambda l, j: (l, 0, j)),
            pl.BlockSpec((None, 1, tn), lambda l, j: (l, 0, j)),
        ],
        out_specs=pl.BlockSpec((None, 16, tn), lambda l, j: (l, 0, j)),
        out_shape=jax.ShapeDtypeStruct((DEPTH, 16, n), F32),
        compiler_params=_cp(("parallel", "parallel")),
        name="modulation",
    )(cond16, w_mod_bf, b_mod)


def _mod_spec(which, seq_fn, ngrid):
    if ngrid == 1:
        return pl.BlockSpec((None, None, 1, D_MODEL), lambda i: (which, seq_fn(i), 0, 0))
    return pl.BlockSpec((None, None, 1, D_MODEL), lambda i, j: (which, seq_fn(i), 0, 0))


def _nmm_body(x_ref, g_ref, sh_ref, sc_ref, w_ref, o_ref, h_ref):
    @pl.when(pl.program_id(1) == 0)
    def _():
        h = _rms(x_ref[...], g_ref[...]) * (1.0 + sc_ref[...]) + sh_ref[...]
        h_ref[...] = h.astype(BF16)

    o_ref[...] = _dot(h_ref[...], w_ref[...]).astype(o_ref.dtype)


def _norm_mod_matmul(x, g_row, modt_l, w_bf, seq_fn, *, tm, tn, name):
    t, n = x.shape[0], w_bf.shape[1]
    return pl.pallas_call(
        _nmm_body,
        grid=(t // tm, n // tn),
        in_specs=[
            pl.BlockSpec((tm, D_MODEL), lambda i, j: (i, 0)),
            pl.BlockSpec((1, D_MODEL), lambda i, j: (0, 0)),
            _mod_spec(0, seq_fn, 2),
            _mod_spec(1, seq_fn, 2),
            pl.BlockSpec((D_MODEL, tn), lambda i, j: (0, j)),
        ],
        out_specs=pl.BlockSpec((tm, tn), lambda i, j: (i, j)),
        out_shape=jax.ShapeDtypeStruct((t, n), F32),
        scratch_shapes=[pltpu.VMEM((tm, D_MODEL), BF16)],
        compiler_params=_cp(("parallel", "arbitrary")),
        name=name,
    )(x, g_row, modt_l, modt_l, w_bf)


def _mlp_body(x_ref, g_ref, sh_ref, sc_ref, gt_ref, w1_ref, w2_ref, o_ref, h_ref, acc_ref):
    k = pl.program_id(1)

    @pl.when(k == 0)
    def _():
        h = _rms(x_ref[...], g_ref[...]) * (1.0 + sc_ref[...]) + sh_ref[...]
        h_ref[...] = h.astype(BF16)
        acc_ref[...] = jnp.zeros_like(acc_ref)

    a = jnp.maximum(_dot(h_ref[...], w1_ref[...]), 0.0)
    acc_ref[...] += _dot((a * a).astype(BF16), w2_ref[...])

    @pl.when(k == pl.num_programs(1) - 1)
    def _():
        o_ref[...] = x_ref[...] + gt_ref[...] * acc_ref[...]


def _mlp(x, g_row, modt_l, w1_bf, w2_bf, seq_fn, *, tm, tf):
    t = x.shape[0]
    return pl.pallas_call(
        _mlp_body,
        grid=(t // tm, D_FF // tf),
        in_specs=[
            pl.BlockSpec((tm, D_MODEL), lambda i, k: (i, 0)),
            pl.BlockSpec((1, D_MODEL), lambda i, k: (0, 0)),
            _mod_spec(3, seq_fn, 2),
            _mod_spec(4, seq_fn, 2),
            _mod_spec(5, seq_fn, 2),
            pl.BlockSpec((D_MODEL, tf), lambda i, k: (0, k)),
            pl.BlockSpec((tf, D_MODEL), lambda i, k: (k, 0)),
        ],
        out_specs=pl.BlockSpec((tm, D_MODEL), lambda i, k: (i, 0)),
        out_shape=jax.ShapeDtypeStruct((t, D_MODEL), F32),
        scratch_shapes=[pltpu.VMEM((tm, D_MODEL), BF16), pltpu.VMEM((tm, D_MODEL), F32)],
        compiler_params=_cp(("parallel", "arbitrary")),
        name="mlp",
    )(x, g_row, modt_l, modt_l, modt_l, w1_bf, w2_bf)


def _fnorm_body(x_ref, g_ref, o_ref):
    o_ref[...] = _rms(x_ref[...], g_ref[...])


def _final_norm(x, g_row, *, tm):
    t = x.shape[0]
    return pl.pallas_call(
        _fnorm_body,
        grid=(t // tm,),
        in_specs=[pl.BlockSpec((tm, D_MODEL), lambda i: (i, 0)), pl.BlockSpec((1, D_MODEL), lambda i: (0, 0))],
        out_specs=pl.BlockSpec((tm, D_MODEL), lambda i: (i, 0)),
        out_shape=jax.ShapeDtypeStruct((t, D_MODEL), F32),
        compiler_params=_cp(("parallel",)),
        name="final_norm",
    )(x, g_row)


def _s5_operators(lam_re, lam_im, log_dt, b_re, b_im, c_re, c_im):
    hp = lax.Precision.HIGHEST
    t = S5_T
    dt = jnp.exp(log_dt)[..., None]
    ar, ai = lam_re * dt, lam_im * dt
    er = jnp.exp(ar)
    lbr, lbi = er * jnp.cos(ai), er * jnp.sin(ai)
    den = lam_re * lam_re + lam_im * lam_im
    fr = ((lbr - 1.0) * lam_re + lbi * lam_im) / den
    fi = (lbi * lam_re - (lbr - 1.0) * lam_im) / den
    bbr = fr[..., None] * b_re - fi[..., None] * b_im
    bbi = fr[..., None] * b_im + fi[..., None] * b_re
    k = jnp.arange(t + 1, dtype=F32)[:, None, None, None]
    pr = jnp.exp(k * ar[None]) * jnp.cos(k * ai[None])
    pi = jnp.exp(k * ar[None]) * jnp.sin(k * ai[None])
    pbr = pr[..., None] * bbr[None] - pi[..., None] * bbi[None]
    pbi = pr[..., None] * bbi[None] + pi[..., None] * bbr[None]
    pcr = c_re[None] * pr[:, :, :, None, :] - c_im[None] * pi[:, :, :, None, :]
    pci = c_re[None] * pi[:, :, :, None, :] + c_im[None] * pr[:, :, :, None, :]
    kern = (jnp.einsum("dgpn,kdgnq->kdgpq", c_re, pbr[:t], precision=hp)
            - jnp.einsum("dgpn,kdgnq->kdgpq", c_im, pbi[:t], precision=hp))
    eye = jnp.eye(S5_GPB, dtype=F32)
    jb, gpb = S5_JB, S5_GPB

    def blockdiag(a):
        x = a.reshape(a.shape[0], jb, gpb, a.shape[2], a.shape[3])
        x = x[:, :, :, :, None, :] * eye[None, None, :, None, :, None]
        return jnp.transpose(x, (1, 0, 2, 3, 4, 5))

    sidx = jnp.arange(t)
    fin_r = jnp.transpose(pbr[t - 1 - sidx, 0], (0, 1, 3, 2))
    fin_i = jnp.transpose(pbi[t - 1 - sidx, 0], (0, 1, 3, 2))
    bin_r = jnp.transpose(pbr[sidx, 1], (0, 1, 3, 2))
    bin_i = jnp.transpose(pbi[sidx, 1], (0, 1, 3, 2))
    bst = jnp.stack([blockdiag(m) for m in (fin_r, fin_i, bin_r, bin_i)], axis=4)
    bst = bst.reshape(jb, t * LANES, 4 * S5_SW)
    fo_r = jnp.transpose(pcr[sidx + 1, 0], (0, 1, 3, 2))
    fo_i = -jnp.transpose(pci[sidx + 1, 0], (0, 1, 3, 2))
    bo_r = jnp.transpose(pcr[t - sidx, 1], (0, 1, 3, 2))
    bo_i = -jnp.transpose(pci[t - sidx, 1], (0, 1, 3, 2))
    cst = jnp.stack([blockdiag(m) for m in (fo_r, fo_i, bo_r, bo_i)], axis=1)
    cst = jnp.transpose(cst, (0, 1, 3, 4, 2, 5, 6)).reshape(jb, 4 * S5_SW, t * LANES)
    dlt = sidx[None, :] - sidx[:, None]
    kf = kern[jnp.clip(dlt, 0, t - 1), 0]
    kb = kern[jnp.clip(-dlt, 0, t - 1), 1]
    d4 = dlt[:, :, None, None, None]
    ktoe = jnp.where(d4 > 0, kf, 0.0) + jnp.where(d4 < 0, kb, 0.0) + jnp.where(d4 == 0, kf + kb, 0.0)
    ktoe = jnp.transpose(ktoe, (0, 1, 2, 4, 3)).reshape(t * t, A_GROUPS, A_GROUP, A_GROUP)
    wt = blockdiag(ktoe).reshape(jb, t, t, gpb, A_GROUP, gpb, A_GROUP)
    wt = jnp.transpose(wt, (0, 1, 3, 4, 2, 5, 6)).reshape(jb, t * LANES, t * LANES)
    acoef = jnp.stack([pr[t, 0], pi[t, 0], pr[t, 1], pi[t, 1]], axis=0)
    acoef = acoef.reshape(4, jb, S5_SW).transpose(1, 0, 2).reshape(jb, 1, 4 * S5_SW)
    acoef = jnp.broadcast_to(acoef, (jb, S5_BG, 4 * S5_SW))
    return bst.astype(BF16), cst.astype(BF16), wt.astype(BF16), acoef


def _s5_in_body(u_ref, bst_ref, lhs_ref, s_ref, *, seq_len, row_tile):
    nc = seq_len // S5_T

    @pl.when(pl.program_id(2) == 0)
    def _():
        def build(cp, carry):
            base = cp * (2 * S5_T)
            for s in range(S5_T):
                a = u_ref[pl.ds(base + s, S5_BG, stride=seq_len), :]
                b = u_ref[pl.ds(base + S5_T + s, S5_BG, stride=seq_len), :]
                lhs_ref[pl.ds(pl.multiple_of(cp * 16, 16), 16), s * LANES:(s + 1) * LANES] = (
                    jnp.concatenate([a, b], axis=0).astype(BF16))
            return carry

        lax.fori_loop(0, nc // 2, build, 0)

    def mm(r, carry):
        rows = pl.ds(pl.multiple_of(r * row_tile, row_tile), row_tile)
        s_ref[rows, :] = _dot(lhs_ref[rows, :], bst_ref[...])
        return carry

    lax.fori_loop(0, (nc * S5_BG) // row_tile, mm, 0)


def _s5_state_in(proj2d, bst, *, nbg, seq_len):
    nc = seq_len // S5_T
    rows = nc * S5_BG
    kw = S5_T * LANES
    half = 2 * S5_SW
    row_tile = min(rows, 256)
    return pl.pallas_call(
        functools.partial(_s5_in_body, seq_len=seq_len, row_tile=row_tile),
        grid=(nbg, S5_JB, 2),
        in_specs=[
            pl.BlockSpec((S5_BG * seq_len, LANES), lambda g, j, n: (g, j)),
            pl.BlockSpec((None, kw, half), lambda g, j, n: (j, 0, n)),
        ],
        out_specs=[
            pl.BlockSpec((None, None, rows, kw), lambda g, j, n: (g, j, 0, 0)),
            pl.BlockSpec((None, None, rows, half), lambda g, j, n: (g, j, 0, n)),
        ],
        out_shape=[
            jax.ShapeDtypeStruct((nbg, S5_JB, rows, kw), BF16),
            jax.ShapeDtypeStruct((nbg, S5_JB, rows, 4 * S5_SW), F32),
        ],
        compiler_params=_cp(("parallel", "parallel", "arbitrary")),
        name="s5_state_in",
    )(proj2d, bst)


def _s5_scan_body(s_ref, a_ref, x0_ref, x_ref, fin_ref, *, nc):
    w = S5_SW
    afr, afi = a_ref[:, 0:w], a_ref[:, w:2 * w]
    abr, abi = a_ref[:, 2 * w:3 * w], a_ref[:, 3 * w:4 * w]

    def step(c, carry):
        xfr, xfi, xbr, xbi = carry
        rf = pl.ds(pl.multiple_of(c * S5_BG, S5_BG), S5_BG)
        rb = pl.ds(pl.multiple_of((nc - 1 - c) * S5_BG, S5_BG), S5_BG)
        x_ref[rf, 0:w] = xfr
        x_ref[rf, w:2 * w] = xfi
        x_ref[rb, 2 * w:3 * w] = xbr
        x_ref[rb, 3 * w:4 * w] = xbi
        nfr = afr * xfr - afi * xfi + s_ref[rf, 0:w]
        nfi = afr * xfi + afi * xfr + s_ref[rf, w:2 * w]
        nbr = abr * xbr - abi * xbi + s_ref[rb, 2 * w:3 * w]
        nbi = abr * xbi + abi * xbr + s_ref[rb, 3 * w:4 * w]
        return nfr, nfi, nbr, nbi

    init = (x0_ref[:, 0:w], x0_ref[:, w:2 * w], x0_ref[:, 2 * w:3 * w], x0_ref[:, 3 * w:4 * w])
    xfr, xfi, xbr, xbi = lax.fori_loop(0, nc, step, init)
    fin_ref[:, 0:w] = xfr
    fin_ref[:, w:2 * w] = xfi
    fin_ref[:, 2 * w:3 * w] = xbr
    fin_ref[:, 3 * w:4 * w] = xbi


def _s5_scan(s_all, acoef, x0, *, nbg, seq_len):
    nc = seq_len // S5_T
    rows = nc * S5_BG
    sw4 = 4 * S5_SW
    return pl.pallas_call(
        functools.partial(_s5_scan_body, nc=nc),
        grid=(nbg, S5_JB),
        in_specs=[
            pl.BlockSpec((None, None, rows, sw4), lambda g, j: (g, j, 0, 0)),
            pl.BlockSpec((None, S5_BG, sw4), lambda g, j: (j, 0, 0)),
            pl.BlockSpec((None, None, S5_BG, sw4), lambda g, j: (g, j, 0, 0)),
        ],
        out_specs=[
            pl.BlockSpec((None, None, rows, sw4), lambda g, j: (g, j, 0, 0)),
            pl.BlockSpec((None, None, S5_BG, sw4), lambda g, j: (g, j, 0, 0)),
        ],
        out_shape=[
            jax.ShapeDtypeStruct((nbg, S5_JB, rows, sw4), F32),
            jax.ShapeDtypeStruct((nbg, S5_JB, S5_BG, sw4), F32),
        ],
        compiler_params=_cp(("parallel", "parallel")),
        name="s5_scan",
    )(s_all, acoef, x0)


def _s5_out_body(lhs_ref, x_ref, wt_ref, cst_ref, y_ref, acc_ref, *, chunks):
    acc_ref[...] = _dot(lhs_ref[...], wt_ref[...]) + _dot(x_ref[...].astype(BF16), cst_ref[...])

    def unchunk(c, carry):
        rows = pl.ds(pl.multiple_of(c * S5_BG, S5_BG), S5_BG)
        for t in range(S5_T):
            y_ref[:, c * S5_T + t, :] = acc_ref[rows, t * LANES:(t + 1) * LANES]
        return carry

    lax.fori_loop(0, chunks, unchunk, 0)


def _s5_output(lhs, x_all, wt, cst, *, nbg, seq_len):
    nc = seq_len // S5_T
    chunks = min(nc, 32)
    tr = chunks * S5_BG
    kw = S5_T * LANES
    sw4 = 4 * S5_SW
    return pl.pallas_call(
        functools.partial(_s5_out_body, chunks=chunks),
        grid=(nbg, S5_JB, nc // chunks),
        in_specs=[
            pl.BlockSpec((None, None, tr, kw), lambda g, j, r: (g, j, r, 0)),
            pl.BlockSpec((None, None, tr, sw4), lambda g, j, r: (g, j, r, 0)),
            pl.BlockSpec((None, kw, kw), lambda g, j, r: (j, 0, 0)),
            pl.BlockSpec((None, sw4, kw), lambda g, j, r: (j, 0, 0)),
        ],
        out_specs=pl.BlockSpec((S5_BG, chunks * S5_T, LANES), lambda g, j, r: (g, r, j)),
        out_shape=jax.ShapeDtypeStruct((nbg * S5_BG, seq_len, MIX_A), F32),
        scratch_shapes=[pltpu.VMEM((tr, kw), F32)],
        compiler_params=_cp(("parallel", "parallel", "arbitrary")),
        name="s5_output",
    )(lhs, x_all, wt, cst)


def _s5_mix(proj2d, ops, x0, *, nbg, seq_len):
    bst, cst, wt, acoef = ops
    lhs, s_all = _s5_state_in(proj2d, bst, nbg=nbg, seq_len=seq_len)
    x_all, fin = _s5_scan(s_all, acoef, x0, nbg=nbg, seq_len=seq_len)
    y = _s5_output(lhs, x_all, wt, cst, nbg=nbg, seq_len=seq_len)
    return y, fin


def _s5_pack_state(re, im):
    b = re.shape[0]
    x = jnp.stack([re[:, 0], im[:, 0], re[:, 1], im[:, 1]], axis=1)
    x = x.reshape(b // S5_BG, S5_BG, 4, S5_JB, S5_SW)
    return jnp.transpose(x, (0, 3, 1, 2, 4)).reshape(b // S5_BG, S5_JB, S5_BG, 4 * S5_SW)


def _s5_unpack_state(fin):
    nbg = fin.shape[0]
    x = fin.reshape(nbg, S5_JB, S5_BG, 4, S5_SW)
    x = jnp.transpose(x, (0, 2, 3, 1, 4)).reshape(nbg * S5_BG, 4, A_GROUPS, S5_N)
    return jnp.stack([x[:, 0], x[:, 2]], axis=1), jnp.stack([x[:, 1], x[:, 3]], axis=1)


Q_BLK, K_BLK, V_BLK = MIX_A // LANES, (MIX_A + MIX_B) // LANES, (MIX_A + 2 * MIX_B) // LANES


def _pair_softmax_pv(q, parts):
    lane = lax.broadcasted_iota(jnp.int32, (1, LANES), 1)
    outs = []
    for half in range(2):
        sel = (lane < NA_HD) if half == 0 else (lane >= NA_HD)
        qm = jnp.where(sel, q, 0.0).astype(BF16)
        scores = []
        for k, _, bias in parts:
            s = _dot_nt(qm, k)
            if bias is not None:
                s = s + bias[half]
            scores.append(s)
        m = scores[0].max(axis=-1, keepdims=True)
        for s in scores[1:]:
            m = jnp.maximum(m, s.max(axis=-1, keepdims=True))
        l = None
        o = None
        for s, (_, v, _) in zip(scores, parts):
            p = jnp.exp(s - m)
            ls = p.sum(axis=-1, keepdims=True)
            os_ = _dot(p.astype(BF16), v)
            l = ls if l is None else l + ls
            o = os_ if o is None else o + os_
        outs.append(o / l)
    return jnp.where(lane < NA_HD, outs[0], outs[1])


def _ctx_attn_body(q_ref, k_ref, v_ref, o_ref):
    q = q_ref[...] * (NA_HD ** -0.5)
    o_ref[...] = _pair_softmax_pv(q, [(k_ref[...].astype(BF16), v_ref[...].astype(BF16), None)])


def _context_attention(proj, *, nb, seq_len):
    npair = NA_HEADS // 2
    return pl.pallas_call(
        _ctx_attn_body,
        grid=(nb, npair),
        in_specs=[
            pl.BlockSpec((seq_len, LANES), lambda b, h: (b, Q_BLK + h)),
            pl.BlockSpec((seq_len, LANES), lambda b, h: (b, K_BLK + h)),
            pl.BlockSpec((seq_len, LANES), lambda b, h: (b, V_BLK + h)),
        ],
        out_specs=pl.BlockSpec((seq_len, LANES), lambda b, h: (b, h)),
        out_shape=jax.ShapeDtypeStruct((nb * seq_len, MIX_B), F32),
        compiler_params=_cp(("parallel", "parallel")),
        name="context_attention",
    )(proj, proj, proj)


NA_QROWS = 4
NA_WROWS = NA_QROWS + NA_KH


def _na_window_start(g, rows):
    rs = jnp.clip(g * NA_QROWS - NA_KH // 2, 0, rows - NA_KH)
    return jnp.minimum(rs, rows - NA_WROWS)


def _na_bias_tables(rpb, rows):
    import numpy as np

    ngroups = rows // NA_QROWS
    tabs = []
    for g in (0, 1, ngroups - 1):
        rs0 = int(np.clip(g * NA_QROWS - NA_KH // 2, 0, rows - NA_KH))
        ws = min(rs0, rows - NA_WROWS)
        ri, c, wi, kc = np.meshgrid(np.arange(NA_QROWS), np.arange(GRID_W), np.arange(NA_WROWS),
                                    np.arange(GRID_W), indexing="ij")
        r = g * NA_QROWS + ri
        rs = np.clip(r - NA_KH // 2, 0, rows - NA_KH)
        row = ws + wi
        cs = np.clip(c - NA_KW // 2, 0, GRID_W - NA_KW)
        valid = (row >= rs) & (row < rs + NA_KH) & (kc >= cs) & (kc < cs + NA_KW)
        dr = np.clip(row - r + NA_KH - 1, 0, 2 * NA_KH - 2)
        dc = np.clip(kc - c + NA_KW - 1, 0, 2 * NA_KW - 2)
        shp = (NA_QROWS * GRID_W, NA_WROWS * GRID_W)
        b = rpb[:, dr.reshape(shp), dc.reshape(shp)]
        tabs.append(jnp.where(jnp.asarray(valid.reshape(shp))[None], b, NEG_BIG))
    t = jnp.stack(tabs, axis=1)
    t = t.reshape(NA_HEADS // 2, 2, 3, *t.shape[2:])
    return jnp.transpose(t, (0, 2, 1, 3, 4))


def _nbr_attn_body(q_ref, k_ref, v_ref, ck_ref, cv_ref, bias_ref, o_ref, *, rows):
    g = pl.program_id(2)
    ngroups = rows // NA_QROWS
    case = jnp.where(g == 0, 0, jnp.where(g == ngroups - 1, 2, 1))
    ws = _na_window_start(g, rows)
    win = pl.ds(pl.multiple_of(ws * GRID_W, GRID_W), NA_WROWS * GRID_W)
    q = q_ref[...] * (NA_HD ** -0.5)
    kw = k_ref[win, :].astype(BF16)
    vw = v_ref[win, :].astype(BF16)
    bias = (bias_ref[case, 0], bias_ref[case, 1])
    o_ref[...] = _pair_softmax_pv(q, [(kw, vw, bias),
                                      (ck_ref[...].astype(BF16), cv_ref[...].astype(BF16), None)])


def _neighbourhood_attention(proj, ck, cv, bias, *, nb, seq_len):
    rows = seq_len // GRID_W
    ngroups = rows // NA_QROWS
    npair = NA_HEADS // 2
    qt = NA_QROWS * GRID_W
    past = ck.shape[1]
    return pl.pallas_call(
        functools.partial(_nbr_attn_body, rows=rows),
        grid=(npair, nb, ngroups),
        in_specs=[
            pl.BlockSpec((qt, LANES), lambda h, b, g: (b * ngroups + g, Q_BLK + h)),
            pl.BlockSpec((seq_len, LANES), lambda h, b, g: (b, K_BLK + h)),
            pl.BlockSpec((seq_len, LANES), lambda h, b, g: (b, V_BLK + h)),
            pl.BlockSpec((None, past, LANES), lambda h, b, g: (b, 0, h)),
            pl.BlockSpec((None, past, LANES), lambda h, b, g: (b, 0, h)),
            pl.BlockSpec((None, 3, 2, qt, NA_WROWS * GRID_W), lambda h, b, g: (h, 0, 0, 0, 0)),
        ],
        out_specs=pl.BlockSpec((qt, LANES), lambda h, b, g: (b * ngroups + g, h)),
        out_shape=jax.ShapeDtypeStruct((nb * seq_len, MIX_B), F32),
        compiler_params=_cp(("parallel", "parallel", "parallel")),
        name="neighbourhood_attention",
    )(proj, proj, proj, ck, cv, bias)


def _even_out_body(x_ref, y_ref, u_ref, bo_ref, d_ref, wg_ref, bg_ref, wo_ref, gt_ref, o_ref):
    z = _gelu_tanh(y_ref[...] + d_ref[...] * u_ref[...])
    a = z * _sigmoid(_dot(z.astype(BF16), wg_ref[...]) + bg_ref[...])
    cat = jnp.concatenate([a, bo_ref[...]], axis=-1).astype(BF16)
    o_ref[...] = x_ref[...] + gt_ref[...] * _dot(cat, wo_ref[...])


def _even_out(x, y, proj, bo, d_row, wg_bf, bg_row, wo_bf, modt_l, seq_fn, *, tm):
    t = x.shape[0]
    return pl.pallas_call(
        _even_out_body,
        grid=(t // tm,),
        in_specs=[
            pl.BlockSpec((tm, D_MODEL), lambda i: (i, 0)),
            pl.BlockSpec((tm, MIX_A), lambda i: (i, 0)),
            pl.BlockSpec((tm, MIX_A), lambda i: (i, 0)),
            pl.BlockSpec((tm, MIX_B), lambda i: (i, 0)),
            pl.BlockSpec((1, MIX_A), lambda i: (0, 0)),
            pl.BlockSpec((MIX_A, MIX_A), lambda i: (0, 0)),
            pl.BlockSpec((1, MIX_A), lambda i: (0, 0)),
            pl.BlockSpec((D_MODEL, D_MODEL), lambda i: (0, 0)),
            _mod_spec(2, seq_fn, 1),
        ],
        out_specs=pl.BlockSpec((tm, D_MODEL), lambda i: (i, 0)),
        out_shape=jax.ShapeDtypeStruct((t, D_MODEL), F32),
        compiler_params=_cp(("parallel",)),
        name="even_out",
    )(x, y, proj, bo, d_row, wg_bf, bg_row, wo_bf, modt_l)


DN_QKV = 3 * D_MODEL
DN_PROJ_PAD = 4 * D_MODEL + LANES
DN_GATE_BLK = (4 * D_MODEL) // LANES
DN_CB = 512
DN_HALO = SUBLANES


def _gdn_pre_body(x_ref, w_ref, o_ref, pad_ref, *, seq_len, rt):
    kind = pl.program_id(1) // (D_MODEL // DN_CB)
    zeros = jnp.zeros((DN_HALO, DN_CB), F32)
    pad_ref[0:DN_HALO, :] = zeros
    pad_ref[DN_HALO + seq_len:DN_HALO + seq_len + DN_HALO, :] = zeros
    pad_ref[DN_HALO:DN_HALO + seq_len, :] = x_ref[...]
    qscale = jnp.where(kind == 0, DN_DK ** -0.5, 1.0)
    lo = DN_HALO - (DN_CONV - 1) // 2

    def tile(r, carry):
        base = pl.multiple_of(r * rt, rt)
        xt = pad_ref[pl.ds(base, rt + 2 * DN_HALO), :]
        y = None
        for k in range(DN_CONV):
            term = w_ref[k:k + 1, :] * xt[lo + k:lo + k + rt, :]
            y = term if y is None else y + term
        y = _silu(y)
        segs = []
        for h in range(DN_CB // DN_DK):
            seg = y[:, h * DN_DK:(h + 1) * DN_DK]
            nrm = seg * lax.rsqrt(jnp.sum(seg * seg, axis=-1, keepdims=True) + EPS) * qscale
            segs.append(jnp.where(kind == 2, seg, nrm))
        o_ref[pl.ds(base, rt), :] = jnp.concatenate(segs, axis=-1)
        return carry

    lax.fori_loop(0, seq_len // rt, tile, 0)


def _gdn_pre(proj, conv_w, *, nb, seq_len):
    rt = 256
    return pl.pallas_call(
        functools.partial(_gdn_pre_body, seq_len=seq_len, rt=rt),
        grid=(nb, DN_QKV // DN_CB),
        in_specs=[
            pl.BlockSpec((seq_len, DN_CB), lambda b, c: (b, c)),
            pl.BlockSpec((DN_CONV, DN_CB), lambda b, c: (0, c)),
        ],
        out_specs=pl.BlockSpec((seq_len, DN_CB), lambda b, c: (b, c)),
        out_shape=jax.ShapeDtypeStruct((nb * seq_len, DN_QKV), F32),
        scratch_shapes=[pltpu.VMEM((seq_len + 2 * DN_HALO, DN_CB), F32)],
        compiler_params=_cp(("parallel", "parallel")),
        name="gdn_pre",
    )(proj, conv_w)


def _split_bf16(x, n):
    parts = []
    r = x
    for _ in range(n):
        p = r.astype(BF16)
        parts.append(p)
        r = r - p.astype(F32)
    return parts


def _mm(a, b, passes):
    if passes == 1:
        return _dot(a.astype(BF16), b.astype(BF16))
    a_hi, a_lo = _split_bf16(a, 2)
    b_hi, b_lo = _split_bf16(b, 2)
    return _dot(a_hi, b_hi) + _dot(a_hi, b_lo) + _dot(a_lo, b_hi)


def _mm_exact_lhs(a_bf, b):
    return sum(_dot(a_bf, p) for p in _split_bf16(b, 3))


def _softplus(x):
    return jnp.maximum(x, 0.0) + jnp.log(1.0 + jnp.exp(-jnp.abs(x)))


def _gdn_body(q_ref, k_ref, v_ref, gt_ref, ga_ref, gb_ref, s0_ref, o_ref, sl_ref, s_scr, vn_scr,
              *, direction, inv_passes):
    t = pl.program_id(1)
    nt = pl.num_programs(1)
    tile = GDN_TILE
    c_len = DN_CHUNK
    nchunk = tile // c_len

    @pl.when(t == 0)
    def _():
        s_scr[...] = s0_ref[...]
        vn_scr[...] = jnp.zeros_like(vn_scr)

    ri = lax.broadcasted_iota(jnp.int32, (tile, tile), 0)
    ci = lax.broadcasted_iota(jnp.int32, (tile, tile), 1)
    same = (ri // c_len) == (ci // c_len)
    if direction == 0:
        incl = same & (ci <= ri)
        strict = same & (ci < ri)
    else:
        incl = same & (ci >= ri)
        strict = same & (ci > ri)
    tri_bf = jnp.where(incl, 1.0, 0.0).astype(BF16)
    blk_bf = jnp.where(same, 1.0, 0.0).astype(BF16)
    late, early = (ri, ci) if direction == 0 else (ci, ri)
    off_masks = [(((late >> lvl) & 1) == 1) & (((early >> lvl) & 1) == 0) & ((ri >> (lvl + 1)) == (ci >> (lvl + 1)))
                 for lvl in range(int(math.log2(c_len)))]

    gt = gt_ref[...]
    g_all = ga_ref[...] * _softplus(gt + gb_ref[...])
    beta_all = _sigmoid(gt)
    gc_all = _mm_exact_lhs(tri_bf, g_all)
    gl_all = _mm_exact_lhs(blk_bf, g_all)
    gc_t = jnp.transpose(gc_all)
    chunk_order = range(nchunk) if direction == 0 else range(nchunk - 1, -1, -1)

    for h in range(DN_HEADS):
        gl_ = direction * DN_HEADS + h
        bl_ = 2 * DN_HEADS + gl_
        hs = slice(h * DN_DK, (h + 1) * DN_DK)
        gcol = gc_all[:, gl_:gl_ + 1]
        grow = gc_t[gl_:gl_ + 1, :]
        glcol = gl_all[:, gl_:gl_ + 1]
        bcol = beta_all[:, bl_:bl_ + 1]
        qh, kh, vh = q_ref[:, hs], k_ref[:, hs], v_ref[:, hs]
        k_bf = kh.astype(BF16)
        decay = jnp.exp(jnp.where(incl, gcol - grow, NEG_BIG))
        a_mat = jnp.where(strict, bcol * _dot_nt(k_bf, k_bf) * decay, 0.0)
        attn = (_dot_nt(qh.astype(BF16), k_bf) * decay).astype(BF16)
        egc = jnp.exp(gcol)
        t_inv = jnp.where(ri == ci, 1.0, 0.0) - jnp.where(off_masks[0], a_mat, 0.0)
        for lvl in range(1, len(off_masks)):
            t_inv = t_inv - _mm(t_inv, _mm(jnp.where(off_masks[lvl], a_mat, 0.0), t_inv, inv_passes), inv_passes)
        x = _mm(t_inv, jnp.concatenate([vh * bcol, kh * (bcol * egc)], axis=-1), inv_passes)
        u_mat, w_bf = x[:, :DN_DV], x[:, DN_DV:].astype(BF16)
        qg_bf = (qh * egc).astype(BF16)
        kg_bf = (kh * jnp.exp(glcol - gcol)).astype(BF16)
        s = s_scr[h]
        for c in chunk_order:
            rows = slice(c * c_len, (c + 1) * c_len)
            s_bf = s.astype(BF16)
            v_new = u_mat[rows] - _dot(w_bf[rows], s_bf)
            vn_scr[h, rows, :] = v_new.astype(BF16)
            o_ref[rows, hs] = _dot(qg_bf[rows], s_bf) + _dot(attn[rows], vn_scr[h])
            s = s * jnp.exp(glcol[c * c_len:c * c_len + 1, :]) + _dot_tn(kg_bf[rows], v_new.astype(BF16))
        s_scr[h] = s

    @pl.when(t == nt - 1)
    def _():
        sl_ref[...] = s_scr[...]


def _gdn_scan(qkv, proj, ga_row, gb_row, s0, *, nb, seq_len, direction, inv_passes):
    nt = seq_len // GDN_TILE
    tile = GDN_TILE
    nq = D_MODEL // D_MODEL

    def row(b, t):
        return b * nt + (t if direction == 0 else nt - 1 - t)

    return pl.pallas_call(
        functools.partial(_gdn_body, direction=direction, inv_passes=inv_passes),
        grid=(nb, nt),
        in_specs=[
            pl.BlockSpec((tile, D_MODEL), lambda b, t: (row(b, t), 0)),
            pl.BlockSpec((tile, D_MODEL), lambda b, t: (row(b, t), nq)),
            pl.BlockSpec((tile, D_MODEL), lambda b, t: (row(b, t), 2 * nq)),
            pl.BlockSpec((tile, LANES), lambda b, t: (row(b, t), DN_GATE_BLK)),
            pl.BlockSpec((1, LANES), lambda b, t: (0, 0)),
            pl.BlockSpec((1, LANES), lambda b, t: (0, 0)),
            pl.BlockSpec((None, DN_HEADS, DN_DK, DN_DV), lambda b, t: (b, 0, 0, 0)),
        ],
        out_specs=[
            pl.BlockSpec((tile, D_MODEL), lambda b, t: (row(b, t), 0)),
            pl.BlockSpec((None, DN_HEADS, DN_DK, DN_DV), lambda b, t: (b, 0, 0, 0)),
        ],
        out_shape=[
            jax.ShapeDtypeStruct((nb * seq_len, D_MODEL), F32),
            jax.ShapeDtypeStruct((nb, DN_HEADS, DN_DK, DN_DV), F32),
        ],
        scratch_shapes=[pltpu.VMEM((DN_HEADS, DN_DK, DN_DV), F32), pltpu.VMEM((DN_HEADS, tile, DN_DV), BF16)],
        compiler_params=_cp(("parallel", "arbitrary")),
        name="gdn_scan_fwd" if direction == 0 else "gdn_scan_bwd",
    )(qkv, qkv, qkv, proj, ga_row, gb_row, s0)


def _odd_out_body(x_ref, of_ref, ob_ref, z_ref, ng_ref, wo_ref, gt_ref, o_ref):
    o = of_ref[...] + ob_ref[...]
    segs = []
    for h in range(DN_HEADS):
        seg = o[:, h * DN_DV:(h + 1) * DN_DV]
        segs.append(seg * lax.rsqrt(jnp.mean(seg * seg, axis=-1, keepdims=True) + EPS))
    y = jnp.concatenate(segs, axis=-1) * ng_ref[...] * _silu(z_ref[...])
    o_ref[...] = x_ref[...] + gt_ref[...] * _dot(y.astype(BF16), wo_ref[...])


def _odd_out(x, o_f, o_b, proj, ng_row, wo_bf, modt_l, seq_fn, *, tm):
    t = x.shape[0]
    zblk = DN_QKV // D_MODEL
    return pl.pallas_call(
        _odd_out_body,
        grid=(t // tm,),
        in_specs=[
            pl.BlockSpec((tm, D_MODEL), lambda i: (i, 0)),
            pl.BlockSpec((tm, D_MODEL), lambda i: (i, 0)),
            pl.BlockSpec((tm, D_MODEL), lambda i: (i, 0)),
            pl.BlockSpec((tm, D_MODEL), lambda i: (i, zblk)),
            pl.BlockSpec((1, D_MODEL), lambda i: (0, 0)),
            pl.BlockSpec((D_MODEL, D_MODEL), lambda i: (0, 0)),
            _mod_spec(2, seq_fn, 1),
        ],
        out_specs=pl.BlockSpec((tm, D_MODEL), lambda i: (i, 0)),
        out_shape=jax.ShapeDtypeStruct((t, D_MODEL), F32),
        compiler_params=_cp(("parallel",)),
        name="odd_out",
    )(x, o_f, o_b, proj, ng_row, wo_bf, modt_l)


ROW_TILE = 512


def _seq_fn(slot0, seq_len, tm):
    if slot0 == 0:
        return lambda i: 0
    return lambda i: slot0 + (i * tm) // seq_len


def kernel(x_prompt, x_sample, c, cache_na_k, cache_na_v, state_s5_re, state_s5_im, state_dn, c_ctx, norm_mix_g, norm_ff_g, w_mod, b_mod, w_ff1, w_ff2, w_in_e, w_out_e, s5_lam_re, s5_lam_im, s5_log_dt, s5_b_re, s5_b_im, s5_c_re, s5_c_im, s5_d, s5_w_glu, s5_b_glu, na_rpb, w_in_o, dn_conv_w, dn_a_log, dn_dt_bias, dn_norm_g, w_out_o, final_norm_g):
    nbp, lp = x_prompt.shape[:2]
    nbs, ls = x_sample.shape[:2]
    assert nbp % S5_BG == 0 and nbs % S5_BG == 0 and 1 + nbs <= 16
    streams = [
        dict(x=x_prompt.reshape(nbp * lp, D_MODEL), nb=nbp, L=lp, slot0=0),
        dict(x=x_sample.reshape(nbs * ls, D_MODEL), nb=nbs, L=ls, slot0=1),
    ]
    cond16 = jnp.zeros((16, D_MODEL), F32).at[0].set(c_ctx).at[1:1 + nbs].set(c)
    mod = _modulation(cond16, w_mod.astype(BF16), b_mod[:, None, :])
    modt = jnp.transpose(mod.reshape(DEPTH, 16, 6, D_MODEL), (0, 2, 1, 3))[:, :, :, None, :]

    new_k, new_v, new_sre, new_sim, new_dn = [], [], [], [], []
    for l in range(DEPTH):
        e = l // 2
        if l % 2 == 0:
            w_in = w_in_e[e].astype(BF16)
            w_out = w_out_e[e].astype(BF16)
            w_glu = s5_w_glu[e].astype(BF16)
            ops = _s5_operators(s5_lam_re[e], s5_lam_im[e], s5_log_dt[e], s5_b_re[e], s5_b_im[e],
                                s5_c_re[e], s5_c_im[e])
            bias = _na_bias_tables(na_rpb[e], ls // GRID_W)
        else:
            w_in = jnp.pad(w_in_o[e], ((0, 0), (0, DN_PROJ_PAD - w_in_o.shape[-1]))).astype(BF16)
            w_out = w_out_o[e].astype(BF16)
            ga_row = jnp.zeros((1, LANES), F32).at[0, :2 * DN_HEADS].set(-jnp.exp(dn_a_log[e]).reshape(-1))
            gb_row = jnp.zeros((1, LANES), F32).at[0, :2 * DN_HEADS].set(dn_dt_bias[e].reshape(-1))
            ng_row = jnp.tile(dn_norm_g[e], DN_HEADS)[None]
        w1 = w_ff1[l].astype(BF16)
        w2 = w_ff2[l].astype(BF16)
        for st in streams:
            x, nb, sl = st["x"], st["nb"], st["L"]
            seq = _seq_fn(st["slot0"], sl, ROW_TILE)
            is_ctx = st["slot0"] == 0
            if l % 2 == 0:
                proj = _norm_mod_matmul(x, norm_mix_g[l][None], modt[l], w_in, seq, tm=ROW_TILE, tn=1024,
                                        name="in_proj_even")
                nbg = nb // S5_BG
                if is_ctx:
                    x0 = jnp.zeros((nbg, S5_JB, S5_BG, 4 * S5_SW), F32)
                else:
                    x0 = _s5_pack_state(state_s5_re[:, e], state_s5_im[:, e])
                y, fin = _s5_mix(proj, ops, x0, nbg=nbg, seq_len=sl)
                if is_ctx:
                    bo = _context_attention(proj, nb=nb, seq_len=sl)
                    kv = proj.reshape(nb, sl, -1)
                    new_k.append(kv[:, :, MIX_A + MIX_B:MIX_A + 2 * MIX_B].reshape(nb, sl, NA_HEADS, NA_HD))
                    new_v.append(kv[:, :, MIX_A + 2 * MIX_B:].reshape(nb, sl, NA_HEADS, NA_HD))
                    sre, sim = _s5_unpack_state(fin)
                    new_sre.append(sre)
                    new_sim.append(sim)
                else:
                    past = cache_na_k.shape[2]
                    ck = cache_na_k[:, e].reshape(nb, past, MIX_B)
                    cv = cache_na_v[:, e].reshape(nb, past, MIX_B)
                    bo = _neighbourhood_attention(proj, ck, cv, bias, nb=nb, seq_len=sl)
                x = _even_out(x, y.reshape(nb * sl, MIX_A), proj, bo, s5_d[e][None], w_glu, s5_b_glu[e][None],
                              w_out, modt[l], seq, tm=ROW_TILE)
            else:
                proj = _norm_mod_matmul(x, norm_mix_g[l][None], modt[l], w_in, seq, tm=ROW_TILE,
                                        tn=DN_PROJ_PAD // 3, name="in_proj_odd")
                qkv = _gdn_pre(proj, dn_conv_w[e], nb=nb, seq_len=sl)
                if is_ctx:
                    s0 = jnp.zeros((nb, 2, DN_HEADS, DN_DK, DN_DV), F32)
                else:
                    s0 = state_dn[:, e]
                o_f, s_f = _gdn_scan(qkv, proj, ga_row, gb_row, s0[:, 0], nb=nb, seq_len=sl, direction=0,
                                     inv_passes=1)
                o_b, s_b = _gdn_scan(qkv, proj, ga_row, gb_row, s0[:, 1], nb=nb, seq_len=sl, direction=1,
                                     inv_passes=1)
                if is_ctx:
                    new_dn.append(jnp.stack([s_f, s_b], axis=1))
                x = _odd_out(x, o_f, o_b, proj, ng_row, w_out, modt[l], seq, tm=ROW_TILE)
            st["x"] = _mlp(x, norm_ff_g[l][None], modt[l], w1, w2, seq, tm=ROW_TILE, tf=1024)

    y_prompt = _final_norm(streams[0]["x"], final_norm_g[None], tm=ROW_TILE).reshape(nbp, lp, D_MODEL)
    y_sample = _final_norm(streams[1]["x"], final_norm_g[None], tm=ROW_TILE).reshape(nbs, ls, D_MODEL)
    return (y_prompt, y_sample, jnp.stack(new_k, axis=1), jnp.stack(new_v, axis=1),
            jnp.stack(new_sre, axis=1), jnp.stack(new_sim, axis=1), jnp.stack(new_dn, axis=1))
```

```python
import functools
import math

import jax
import jax.numpy as jnp
from jax import lax
from jax.experimental import pallas as pl
from jax.experimental.pallas import tpu as pltpu

F32 = jnp.float32
BF16 = jnp.bfloat16

D_MODEL = 1024
DEPTH = 2
GRID_W = 64
MIX_A = D_MODEL // 2
A_GROUP = 16
A_GROUPS = MIX_A // A_GROUP
S5_N = 64
MIX_B = D_MODEL - MIX_A
NA_HEADS = 8
NA_HD = MIX_B // NA_HEADS
NA_KH = 8
NA_KW = 16
DN_HEADS = 8
DN_DK = D_MODEL // DN_HEADS
DN_DV = D_MODEL // DN_HEADS
DN_CONV = 5
DN_CHUNK = 64
D_FF = 4 * D_MODEL
EPS = 1e-6

LANES = 128
SUBLANES = 8
VMEM_LIMIT = 56 * 1024 * 1024
NEG_BIG = -1e30

S5_T = 16
S5_BG = SUBLANES
S5_JB = MIX_A // LANES
S5_GPB = LANES // A_GROUP
S5_SW = S5_GPB * S5_N
GDN_TILE = 256


def _cp(sem, vmem=VMEM_LIMIT):
    return pltpu.CompilerParams(dimension_semantics=sem, vmem_limit_bytes=vmem)


def _rms(x, g):
    return x * lax.rsqrt(jnp.mean(x * x, axis=-1, keepdims=True) + EPS) * g


def _sigmoid(x):
    return 1.0 / (1.0 + jnp.exp(-x))


def _silu(x):
    return x * _sigmoid(x)


def _gelu_tanh(x):
    return 0.5 * x * (1.0 + jnp.tanh(math.sqrt(2.0 / math.pi) * (x + 0.044715 * (x * x * x))))


def _dot(a, b):
    return jnp.dot(a, b, preferred_element_type=F32)


def _dot_nt(a, b):
    return lax.dot_general(a, b, (((1,), (1,)), ((), ())), preferred_element_type=F32)


def _dot_tn(a, b):
    return lax.dot_general(a, b, (((0,), (0,)), ((), ())), preferred_element_type=F32)


def _mod_body(c_ref, w_ref, b_ref, o_ref):
    c = c_ref[...]
    o_ref[...] = _dot(_silu(c).astype(BF16), w_ref[...]) + b_ref[...]


def _modulation(cond16, w_mod_bf, b_mod):
    tn = 1536
    n = 6 * D_MODEL
    return pl.pallas_call(
        _mod_body,
        grid=(DEPTH, n // tn),
        in_specs=[
            pl.BlockSpec((16, D_MODEL), lambda l, j: (0, 0)),
            pl.BlockSpec((None, D_MODEL, tn), lambda l, j: (l, 0, j)),
            pl.BlockSpec((None, 1, tn), lambda l, j: (l, 0, j)),
        ],
        out_specs=pl.BlockSpec((None, 16, tn), lambda l, j: (l, 0, j)),
        out_shape=jax.ShapeDtypeStruct((DEPTH, 16, n), F32),
        compiler_params=_cp(("parallel", "parallel")),
        name="modulation",
    )(cond16, w_mod_bf, b_mod)


def _mod_spec(which, seq_fn, ngrid):
    if ngrid == 1:
        return pl.BlockSpec((None, None, 1, D_MODEL), lambda i: (which, seq_fn(i), 0, 0))
    return pl.BlockSpec((None, None, 1, D_MODEL), lambda i, j: (which, seq_fn(i), 0, 0))


def _nmm_body(x_ref, g_ref, sh_ref, sc_ref, w_ref, o_ref, h_ref):
    @pl.when(pl.program_id(1) == 0)
    def _():
        h = _rms(x_ref[...], g_ref[...]) * (1.0 + sc_ref[...]) + sh_ref[...]
        h_ref[...] = h.astype(BF16)

    o_ref[...] = _dot(h_ref[...], w_ref[...]).astype(o_ref.dtype)


def _norm_mod_matmul(x, g_row, modt_l, w_bf, seq_fn, *, tm, tn, name):
    t, n = x.shape[0], w_bf.shape[1]
    return pl.pallas_call(
        _nmm_body,
        grid=(t // tm, n // tn),
        in_specs=[
            pl.BlockSpec((tm, D_MODEL), lambda i, j: (i, 0)),
            pl.BlockSpec((1, D_MODEL), lambda i, j: (0, 0)),
            _mod_spec(0, seq_fn, 2),
            _mod_spec(1, seq_fn, 2),
            pl.BlockSpec((D_MODEL, tn), lambda i, j: (0, j)),
        ],
        out_specs=pl.BlockSpec((tm, tn), lambda i, j: (i, j)),
        out_shape=jax.ShapeDtypeStruct((t, n), F32),
        scratch_shapes=[pltpu.VMEM((tm, D_MODEL), BF16)],
        compiler_params=_cp(("parallel", "arbitrary")),
        name=name,
    )(x, g_row, modt_l, modt_l, w_bf)


def _mlp_body(x_ref, g_ref, sh_ref, sc_ref, gt_ref, w1_ref, w2_ref, o_ref, h_ref, acc_ref):
    k = pl.program_id(1)

    @pl.when(k == 0)
    def _():
        h = _rms(x_ref[...], g_ref[...]) * (1.0 + sc_ref[...]) + sh_ref[...]
        h_ref[...] = h.astype(BF16)
        acc_ref[...] = jnp.zeros_like(acc_ref)

    a = jnp.maximum(_dot(h_ref[...], w1_ref[...]), 0.0)
    acc_ref[...] += _dot((a * a).astype(BF16), w2_ref[...])

    @pl.when(k == pl.num_programs(1) - 1)
    def _():
        o_ref[...] = x_ref[...] + gt_ref[...] * acc_ref[...]


def _mlp(x, g_row, modt_l, w1_bf, w2_bf, seq_fn, *, tm, tf):
    t = x.shape[0]
    return pl.pallas_call(
        _mlp_body,
        grid=(t // tm, D_FF // tf),
        in_specs=[
            pl.BlockSpec((tm, D_MODEL), lambda i, k: (i, 0)),
            pl.BlockSpec((1, D_MODEL), lambda i, k: (0, 0)),
            _mod_spec(3, seq_fn, 2),
            _mod_spec(4, seq_fn, 2),
            _mod_spec(5, seq_fn, 2),
            pl.BlockSpec((D_MODEL, tf), lambda i, k: (0, k)),
            pl.BlockSpec((tf, D_MODEL), lambda i, k: (k, 0)),
        ],
        out_specs=pl.BlockSpec((tm, D_MODEL), lambda i, k: (i, 0)),
        out_shape=jax.ShapeDtypeStruct((t, D_MODEL), F32),
        scratch_shapes=[pltpu.VMEM((tm, D_MODEL), BF16), pltpu.VMEM((tm, D_MODEL), F32)],
        compiler_params=_cp(("parallel", "arbitrary")),
        name="mlp",
    )(x, g_row, modt_l, modt_l, modt_l, w1_bf, w2_bf)


def _fnorm_body(x_ref, g_ref, o_ref):
    o_ref[...] = _rms(x_ref[...], g_ref[...])


def _final_norm(x, g_row, *, tm):
    t = x.shape[0]
    return pl.pallas_call(
        _fnorm_body,
        grid=(t // tm,),
        in_specs=[pl.BlockSpec((tm, D_MODEL), lambda i: (i, 0)), pl.BlockSpec((1, D_MODEL), lambda i: (0, 0))],
        out_specs=pl.BlockSpec((tm, D_MODEL), lambda i: (i, 0)),
        out_shape=jax.ShapeDtypeStruct((t, D_MODEL), F32),
        compiler_params=_cp(("parallel",)),
        name="final_norm",
    )(x, g_row)


def _s5_operators(lam_re, lam_im, log_dt, b_re, b_im, c_re, c_im):
    hp = lax.Precision.HIGHEST
    t = S5_T
    dt = jnp.exp(log_dt)[..., None]
    ar, ai = lam_re * dt, lam_im * dt
    er = jnp.exp(ar)
    lbr, lbi = er * jnp.cos(ai), er * jnp.sin(ai)
    den = lam_re * lam_re + lam_im * lam_im
    fr = ((lbr - 1.0) * lam_re + lbi * lam_im) / den
    fi = (lbi * lam_re - (lbr - 1.0) * lam_im) / den
    bbr = fr[..., None] * b_re - fi[..., None] * b_im
    bbi = fr[..., None] * b_im + fi[..., None] * b_re
    k = jnp.arange(t + 1, dtype=F32)[:, None, None, None]
    pr = jnp.exp(k * ar[None]) * jnp.cos(k * ai[None])
    pi = jnp.exp(k * ar[None]) * jnp.sin(k * ai[None])
    pbr = pr[..., None] * bbr[None] - pi[..., None] * bbi[None]
    pbi = pr[..., None] * bbi[None] + pi[..., None] * bbr[None]
    pcr = c_re[None] * pr[:, :, :, None, :] - c_im[None] * pi[:, :, :, None, :]
    pci = c_re[None] * pi[:, :, :, None, :] + c_im[None] * pr[:, :, :, None, :]
    kern = (jnp.einsum("dgpn,kdgnq->kdgpq", c_re, pbr[:t], precision=hp)
            - jnp.einsum("dgpn,kdgnq->kdgpq", c_im, pbi[:t], precision=hp))
    import numpy as np

    jb, gpb = S5_JB, S5_GPB
    kw, sw4 = t * LANES, 4 * S5_SW

    def expand(compact, col_src, row_group, col_group):
        onehot = jnp.asarray(col_src[None, :] == np.arange(compact.shape[-1])[:, None], BF16)
        full = jnp.dot(compact.astype(BF16), onehot, preferred_element_type=BF16)
        return jnp.where(jnp.asarray(row_group[:, None] == col_group[None, :])[None], full, 0.0)

    state_rows = np.arange(sw4)
    chunk_rows = np.arange(kw)
    state_grp, chunk_grp = (state_rows // S5_N) % gpb, (chunk_rows // A_GROUP) % gpb
    state_src = (state_rows // S5_SW) * S5_N + state_rows % S5_N
    chunk_src = (chunk_rows // LANES) * A_GROUP + chunk_rows % A_GROUP

    sidx = jnp.arange(t)
    secs = jnp.stack([pbr[t - 1 - sidx, 0], pbi[t - 1 - sidx, 0], pbr[sidx, 1], pbi[sidx, 1]], axis=0)
    secs = secs.reshape(4, t, jb, gpb, S5_N, A_GROUP)
    bst = expand(jnp.transpose(secs, (2, 1, 3, 5, 0, 4)).reshape(jb, kw, 4 * S5_N), state_src, chunk_grp, state_grp)
    secs = jnp.stack([pcr[sidx + 1, 0], -pci[sidx + 1, 0], pcr[t - sidx, 1], -pci[t - sidx, 1]], axis=0)
    secs = secs.reshape(4, t, jb, gpb, A_GROUP, S5_N)
    cst = expand(jnp.transpose(secs, (2, 0, 3, 5, 1, 4)).reshape(jb, sw4, t * A_GROUP), chunk_src, state_grp, chunk_grp)
    dlt = sidx[None, :] - sidx[:, None]
    kf = kern[jnp.clip(dlt, 0, t - 1), 0]
    kb = kern[jnp.clip(-dlt, 0, t - 1), 1]
    d4 = dlt[:, :, None, None, None]
    ktoe = jnp.where(d4 > 0, kf, 0.0) + jnp.where(d4 < 0, kb, 0.0) + jnp.where(d4 == 0, kf + kb, 0.0)
    ktoe = ktoe.reshape(t, t, jb, gpb, A_GROUP, A_GROUP)
    wt = expand(jnp.transpose(ktoe, (2, 0, 3, 5, 1, 4)).reshape(jb, kw, t * A_GROUP), chunk_src, chunk_grp, chunk_grp)
    acoef = jnp.stack([pr[t, 0], pi[t, 0], pr[t, 1], pi[t, 1]], axis=0)
    acoef = acoef.reshape(4, jb, S5_SW).transpose(1, 0, 2).reshape(jb, 1, 4 * S5_SW)
    acoef = jnp.broadcast_to(acoef, (jb, S5_BG, 4 * S5_SW))
    return bst.astype(BF16), cst.astype(BF16), wt.astype(BF16), acoef


def _s5_in_body(u_ref, bst_ref, lhs_ref, s_ref, *, seq_len, row_tile):
    nc = seq_len // S5_T

    @pl.when(pl.program_id(2) == 0)
    def _():
        def build(cp, carry):
            base = cp * (2 * S5_T)
            for s in range(S5_T):
                a = u_ref[pl.ds(base + s, S5_BG, stride=seq_len), :]
                b = u_ref[pl.ds(base + S5_T + s, S5_BG, stride=seq_len), :]
                lhs_ref[pl.ds(pl.multiple_of(cp * 16, 16), 16), s * LANES:(s + 1) * LANES] = (
                    jnp.concatenate([a, b], axis=0).astype(BF16))
            return carry

        lax.fori_loop(0, nc // 2, build, 0)

    def mm(r, carry):
        rows = pl.ds(pl.multiple_of(r * row_tile, row_tile), row_tile)
        s_ref[rows, :] = _dot(lhs_ref[rows, :], bst_ref[...])
        return carry

    lax.fori_loop(0, (nc * S5_BG) // row_tile, mm, 0)


def _s5_state_in(proj2d, bst, *, nbg, seq_len):
    nc = seq_len // S5_T
    rows = nc * S5_BG
    kw = S5_T * LANES
    half = 2 * S5_SW
    row_tile = min(rows, 256)
    return pl.pallas_call(
        functools.partial(_s5_in_body, seq_len=seq_len, row_tile=row_tile),
        grid=(nbg, S5_JB, 2),
        in_specs=[
            pl.BlockSpec((S5_BG * seq_len, LANES), lambda g, j, n: (g, j)),
            pl.BlockSpec((None, kw, half), lambda g, j, n: (j, 0, n)),
        ],
        out_specs=[
            pl.BlockSpec((None, None, rows, kw), lambda g, j, n: (g, j, 0, 0)),
            pl.BlockSpec((None, None, rows, half), lambda g, j, n: (g, j, 0, n)),
        ],
        out_shape=[
            jax.ShapeDtypeStruct((nbg, S5_JB, rows, kw), BF16),
            jax.ShapeDtypeStruct((nbg, S5_JB, rows, 4 * S5_SW), F32),
        ],
        compiler_params=_cp(("parallel", "parallel", "arbitrary")),
        name="s5_state_in",
    )(proj2d, bst)


def _s5_scan_body(s_ref, a_ref, x0_ref, x_ref, fin_ref, *, nc):
    w = S5_SW
    afr, afi = a_ref[:, 0:w], a_ref[:, w:2 * w]
    abr, abi = a_ref[:, 2 * w:3 * w], a_ref[:, 3 * w:4 * w]

    def step(c, carry):
        xfr, xfi, xbr, xbi = carry
        rf = pl.ds(pl.multiple_of(c * S5_BG, S5_BG), S5_BG)
        rb = pl.ds(pl.multiple_of((nc - 1 - c) * S5_BG, S5_BG), S5_BG)
        x_ref[rf, 0:w] = xfr
        x_ref[rf, w:2 * w] = xfi
        x_ref[rb, 2 * w:3 * w] = xbr
        x_ref[rb, 3 * w:4 * w] = xbi
        nfr = afr * xfr - afi * xfi + s_ref[rf, 0:w]
        nfi = afr * xfi + afi * xfr + s_ref[rf, w:2 * w]
        nbr = abr * xbr - abi * xbi + s_ref[rb, 2 * w:3 * w]
        nbi = abr * xbi + abi * xbr + s_ref[rb, 3 * w:4 * w]
        return nfr, nfi, nbr, nbi

    init = (x0_ref[:, 0:w], x0_ref[:, w:2 * w], x0_ref[:, 2 * w:3 * w], x0_ref[:, 3 * w:4 * w])
    xfr, xfi, xbr, xbi = lax.fori_loop(0, nc, step, init)
    fin_ref[:, 0:w] = xfr
    fin_ref[:, w:2 * w] = xfi
    fin_ref[:, 2 * w:3 * w] = xbr
    fin_ref[:, 3 * w:4 * w] = xbi


def _s5_scan(s_all, acoef, x0, *, nbg, seq_len):
    nc = seq_len // S5_T
    rows = nc * S5_BG
    sw4 = 4 * S5_SW
    return pl.pallas_call(
        functools.partial(_s5_scan_body, nc=nc),
        grid=(nbg, S5_JB),
        in_specs=[
            pl.BlockSpec((None, None, rows, sw4), lambda g, j: (g, j, 0, 0)),
            pl.BlockSpec((None, S5_BG, sw4), lambda g, j: (j, 0, 0)),
            pl.BlockSpec((None, None, S5_BG, sw4), lambda g, j: (g, j, 0, 0)),
        ],
        out_specs=[
            pl.BlockSpec((None, None, rows, sw4), lambda g, j: (g, j, 0, 0)),
            pl.BlockSpec((None, None, S5_BG, sw4), lambda g, j: (g, j, 0, 0)),
        ],
        out_shape=[
            jax.ShapeDtypeStruct((nbg, S5_JB, rows, sw4), F32),
            jax.ShapeDtypeStruct((nbg, S5_JB, S5_BG, sw4), F32),
        ],
        compiler_params=_cp(("parallel", "parallel")),
        name="s5_scan",
    )(s_all, acoef, x0)


def _s5_out_body(lhs_ref, x_ref, wt_ref, cst_ref, y_ref, acc_ref, *, chunks):
    acc_ref[...] = _dot(lhs_ref[...], wt_ref[...]) + _dot(x_ref[...].astype(BF16), cst_ref[...])

    def unchunk(c, carry):
        rows = pl.ds(pl.multiple_of(c * S5_BG, S5_BG), S5_BG)
        for t in range(S5_T):
            y_ref[:, c * S5_T + t, :] = acc_ref[rows, t * LANES:(t + 1) * LANES]
        return carry

    lax.fori_loop(0, chunks, unchunk, 0)


def _s5_output(lhs, x_all, wt, cst, *, nbg, seq_len):
    nc = seq_len // S5_T
    chunks = min(nc, 32)
    tr = chunks * S5_BG
    kw = S5_T * LANES
    sw4 = 4 * S5_SW
    return pl.pallas_call(
        functools.partial(_s5_out_body, chunks=chunks),
        grid=(nbg, S5_JB, nc // chunks),
        in_specs=[
            pl.BlockSpec((None, None, tr, kw), lambda g, j, r: (g, j, r, 0)),
            pl.BlockSpec((None, None, tr, sw4), lambda g, j, r: (g, j, r, 0)),
            pl.BlockSpec((None, kw, kw), lambda g, j, r: (j, 0, 0)),
            pl.BlockSpec((None, sw4, kw), lambda g, j, r: (j, 0, 0)),
        ],
        out_specs=pl.BlockSpec((S5_BG, chunks * S5_T, LANES), lambda g, j, r: (g, r, j)),
        out_shape=jax.ShapeDtypeStruct((nbg * S5_BG, seq_len, MIX_A), F32),
        scratch_shapes=[pltpu.VMEM((tr, kw), F32)],
        compiler_params=_cp(("parallel", "parallel", "arbitrary")),
        name="s5_output",
    )(lhs, x_all, wt, cst)


def _s5_mix(proj2d, ops, x0, *, nbg, seq_len):
    bst, cst, wt, acoef = ops
    lhs, s_all = _s5_state_in(proj2d, bst, nbg=nbg, seq_len=seq_len)
    x_all, fin = _s5_scan(s_all, acoef, x0, nbg=nbg, seq_len=seq_len)
    y = _s5_output(lhs, x_all, wt, cst, nbg=nbg, seq_len=seq_len)
    return y, fin


def _s5_pack_state(re, im):
    b = re.shape[0]
    x = jnp.stack([re[:, 0], im[:, 0], re[:, 1], im[:, 1]], axis=1)
    x = x.reshape(b // S5_BG, S5_BG, 4, S5_JB, S5_SW)
    return jnp.transpose(x, (0, 3, 1, 2, 4)).reshape(b // S5_BG, S5_JB, S5_BG, 4 * S5_SW)


def _s5_unpack_state(fin):
    nbg = fin.shape[0]
    x = fin.reshape(nbg, S5_JB, S5_BG, 4, S5_SW)
    x = jnp.transpose(x, (0, 2, 3, 1, 4)).reshape(nbg * S5_BG, 4, A_GROUPS, S5_N)
    return jnp.stack([x[:, 0], x[:, 2]], axis=1), jnp.stack([x[:, 1], x[:, 3]], axis=1)


Q_BLK, K_BLK, V_BLK = MIX_A // LANES, (MIX_A + MIX_B) // LANES, (MIX_A + 2 * MIX_B) // LANES


def _pair_softmax_pv(q, parts):
    lane = lax.broadcasted_iota(jnp.int32, (1, LANES), 1)
    outs = []
    for half in range(2):
        sel = (lane < NA_HD) if half == 0 else (lane >= NA_HD)
        qm = jnp.where(sel, q, 0.0).astype(BF16)
        scores = []
        for k, _, bias in parts:
            s = _dot_nt(qm, k)
            if bias is not None:
                s = s + bias[half]
            scores.append(s)
        m = scores[0].max(axis=-1, keepdims=True)
        for s in scores[1:]:
            m = jnp.maximum(m, s.max(axis=-1, keepdims=True))
        l = None
        o = None
        for s, (_, v, _) in zip(scores, parts):
            p = jnp.exp(s - m)
            ls = p.sum(axis=-1, keepdims=True)
            os_ = _dot(p.astype(BF16), v)
            l = ls if l is None else l + ls
            o = os_ if o is None else o + os_
        outs.append(o / l)
    return jnp.where(lane < NA_HD, outs[0], outs[1])


def _ctx_attn_body(q_ref, k_ref, v_ref, o_ref):
    q = q_ref[...] * (NA_HD ** -0.5)
    o_ref[...] = _pair_softmax_pv(q, [(k_ref[...].astype(BF16), v_ref[...].astype(BF16), None)])


def _context_attention(proj, *, nb, seq_len):
    npair = NA_HEADS // 2
    return pl.pallas_call(
        _ctx_attn_body,
        grid=(nb, npair),
        in_specs=[
            pl.BlockSpec((seq_len, LANES), lambda b, h: (b, Q_BLK + h)),
            pl.BlockSpec((seq_len, LANES), lambda b, h: (b, K_BLK + h)),
            pl.BlockSpec((seq_len, LANES), lambda b, h: (b, V_BLK + h)),
        ],
        out_specs=pl.BlockSpec((seq_len, LANES), lambda b, h: (b, h)),
        out_shape=jax.ShapeDtypeStruct((nb * seq_len, MIX_B), F32),
        compiler_params=_cp(("parallel", "parallel")),
        name="context_attention",
    )(proj, proj, proj)


NA_QROWS = 4
NA_WROWS = NA_QROWS + NA_KH


def _na_window_start(g, rows):
    rs = jnp.clip(g * NA_QROWS - NA_KH // 2, 0, rows - NA_KH)
    return jnp.minimum(rs, rows - NA_WROWS)


def _na_bias_tables(rpb, rows):
    import numpy as np

    ngroups = rows // NA_QROWS
    shp = (NA_QROWS * GRID_W, NA_WROWS * GRID_W)
    ndc = 2 * NA_KW - 1
    c, kc = np.meshgrid(np.arange(GRID_W), np.arange(GRID_W), indexing="ij")
    cs = np.clip(c - NA_KW // 2, 0, GRID_W - NA_KW)
    col_ok = (kc >= cs) & (kc < cs + NA_KW)
    col_sel = ((kc - c + NA_KW - 1)[None] == np.arange(ndc)[:, None, None]) & col_ok[None]
    dr_idx, valid = [], []
    for g in (0, 1, ngroups - 1):
        rs0 = int(np.clip(g * NA_QROWS - NA_KH // 2, 0, rows - NA_KH))
        ws = min(rs0, rows - NA_WROWS)
        ri, wi = np.meshgrid(np.arange(NA_QROWS), np.arange(NA_WROWS), indexing="ij")
        r = g * NA_QROWS + ri
        rs = np.clip(r - NA_KH // 2, 0, rows - NA_KH)
        row = ws + wi
        row_ok = (row >= rs) & (row < rs + NA_KH)
        dr_idx.append(np.clip(row - r + NA_KH - 1, 0, 2 * NA_KH - 2))
        valid.append((row_ok[:, None, :, None] & col_ok[None, :, None, :]).reshape(shp))
    rows_sel = rpb[:, np.stack(dr_idx)]
    t = jnp.einsum("hgrwd,dck->hgrcwk", rows_sel, jnp.asarray(col_sel, F32), precision=lax.Precision.HIGHEST)
    t = jnp.where(jnp.asarray(np.stack(valid))[None], t.reshape(NA_HEADS, 3, *shp), NEG_BIG)
    t = t.reshape(NA_HEADS // 2, 2, 3, *shp)
    return jnp.transpose(t, (0, 2, 1, 3, 4))


def _nbr_attn_body(q_ref, k_ref, v_ref, ck_ref, cv_ref, bias_ref, o_ref, *, rows):
    g = pl.program_id(2)
    ngroups = rows // NA_QROWS
    case = jnp.where(g == 0, 0, jnp.where(g == ngroups - 1, 2, 1))
    ws = _na_window_start(g, rows)
    win = pl.ds(pl.multiple_of(ws * GRID_W, GRID_W), NA_WROWS * GRID_W)
    q = q_ref[...] * (NA_HD ** -0.5)
    kw = k_ref[win, :].astype(BF16)
    vw = v_ref[win, :].astype(BF16)
    bias = (bias_ref[case, 0], bias_ref[case, 1])
    o_ref[...] = _pair_softmax_pv(q, [(kw, vw, bias),
                                      (ck_ref[...].astype(BF16), cv_ref[...].astype(BF16), None)])


def _neighbourhood_attention(proj, ck, cv, bias, *, nb, seq_len):
    rows = seq_len // GRID_W
    ngroups = rows // NA_QROWS
    npair = NA_HEADS // 2
    qt = NA_QROWS * GRID_W
    past = ck.shape[1]
    return pl.pallas_call(
        functools.partial(_nbr_attn_body, rows=rows),
        grid=(npair, nb, ngroups),
        in_specs=[
            pl.BlockSpec((qt, LANES), lambda h, b, g: (b * ngroups + g, Q_BLK + h)),
            pl.BlockSpec((seq_len, LANES), lambda h, b, g: (b, K_BLK + h)),
            pl.BlockSpec((seq_len, LANES), lambda h, b, g: (b, V_BLK + h)),
            pl.BlockSpec((None, past, LANES), lambda h, b, g: (b, 0, h)),
            pl.BlockSpec((None, past, LANES), lambda h, b, g: (b, 0, h)),
            pl.BlockSpec((None, 3, 2, qt, NA_WROWS * GRID_W), lambda h, b, g: (h, 0, 0, 0, 0)),
        ],
        out_specs=pl.BlockSpec((qt, LANES), lambda h, b, g: (b * ngroups + g, h)),
        out_shape=jax.ShapeDtypeStruct((nb * seq_len, MIX_B), F32),
        compiler_params=_cp(("parallel", "parallel", "parallel")),
        name="neighbourhood_attention",
    )(proj, proj, proj, ck, cv, bias)


def _even_out_body(x_ref, y_ref, u_ref, bo_ref, d_ref, wg_ref, bg_ref, wo_ref, gt_ref, o_ref):
    z = _gelu_tanh(y_ref[...] + d_ref[...] * u_ref[...])
    a = z * _sigmoid(_dot(z.astype(BF16), wg_ref[...]) + bg_ref[...])
    cat = jnp.concatenate([a, bo_ref[...]], axis=-1).astype(BF16)
    o_ref[...] = x_ref[...] + gt_ref[...] * _dot(cat, wo_ref[...])


def _even_out(x, y, proj, bo, d_row, wg_bf, bg_row, wo_bf, modt_l, seq_fn, *, tm):
    t = x.shape[0]
    return pl.pallas_call(
        _even_out_body,
        grid=(t // tm,),
        in_specs=[
            pl.BlockSpec((tm, D_MODEL), lambda i: (i, 0)),
            pl.BlockSpec((tm, MIX_A), lambda i: (i, 0)),
            pl.BlockSpec((tm, MIX_A), lambda i: (i, 0)),
            pl.BlockSpec((tm, MIX_B), lambda i: (i, 0)),
            pl.BlockSpec((1, MIX_A), lambda i: (0, 0)),
            pl.BlockSpec((MIX_A, MIX_A), lambda i: (0, 0)),
            pl.BlockSpec((1, MIX_A), lambda i: (0, 0)),
            pl.BlockSpec((D_MODEL, D_MODEL), lambda i: (0, 0)),
            _mod_spec(2, seq_fn, 1),
        ],
        out_specs=pl.BlockSpec((tm, D_MODEL), lambda i: (i, 0)),
        out_shape=jax.ShapeDtypeStruct((t, D_MODEL), F32),
        compiler_params=_cp(("parallel",)),
        name="even_out",
    )(x, y, proj, bo, d_row, wg_bf, bg_row, wo_bf, modt_l)


DN_QKV = 3 * D_MODEL
DN_PROJ_PAD = 4 * D_MODEL + LANES
DN_GATE_BLK = (4 * D_MODEL) // LANES
DN_CB = 512
DN_HALO = SUBLANES


def _gdn_pre_body(x_ref, w_ref, o_ref, pad_ref, *, seq_len, rt):
    kind = pl.program_id(1) // (D_MODEL // DN_CB)
    zeros = jnp.zeros((DN_HALO, DN_CB), F32)
    pad_ref[0:DN_HALO, :] = zeros
    pad_ref[DN_HALO + seq_len:DN_HALO + seq_len + DN_HALO, :] = zeros
    pad_ref[DN_HALO:DN_HALO + seq_len, :] = x_ref[...]
    qscale = jnp.where(kind == 0, DN_DK ** -0.5, 1.0)
    lo = DN_HALO - (DN_CONV - 1) // 2

    def tile(r, carry):
        base = pl.multiple_of(r * rt, rt)
        xt = pad_ref[pl.ds(base, rt + 2 * DN_HALO), :]
        y = None
        for k in range(DN_CONV):
            term = w_ref[k:k + 1, :] * xt[lo + k:lo + k + rt, :]
            y = term if y is None else y + term
        y = _silu(y)
        segs = []
        for h in range(DN_CB // DN_DK):
            seg = y[:, h * DN_DK:(h + 1) * DN_DK]
            nrm = seg * lax.rsqrt(jnp.sum(seg * seg, axis=-1, keepdims=True) + EPS) * qscale
            segs.append(jnp.where(kind == 2, seg, nrm))
        o_ref[pl.ds(base, rt), :] = jnp.concatenate(segs, axis=-1)
        return carry

    lax.fori_loop(0, seq_len // rt, tile, 0)


def _gdn_pre(proj, conv_w, *, nb, seq_len):
    rt = 256
    return pl.pallas_call(
        functools.partial(_gdn_pre_body, seq_len=seq_len, rt=rt),
        grid=(nb, DN_QKV // DN_CB),
        in_specs=[
            pl.BlockSpec((seq_len, DN_CB), lambda b, c: (b, c)),
            pl.BlockSpec((DN_CONV, DN_CB), lambda b, c: (0, c)),
        ],
        out_specs=pl.BlockSpec((seq_len, DN_CB), lambda b, c: (b, c)),
        out_shape=jax.ShapeDtypeStruct((nb * seq_len, DN_QKV), F32),
        scratch_shapes=[pltpu.VMEM((seq_len + 2 * DN_HALO, DN_CB), F32)],
        compiler_params=_cp(("parallel", "parallel")),
        name="gdn_pre",
    )(proj, conv_w)


def _split_bf16(x, n):
    parts = []
    r = x
    for _ in range(n):
        p = r.astype(BF16)
        parts.append(p)
        r = r - p.astype(F32)
    return parts


def _mm_exact_lhs(a_bf, b):
    return sum(_dot(a_bf, p) for p in _split_bf16(b, 3))


def _softplus(x):
    return jnp.maximum(x, 0.0) + jnp.log(1.0 + jnp.exp(-jnp.abs(x)))


def _gdn_body(q_ref, k_ref, v_ref, gt_ref, ga_ref, gb_ref, s0_ref, o_ref, sl_ref,
              s_scr, vn_scr, a_scr, t_scr, m_scr, attn_scr, rhs_scr, u_scr, w_scr, qg_scr, kg_scr, *, direction):
    t = pl.program_id(1)
    nt = pl.num_programs(1)
    tile = GDN_TILE
    c_len = DN_CHUNK
    nchunk = tile // c_len
    heads = range(DN_HEADS)

    @pl.when(t == 0)
    def _():
        s_scr[...] = s0_ref[...]
        vn_scr[...] = jnp.zeros_like(vn_scr)

    ri = lax.broadcasted_iota(jnp.int32, (tile, tile), 0)
    ci = lax.broadcasted_iota(jnp.int32, (tile, tile), 1)
    same = (ri // c_len) == (ci // c_len)
    if direction == 0:
        incl = same & (ci <= ri)
        strict = same & (ci < ri)
    else:
        incl = same & (ci >= ri)
        strict = same & (ci > ri)
    tri_bf = jnp.where(incl, 1.0, 0.0).astype(BF16)
    blk_bf = jnp.where(same, 1.0, 0.0).astype(BF16)
    late, early = (ri, ci) if direction == 0 else (ci, ri)
    off_masks = [(((late >> lvl) & 1) == 1) & (((early >> lvl) & 1) == 0) & ((ri >> (lvl + 1)) == (ci >> (lvl + 1)))
                 for lvl in range(int(math.log2(c_len)))]

    gt = gt_ref[...]
    g_all = ga_ref[...] * _softplus(gt + gb_ref[...])
    beta_all = _sigmoid(gt)
    gc_all = _mm_exact_lhs(tri_bf, g_all)
    gl_all = _mm_exact_lhs(blk_bf, g_all)
    gc_t = jnp.transpose(gc_all)
    chunk_order = range(nchunk) if direction == 0 else range(nchunk - 1, -1, -1)
    eye = jnp.where(ri == ci, 1.0, 0.0)

    def lane(h):
        return direction * DN_HEADS + h

    def hsl(h):
        return slice(h * DN_DK, (h + 1) * DN_DK)

    for h in heads:
        gcol = gc_all[:, lane(h):lane(h) + 1]
        grow = gc_t[lane(h):lane(h) + 1, :]
        glcol = gl_all[:, lane(h):lane(h) + 1]
        bcol = beta_all[:, 2 * DN_HEADS + lane(h):2 * DN_HEADS + lane(h) + 1]
        qh, kh, vh = q_ref[:, hsl(h)], k_ref[:, hsl(h)], v_ref[:, hsl(h)]
        k_bf = kh.astype(BF16)
        decay = jnp.exp(jnp.where(incl, gcol - grow, NEG_BIG))
        a_mat = jnp.where(strict, bcol * _dot_nt(k_bf, k_bf) * decay, 0.0)
        a_scr[h] = a_mat
        t_scr[h] = (eye - jnp.where(off_masks[0], a_mat, 0.0)).astype(BF16)
        attn_scr[h] = (_dot_nt(qh.astype(BF16), k_bf) * decay).astype(BF16)
        egc = jnp.exp(gcol)
        rhs_scr[h] = jnp.concatenate([vh * bcol, kh * (bcol * egc)], axis=-1).astype(BF16)
        qg_scr[h] = (qh * egc).astype(BF16)
        kg_scr[h] = (kh * jnp.exp(glcol - gcol)).astype(BF16)

    for lvl in range(1, len(off_masks)):
        for h in heads:
            m_scr[h] = _dot(jnp.where(off_masks[lvl], a_scr[h], 0.0).astype(BF16), t_scr[h]).astype(BF16)
        for h in heads:
            t_scr[h] = t_scr[h] - _dot(t_scr[h], m_scr[h]).astype(BF16)

    for h in heads:
        x = _dot(t_scr[h], rhs_scr[h])
        u_scr[h] = x[:, :DN_DV]
        w_scr[h] = x[:, DN_DV:].astype(BF16)

    for c in chunk_order:
        rows = slice(c * c_len, (c + 1) * c_len)
        for h in heads:
            s = s_scr[h]
            s_bf = s.astype(BF16)
            v_new = u_scr[h, rows, :] - _dot(w_scr[h, rows, :], s_bf)
            vn_scr[h, rows, :] = v_new.astype(BF16)
            o_ref[rows, hsl(h)] = _dot(qg_scr[h, rows, :], s_bf) + _dot(attn_scr[h, rows, :], vn_scr[h])
            gl_c = gl_all[c * c_len:c * c_len + 1, lane(h):lane(h) + 1]
            s_scr[h] = s * jnp.exp(gl_c) + _dot_tn(kg_scr[h, rows, :], v_new.astype(BF16))

    @pl.when(t == nt - 1)
    def _():
        sl_ref[...] = s_scr[...]


def _gdn_scan(qkv, proj, ga_row, gb_row, s0, *, nb, seq_len, direction):
    nt = seq_len // GDN_TILE
    tile = GDN_TILE
    nq = D_MODEL // D_MODEL

    def row(b, t):
        return b * nt + (t if direction == 0 else nt - 1 - t)

    return pl.pallas_call(
        functools.partial(_gdn_body, direction=direction),
        grid=(nb, nt),
        in_specs=[
            pl.BlockSpec((tile, D_MODEL), lambda b, t: (row(b, t), 0)),
            pl.BlockSpec((tile, D_MODEL), lambda b, t: (row(b, t), nq)),
            pl.BlockSpec((tile, D_MODEL), lambda b, t: (row(b, t), 2 * nq)),
            pl.BlockSpec((tile, LANES), lambda b, t: (row(b, t), DN_GATE_BLK)),
            pl.BlockSpec((1, LANES), lambda b, t: (0, 0)),
            pl.BlockSpec((1, LANES), lambda b, t: (0, 0)),
            pl.BlockSpec((None, DN_HEADS, DN_DK, DN_DV), lambda b, t: (b, 0, 0, 0)),
        ],
        out_specs=[
            pl.BlockSpec((tile, D_MODEL), lambda b, t: (row(b, t), 0)),
            pl.BlockSpec((None, DN_HEADS, DN_DK, DN_DV), lambda b, t: (b, 0, 0, 0)),
        ],
        out_shape=[
            jax.ShapeDtypeStruct((nb * seq_len, D_MODEL), F32),
            jax.ShapeDtypeStruct((nb, DN_HEADS, DN_DK, DN_DV), F32),
        ],
        scratch_shapes=[
            pltpu.VMEM((DN_HEADS, DN_DK, DN_DV), F32),
            pltpu.VMEM((DN_HEADS, tile, DN_DV), BF16),
            pltpu.VMEM((DN_HEADS, tile, tile), F32),
            pltpu.VMEM((DN_HEADS, tile, tile), BF16),
            pltpu.VMEM((DN_HEADS, tile, tile), BF16),
            pltpu.VMEM((DN_HEADS, tile, tile), BF16),
            pltpu.VMEM((DN_HEADS, tile, 2 * DN_DV), BF16),
            pltpu.VMEM((DN_HEADS, tile, DN_DV), F32),
            pltpu.VMEM((DN_HEADS, tile, DN_DK), BF16),
            pltpu.VMEM((DN_HEADS, tile, DN_DK), BF16),
            pltpu.VMEM((DN_HEADS, tile, DN_DK), BF16),
        ],
        compiler_params=_cp(("parallel", "arbitrary")),
        name="gdn_scan_fwd" if direction == 0 else "gdn_scan_bwd",
    )(qkv, qkv, qkv, proj, ga_row, gb_row, s0)


def _odd_out_body(x_ref, of_ref, ob_ref, z_ref, ng_ref, wo_ref, gt_ref, o_ref):
    o = of_ref[...] + ob_ref[...]
    segs = []
    for h in range(DN_HEADS):
        seg = o[:, h * DN_DV:(h + 1) * DN_DV]
        segs.append(seg * lax.rsqrt(jnp.mean(seg * seg, axis=-1, keepdims=True) + EPS))
    y = jnp.concatenate(segs, axis=-1) * ng_ref[...] * _silu(z_ref[...])
    o_ref[...] = x_ref[...] + gt_ref[...] * _dot(y.astype(BF16), wo_ref[...])


def _odd_out(x, o_f, o_b, proj, ng_row, wo_bf, modt_l, seq_fn, *, tm):
    t = x.shape[0]
    zblk = DN_QKV // D_MODEL
    return pl.pallas_call(
        _odd_out_body,
        grid=(t // tm,),
        in_specs=[
            pl.BlockSpec((tm, D_MODEL), lambda i: (i, 0)),
            pl.BlockSpec((tm, D_MODEL), lambda i: (i, 0)),
            pl.BlockSpec((tm, D_MODEL), lambda i: (i, 0)),
            pl.BlockSpec((tm, D_MODEL), lambda i: (i, zblk)),
            pl.BlockSpec((1, D_MODEL), lambda i: (0, 0)),
            pl.BlockSpec((D_MODEL, D_MODEL), lambda i: (0, 0)),
            _mod_spec(2, seq_fn, 1),
        ],
        out_specs=pl.BlockSpec((tm, D_MODEL), lambda i: (i, 0)),
        out_shape=jax.ShapeDtypeStruct((t, D_MODEL), F32),
        compiler_params=_cp(("parallel",)),
        name="odd_out",
    )(x, o_f, o_b, proj, ng_row, wo_bf, modt_l)


ROW_TILE = 512


def _seq_fn(slot0, seq_len, tm):
    if slot0 == 0:
        return lambda i: 0
    return lambda i: slot0 + (i * tm) // seq_len


def kernel(x_prompt, x_sample, c, cache_na_k, cache_na_v, state_s5_re, state_s5_im, state_dn, c_ctx, norm_mix_g, norm_ff_g, w_mod, b_mod, w_ff1, w_ff2, w_in_e, w_out_e, s5_lam_re, s5_lam_im, s5_log_dt, s5_b_re, s5_b_im, s5_c_re, s5_c_im, s5_d, s5_w_glu, s5_b_glu, na_rpb, w_in_o, dn_conv_w, dn_a_log, dn_dt_bias, dn_norm_g, w_out_o, final_norm_g):
    nbp, lp = x_prompt.shape[:2]
    nbs, ls = x_sample.shape[:2]
    assert nbp % S5_BG == 0 and nbs % S5_BG == 0 and 1 + nbs <= 16
    streams = [
        dict(x=x_prompt.reshape(nbp * lp, D_MODEL), nb=nbp, L=lp, slot0=0),
        dict(x=x_sample.reshape(nbs * ls, D_MODEL), nb=nbs, L=ls, slot0=1),
    ]
    cond16 = jnp.zeros((16, D_MODEL), F32).at[0].set(c_ctx).at[1:1 + nbs].set(c)
    mod = _modulation(cond16, w_mod.astype(BF16), b_mod[:, None, :])
    modt = jnp.transpose(mod.reshape(DEPTH, 16, 6, D_MODEL), (0, 2, 1, 3))[:, :, :, None, :]

    new_k, new_v, new_sre, new_sim, new_dn = [], [], [], [], []
    for l in range(DEPTH):
        e = l // 2
        if l % 2 == 0:
            w_in = w_in_e[e].astype(BF16)
            w_out = w_out_e[e].astype(BF16)
            w_glu = s5_w_glu[e].astype(BF16)
            ops = _s5_operators(s5_lam_re[e], s5_lam_im[e], s5_log_dt[e], s5_b_re[e], s5_b_im[e],
                                s5_c_re[e], s5_c_im[e])
            bias = _na_bias_tables(na_rpb[e], ls // GRID_W)
        else:
            w_in = jnp.pad(w_in_o[e], ((0, 0), (0, DN_PROJ_PAD - w_in_o.shape[-1]))).astype(BF16)
            w_out = w_out_o[e].astype(BF16)
            ga_row = jnp.zeros((1, LANES), F32).at[0, :2 * DN_HEADS].set(-jnp.exp(dn_a_log[e]).reshape(-1))
            gb_row = jnp.zeros((1, LANES), F32).at[0, :2 * DN_HEADS].set(dn_dt_bias[e].reshape(-1))
            ng_row = jnp.tile(dn_norm_g[e], DN_HEADS)[None]
        w1 = w_ff1[l].astype(BF16)
        w2 = w_ff2[l].astype(BF16)
        for st in streams:
            x, nb, sl = st["x"], st["nb"], st["L"]
            seq = _seq_fn(st["slot0"], sl, ROW_TILE)
            is_ctx = st["slot0"] == 0
            if l % 2 == 0:
                proj = _norm_mod_matmul(x, norm_mix_g[l][None], modt[l], w_in, seq, tm=ROW_TILE, tn=1024,
                                        name="in_proj_even")
                nbg = nb // S5_BG
                if is_ctx:
                    x0 = jnp.zeros((nbg, S5_JB, S5_BG, 4 * S5_SW), F32)
                else:
                    x0 = _s5_pack_state(state_s5_re[:, e], state_s5_im[:, e])
                y, fin = _s5_mix(proj, ops, x0, nbg=nbg, seq_len=sl)
                if is_ctx:
                    bo = _context_attention(proj, nb=nb, seq_len=sl)
                    kv = proj.reshape(nb, sl, -1)
                    new_k.append(kv[:, :, MIX_A + MIX_B:MIX_A + 2 * MIX_B].reshape(nb, sl, NA_HEADS, NA_HD))
                    new_v.append(kv[:, :, MIX_A + 2 * MIX_B:].reshape(nb, sl, NA_HEADS, NA_HD))
                    sre, sim = _s5_unpack_state(fin)
                    new_sre.append(sre)
                    new_sim.append(sim)
                else:
                    past = cache_na_k.shape[2]
                    ck = cache_na_k[:, e].reshape(nb, past, MIX_B)
                    cv = cache_na_v[:, e].reshape(nb, past, MIX_B)
                    bo = _neighbourhood_attention(proj, ck, cv, bias, nb=nb, seq_len=sl)
                x = _even_out(x, y.reshape(nb * sl, MIX_A), proj, bo, s5_d[e][None], w_glu, s5_b_glu[e][None],
                              w_out, modt[l], seq, tm=ROW_TILE)
            else:
                proj = _norm_mod_matmul(x, norm_mix_g[l][None], modt[l], w_in, seq, tm=ROW_TILE,
                                        tn=DN_PROJ_PAD // 3, name="in_proj_odd")
                qkv = _gdn_pre(proj, dn_conv_w[e], nb=nb, seq_len=sl)
                if is_ctx:
                    s0 = jnp.zeros((nb, 2, DN_HEADS, DN_DK, DN_DV), F32)
                else:
                    s0 = state_dn[:, e]
                o_f, s_f = _gdn_scan(qkv, proj, ga_row, gb_row, s0[:, 0], nb=nb, seq_len=sl, direction=0)
                o_b, s_b = _gdn_scan(qkv, proj, ga_row, gb_row, s0[:, 1], nb=nb, seq_len=sl, direction=1)
                if is_ctx:
                    new_dn.append(jnp.stack([s_f, s_b], axis=1))
                x = _odd_out(x, o_f, o_b, proj, ng_row, w_out, modt[l], seq, tm=ROW_TILE)
            st["x"] = _mlp(x, norm_ff_g[l][None], modt[l], w1, w2, seq, tm=ROW_TILE, tf=1024)

    y_prompt = _final_norm(streams[0]["x"], final_norm_g[None], tm=ROW_TILE).reshape(nbp, lp, D_MODEL)
    y_sample = _final_norm(streams[1]["x"], final_norm_g[None], tm=ROW_TILE).reshape(nbs, ls, D_MODEL)
    return (y_prompt, y_sample, jnp.stack(new_k, axis=1), jnp.stack(new_v, axis=1),
            jnp.stack(new_sre, axis=1), jnp.stack(new_sim, axis=1), jnp.stack(new_dn, axis=1))
```

```python
import functools
import math

import jax
import jax.numpy as jnp
from jax import lax
from jax.experimental import pallas as pl
from jax.experimental.pallas import tpu as pltpu

F32 = jnp.float32
BF16 = jnp.bfloat16

D_MODEL = 1024
DEPTH = 2
GRID_W = 64
MIX_A = D_MODEL // 2
A_GROUP = 16
A_GROUPS = MIX_A // A_GROUP
S5_N = 64
MIX_B = D_MODEL - MIX_A
NA_HEADS = 8
NA_HD = MIX_B // NA_HEADS
NA_KH = 8
NA_KW = 16
DN_HEADS = 8
DN_DK = D_MODEL // DN_HEADS
DN_DV = D_MODEL // DN_HEADS
DN_CONV = 5
DN_CHUNK = 64
D_FF = 4 * D_MODEL
EPS = 1e-6

LANES = 128
SUBLANES = 8
VMEM_LIMIT = 56 * 1024 * 1024
NEG_BIG = -1e30

S5_T = 16
S5_BG = SUBLANES
S5_JB = MIX_A // LANES
S5_GPB = LANES // A_GROUP
S5_SW = S5_GPB * S5_N
GDN_TILE = 256


def _cp(sem, vmem=VMEM_LIMIT):
    return pltpu.CompilerParams(dimension_semantics=sem, vmem_limit_bytes=vmem)


def _rms(x, g):
    return x * lax.rsqrt(jnp.mean(x * x, axis=-1, keepdims=True) + EPS) * g


def _sigmoid(x):
    return 1.0 / (1.0 + jnp.exp(-x))


def _silu(x):
    return x * _sigmoid(x)


def _gelu_tanh(x):
    return 0.5 * x * (1.0 + jnp.tanh(math.sqrt(2.0 / math.pi) * (x + 0.044715 * (x * x * x))))


def _dot(a, b):
    return jnp.dot(a, b, preferred_element_type=F32)


def _dot_nt(a, b):
    return lax.dot_general(a, b, (((1,), (1,)), ((), ())), preferred_element_type=F32)


def _dot_tn(a, b):
    return lax.dot_general(a, b, (((0,), (0,)), ((), ())), preferred_element_type=F32)


def _mod_body(c_ref, w_ref, b_ref, o_ref):
    c = c_ref[...]
    o_ref[...] = _dot(_silu(c).astype(BF16), w_ref[...]) + b_ref[...]


def _modulation(cond16, w_mod_bf, b_mod):
    tn = 1536
    n = 6 * D_MODEL
    return pl.pallas_call(
        _mod_body,
        grid=(DEPTH, n // tn),
        in_specs=[
            pl.BlockSpec((16, D_MODEL), lambda l, j: (0, 0)),
            pl.BlockSpec((None, D_MODEL, tn), lambda l, j: (l, 0, j)),
            pl.BlockSpec((None, 1, tn), lambda l, j: (l, 0, j)),
        ],
        out_specs=pl.BlockSpec((None, 16, tn), lambda l, j: (l, 0, j)),
        out_shape=jax.ShapeDtypeStruct((DEPTH, 16, n), F32),
        compiler_params=_cp(("parallel", "parallel")),
        name="modulation",
    )(cond16, w_mod_bf, b_mod)


def _mod_spec(which, seq_fn, ngrid):
    if ngrid == 1:
        return pl.BlockSpec((None, None, 1, D_MODEL), lambda i: (which, seq_fn(i), 0, 0))
    return pl.BlockSpec((None, None, 1, D_MODEL), lambda i, j: (which, seq_fn(i), 0, 0))


def _nmm_body(x_ref, g_ref, sh_ref, sc_ref, w_ref, o_ref, h_ref):
    @pl.when(pl.program_id(1) == 0)
    def _():
        h = _rms(x_ref[...], g_ref[...]) * (1.0 + sc_ref[...]) + sh_ref[...]
        h_ref[...] = h.astype(BF16)

    o_ref[...] = _dot(h_ref[...], w_ref[...]).astype(o_ref.dtype)


def _norm_mod_matmul(x, g_row, modt_l, w_bf, seq_fn, *, tm, tn, name, out_dtype=F32):
    t, n = x.shape[0], w_bf.shape[1]
    return pl.pallas_call(
        _nmm_body,
        grid=(t // tm, n // tn),
        in_specs=[
            pl.BlockSpec((tm, D_MODEL), lambda i, j: (i, 0)),
            pl.BlockSpec((1, D_MODEL), lambda i, j: (0, 0)),
            _mod_spec(0, seq_fn, 2),
            _mod_spec(1, seq_fn, 2),
            pl.BlockSpec((D_MODEL, tn), lambda i, j: (0, j)),
        ],
        out_specs=pl.BlockSpec((tm, tn), lambda i, j: (i, j)),
        out_shape=jax.ShapeDtypeStruct((t, n), out_dtype),
        scratch_shapes=[pltpu.VMEM((tm, D_MODEL), BF16)],
        compiler_params=_cp(("parallel", "arbitrary")),
        name=name,
    )(x, g_row, modt_l, modt_l, w_bf)


def _mlp_body(x_ref, g_ref, sh_ref, sc_ref, gt_ref, w1_ref, w2_ref, fg_ref, o_ref, h_ref, acc_ref, *, final_norm):
    k = pl.program_id(1)

    @pl.when(k == 0)
    def _():
        h = _rms(x_ref[...], g_ref[...]) * (1.0 + sc_ref[...]) + sh_ref[...]
        h_ref[...] = h.astype(BF16)
        acc_ref[...] = jnp.zeros_like(acc_ref)

    a = jnp.maximum(_dot(h_ref[...], w1_ref[...]), 0.0)
    acc_ref[...] += _dot((a * a).astype(BF16), w2_ref[...])

    @pl.when(k == pl.num_programs(1) - 1)
    def _():
        y = x_ref[...] + gt_ref[...] * acc_ref[...]
        o_ref[...] = _rms(y, fg_ref[...]) if final_norm else y


def _mlp(x, g_row, modt_l, w1_bf, w2_bf, seq_fn, fg_row, *, tm, tf, final_norm):
    t = x.shape[0]
    return pl.pallas_call(
        functools.partial(_mlp_body, final_norm=final_norm),
        grid=(t // tm, D_FF // tf),
        in_specs=[
            pl.BlockSpec((tm, D_MODEL), lambda i, k: (i, 0)),
            pl.BlockSpec((1, D_MODEL), lambda i, k: (0, 0)),
            _mod_spec(3, seq_fn, 2),
            _mod_spec(4, seq_fn, 2),
            _mod_spec(5, seq_fn, 2),
            pl.BlockSpec((D_MODEL, tf), lambda i, k: (0, k)),
            pl.BlockSpec((tf, D_MODEL), lambda i, k: (k, 0)),
            pl.BlockSpec((1, D_MODEL), lambda i, k: (0, 0)),
        ],
        out_specs=pl.BlockSpec((tm, D_MODEL), lambda i, k: (i, 0)),
        out_shape=jax.ShapeDtypeStruct((t, D_MODEL), F32),
        scratch_shapes=[pltpu.VMEM((tm, D_MODEL), BF16), pltpu.VMEM((tm, D_MODEL), F32)],
        compiler_params=_cp(("parallel", "arbitrary")),
        name="mlp",
    )(x, g_row, modt_l, modt_l, modt_l, w1_bf, w2_bf, fg_row)


def _s5_operators(lam_re, lam_im, log_dt, b_re, b_im, c_re, c_im):
    hp = lax.Precision.HIGHEST
    t = S5_T
    dt = jnp.exp(log_dt)[..., None]
    ar, ai = lam_re * dt, lam_im * dt
    er = jnp.exp(ar)
    lbr, lbi = er * jnp.cos(ai), er * jnp.sin(ai)
    den = lam_re * lam_re + lam_im * lam_im
    fr = ((lbr - 1.0) * lam_re + lbi * lam_im) / den
    fi = (lbi * lam_re - (lbr - 1.0) * lam_im) / den
    bbr = fr[..., None] * b_re - fi[..., None] * b_im
    bbi = fr[..., None] * b_im + fi[..., None] * b_re
    k = jnp.arange(t + 1, dtype=F32)[:, None, None, None]
    pr = jnp.exp(k * ar[None]) * jnp.cos(k * ai[None])
    pi = jnp.exp(k * ar[None]) * jnp.sin(k * ai[None])
    pbr = pr[..., None] * bbr[None] - pi[..., None] * bbi[None]
    pbi = pr[..., None] * bbi[None] + pi[..., None] * bbr[None]
    pcr = c_re[None] * pr[:, :, :, None, :] - c_im[None] * pi[:, :, :, None, :]
    pci = c_re[None] * pi[:, :, :, None, :] + c_im[None] * pr[:, :, :, None, :]
    kern = (jnp.einsum("dgpn,kdgnq->kdgpq", c_re, pbr[:t], precision=hp)
            - jnp.einsum("dgpn,kdgnq->kdgpq", c_im, pbi[:t], precision=hp))
    import numpy as np

    jb, gpb = S5_JB, S5_GPB
    kw, sw4 = t * LANES, 4 * S5_SW

    def expand(compact, col_src, row_group, col_group):
        onehot = jnp.asarray(col_src[None, :] == np.arange(compact.shape[-1])[:, None], BF16)
        full = jnp.dot(compact.astype(BF16), onehot, preferred_element_type=BF16)
        return jnp.where(jnp.asarray(row_group[:, None] == col_group[None, :])[None], full, 0.0)

    state_rows = np.arange(sw4)
    chunk_rows = np.arange(kw)
    state_grp, chunk_grp = (state_rows // S5_N) % gpb, (chunk_rows // A_GROUP) % gpb
    state_src = (state_rows // S5_SW) * S5_N + state_rows % S5_N
    chunk_src = (chunk_rows // LANES) * A_GROUP + chunk_rows % A_GROUP

    sidx = jnp.arange(t)
    secs = jnp.stack([pbr[t - 1 - sidx, 0], pbi[t - 1 - sidx, 0], pbr[sidx, 1], pbi[sidx, 1]], axis=0)
    secs = secs.reshape(4, t, jb, gpb, S5_N, A_GROUP)
    bst = expand(jnp.transpose(secs, (2, 1, 3, 5, 0, 4)).reshape(jb, kw, 4 * S5_N), state_src, chunk_grp, state_grp)
    secs = jnp.stack([pcr[sidx + 1, 0], -pci[sidx + 1, 0], pcr[t - sidx, 1], -pci[t - sidx, 1]], axis=0)
    secs = secs.reshape(4, t, jb, gpb, A_GROUP, S5_N)
    cst = expand(jnp.transpose(secs, (2, 0, 3, 5, 1, 4)).reshape(jb, sw4, t * A_GROUP), chunk_src, state_grp, chunk_grp)
    dlt = sidx[None, :] - sidx[:, None]
    kf = kern[jnp.clip(dlt, 0, t - 1), 0]
    kb = kern[jnp.clip(-dlt, 0, t - 1), 1]
    d4 = dlt[:, :, None, None, None]
    ktoe = jnp.where(d4 > 0, kf, 0.0) + jnp.where(d4 < 0, kb, 0.0) + jnp.where(d4 == 0, kf + kb, 0.0)
    ktoe = ktoe.reshape(t, t, jb, gpb, A_GROUP, A_GROUP)
    wt = expand(jnp.transpose(ktoe, (2, 0, 3, 5, 1, 4)).reshape(jb, kw, t * A_GROUP), chunk_src, chunk_grp, chunk_grp)
    acoef = jnp.stack([pr[t, 0], pi[t, 0], pr[t, 1], pi[t, 1]], axis=0)
    acoef = acoef.reshape(4, jb, S5_SW).transpose(1, 0, 2).reshape(jb, 1, 4 * S5_SW)
    acoef = jnp.broadcast_to(acoef, (jb, S5_BG, 4 * S5_SW))
    return bst.astype(BF16), cst.astype(BF16), wt.astype(BF16), acoef


def _s5_in_body(u_ref, bst_ref, lhs_ref, s_ref, *, seq_len, row_tile):
    nc = seq_len // S5_T

    @pl.when(pl.program_id(2) == 0)
    def _():
        def build(cp, carry):
            base = cp * (2 * S5_T)
            for s in range(S5_T):
                a = u_ref[pl.ds(base + s, S5_BG, stride=seq_len), :]
                b = u_ref[pl.ds(base + S5_T + s, S5_BG, stride=seq_len), :]
                lhs_ref[pl.ds(pl.multiple_of(cp * 16, 16), 16), s * LANES:(s + 1) * LANES] = (
                    jnp.concatenate([a, b], axis=0).astype(BF16))
            return carry

        lax.fori_loop(0, nc // 2, build, 0)

    def mm(r, carry):
        rows = pl.ds(pl.multiple_of(r * row_tile, row_tile), row_tile)
        s_ref[rows, :] = _dot(lhs_ref[rows, :], bst_ref[...])
        return carry

    lax.fori_loop(0, (nc * S5_BG) // row_tile, mm, 0)


def _s5_state_in(proj2d, bst, *, nbg, seq_len):
    nc = seq_len // S5_T
    rows = nc * S5_BG
    kw = S5_T * LANES
    half = 2 * S5_SW
    row_tile = min(rows, 256)
    return pl.pallas_call(
        functools.partial(_s5_in_body, seq_len=seq_len, row_tile=row_tile),
        grid=(nbg, S5_JB, 2),
        in_specs=[
            pl.BlockSpec((S5_BG * seq_len, LANES), lambda g, j, n: (g, j)),
            pl.BlockSpec((None, kw, half), lambda g, j, n: (j, 0, n)),
        ],
        out_specs=[
            pl.BlockSpec((None, None, rows, kw), lambda g, j, n: (g, j, 0, 0)),
            pl.BlockSpec((None, None, rows, half), lambda g, j, n: (g, j, 0, n)),
        ],
        out_shape=[
            jax.ShapeDtypeStruct((nbg, S5_JB, rows, kw), BF16),
            jax.ShapeDtypeStruct((nbg, S5_JB, rows, 4 * S5_SW), F32),
        ],
        compiler_params=_cp(("parallel", "parallel", "arbitrary")),
        name="s5_state_in",
    )(proj2d, bst)


def _s5_scan_body(s_ref, a_ref, x0_ref, x_ref, fin_ref, *, nc):
    w = S5_SW
    afr, afi = a_ref[:, 0:w], a_ref[:, w:2 * w]
    abr, abi = a_ref[:, 2 * w:3 * w], a_ref[:, 3 * w:4 * w]

    def step(c, carry):
        xfr, xfi, xbr, xbi = carry
        rf = pl.ds(pl.multiple_of(c * S5_BG, S5_BG), S5_BG)
        rb = pl.ds(pl.multiple_of((nc - 1 - c) * S5_BG, S5_BG), S5_BG)
        x_ref[rf, 0:w] = xfr
        x_ref[rf, w:2 * w] = xfi
        x_ref[rb, 2 * w:3 * w] = xbr
        x_ref[rb, 3 * w:4 * w] = xbi
        nfr = afr * xfr - afi * xfi + s_ref[rf, 0:w]
        nfi = afr * xfi + afi * xfr + s_ref[rf, w:2 * w]
        nbr = abr * xbr - abi * xbi + s_ref[rb, 2 * w:3 * w]
        nbi = abr * xbi + abi * xbr + s_ref[rb, 3 * w:4 * w]
        return nfr, nfi, nbr, nbi

    init = (x0_ref[:, 0:w], x0_ref[:, w:2 * w], x0_ref[:, 2 * w:3 * w], x0_ref[:, 3 * w:4 * w])
    xfr, xfi, xbr, xbi = lax.fori_loop(0, nc, step, init)
    fin_ref[:, 0:w] = xfr
    fin_ref[:, w:2 * w] = xfi
    fin_ref[:, 2 * w:3 * w] = xbr
    fin_ref[:, 3 * w:4 * w] = xbi


def _s5_scan(s_all, acoef, x0, *, nbg, seq_len):
    nc = seq_len // S5_T
    rows = nc * S5_BG
    sw4 = 4 * S5_SW
    return pl.pallas_call(
        functools.partial(_s5_scan_body, nc=nc),
        grid=(nbg, S5_JB),
        in_specs=[
            pl.BlockSpec((None, None, rows, sw4), lambda g, j: (g, j, 0, 0)),
            pl.BlockSpec((None, S5_BG, sw4), lambda g, j: (j, 0, 0)),
            pl.BlockSpec((None, None, S5_BG, sw4), lambda g, j: (g, j, 0, 0)),
        ],
        out_specs=[
            pl.BlockSpec((None, None, rows, sw4), lambda g, j: (g, j, 0, 0)),
            pl.BlockSpec((None, None, S5_BG, sw4), lambda g, j: (g, j, 0, 0)),
        ],
        out_shape=[
            jax.ShapeDtypeStruct((nbg, S5_JB, rows, sw4), F32),
            jax.ShapeDtypeStruct((nbg, S5_JB, S5_BG, sw4), F32),
        ],
        compiler_params=_cp(("parallel", "parallel")),
        name="s5_scan",
    )(s_all, acoef, x0)


def _s5_out_body(lhs_ref, x_ref, wt_ref, cst_ref, y_ref, acc_ref, *, chunks):
    acc_ref[...] = _dot(lhs_ref[...], wt_ref[...]) + _dot(x_ref[...].astype(BF16), cst_ref[...])

    def unchunk(c, carry):
        rows = pl.ds(pl.multiple_of(c * S5_BG, S5_BG), S5_BG)
        for t in range(S5_T):
            y_ref[:, c * S5_T + t, :] = acc_ref[rows, t * LANES:(t + 1) * LANES]
        return carry

    lax.fori_loop(0, chunks, unchunk, 0)


def _s5_output(lhs, x_all, wt, cst, *, nbg, seq_len):
    nc = seq_len // S5_T
    chunks = min(nc, 32)
    tr = chunks * S5_BG
    kw = S5_T * LANES
    sw4 = 4 * S5_SW
    return pl.pallas_call(
        functools.partial(_s5_out_body, chunks=chunks),
        grid=(nbg, S5_JB, nc // chunks),
        in_specs=[
            pl.BlockSpec((None, None, tr, kw), lambda g, j, r: (g, j, r, 0)),
            pl.BlockSpec((None, None, tr, sw4), lambda g, j, r: (g, j, r, 0)),
            pl.BlockSpec((None, kw, kw), lambda g, j, r: (j, 0, 0)),
            pl.BlockSpec((None, sw4, kw), lambda g, j, r: (j, 0, 0)),
        ],
        out_specs=pl.BlockSpec((S5_BG, chunks * S5_T, LANES), lambda g, j, r: (g, r, j)),
        out_shape=jax.ShapeDtypeStruct((nbg * S5_BG, seq_len, MIX_A), F32),
        scratch_shapes=[pltpu.VMEM((tr, kw), F32)],
        compiler_params=_cp(("parallel", "parallel", "arbitrary")),
        name="s5_output",
    )(lhs, x_all, wt, cst)


def _s5_mix(proj2d, ops, x0, *, nbg, seq_len):
    bst, cst, wt, acoef = ops
    lhs, s_all = _s5_state_in(proj2d, bst, nbg=nbg, seq_len=seq_len)
    x_all, fin = _s5_scan(s_all, acoef, x0, nbg=nbg, seq_len=seq_len)
    y = _s5_output(lhs, x_all, wt, cst, nbg=nbg, seq_len=seq_len)
    return y, fin


def _s5_pack_state(re, im):
    b = re.shape[0]
    x = jnp.stack([re[:, 0], im[:, 0], re[:, 1], im[:, 1]], axis=1)
    x = x.reshape(b // S5_BG, S5_BG, 4, S5_JB, S5_SW)
    return jnp.transpose(x, (0, 3, 1, 2, 4)).reshape(b // S5_BG, S5_JB, S5_BG, 4 * S5_SW)


def _s5_unpack_state(fin):
    nbg = fin.shape[0]
    x = fin.reshape(nbg, S5_JB, S5_BG, 4, S5_SW)
    x = jnp.transpose(x, (0, 2, 3, 1, 4)).reshape(nbg * S5_BG, 4, A_GROUPS, S5_N)
    return jnp.stack([x[:, 0], x[:, 2]], axis=1), jnp.stack([x[:, 1], x[:, 3]], axis=1)


Q_BLK, K_BLK, V_BLK = MIX_A // LANES, (MIX_A + MIX_B) // LANES, (MIX_A + 2 * MIX_B) // LANES


def _pair_softmax_pv(q, parts):
    lane = lax.broadcasted_iota(jnp.int32, (1, LANES), 1)
    outs = []
    for half in range(2):
        sel = (lane < NA_HD) if half == 0 else (lane >= NA_HD)
        qm = jnp.where(sel, q, 0.0).astype(BF16)
        scores = []
        for k, _, bias in parts:
            s = _dot_nt(qm, k)
            if bias is not None:
                s = s + bias[half]
            scores.append(s)
        m = scores[0].max(axis=-1, keepdims=True)
        for s in scores[1:]:
            m = jnp.maximum(m, s.max(axis=-1, keepdims=True))
        l = None
        o = None
        for s, (_, v, _) in zip(scores, parts):
            p = jnp.exp(s - m)
            ls = p.sum(axis=-1, keepdims=True)
            os_ = _dot(p.astype(BF16), v)
            l = ls if l is None else l + ls
            o = os_ if o is None else o + os_
        outs.append(o / l)
    return jnp.where(lane < NA_HD, outs[0], outs[1]).astype(BF16)


def _ctx_attn_body(q_ref, k_ref, v_ref, o_ref):
    q = q_ref[...] * (NA_HD ** -0.5)
    o_ref[...] = _pair_softmax_pv(q, [(k_ref[...].astype(BF16), v_ref[...].astype(BF16), None)])


def _context_attention(proj, *, nb, seq_len):
    npair = NA_HEADS // 2
    return pl.pallas_call(
        _ctx_attn_body,
        grid=(nb, npair),
        in_specs=[
            pl.BlockSpec((seq_len, LANES), lambda b, h: (b, Q_BLK + h)),
            pl.BlockSpec((seq_len, LANES), lambda b, h: (b, K_BLK + h)),
            pl.BlockSpec((seq_len, LANES), lambda b, h: (b, V_BLK + h)),
        ],
        out_specs=pl.BlockSpec((seq_len, LANES), lambda b, h: (b, h)),
        out_shape=jax.ShapeDtypeStruct((nb * seq_len, MIX_B), BF16),
        compiler_params=_cp(("parallel", "parallel")),
        name="context_attention",
    )(proj, proj, proj)


NA_QROWS = 4
NA_WROWS = NA_QROWS + NA_KH


def _na_window_start(g, rows):
    rs = jnp.clip(g * NA_QROWS - NA_KH // 2, 0, rows - NA_KH)
    return jnp.minimum(rs, rows - NA_WROWS)


def _na_bias_tables(rpb, rows):
    import numpy as np

    ngroups = rows // NA_QROWS
    shp = (NA_QROWS * GRID_W, NA_WROWS * GRID_W)
    ndc = 2 * NA_KW - 1
    c, kc = np.meshgrid(np.arange(GRID_W), np.arange(GRID_W), indexing="ij")
    cs = np.clip(c - NA_KW // 2, 0, GRID_W - NA_KW)
    col_ok = (kc >= cs) & (kc < cs + NA_KW)
    col_sel = ((kc - c + NA_KW - 1)[None] == np.arange(ndc)[:, None, None]) & col_ok[None]
    dr_idx, valid = [], []
    for g in (0, 1, ngroups - 1):
        rs0 = int(np.clip(g * NA_QROWS - NA_KH // 2, 0, rows - NA_KH))
        ws = min(rs0, rows - NA_WROWS)
        ri, wi = np.meshgrid(np.arange(NA_QROWS), np.arange(NA_WROWS), indexing="ij")
        r = g * NA_QROWS + ri
        rs = np.clip(r - NA_KH // 2, 0, rows - NA_KH)
        row = ws + wi
        row_ok = (row >= rs) & (row < rs + NA_KH)
        dr_idx.append(np.clip(row - r + NA_KH - 1, 0, 2 * NA_KH - 2))
        valid.append((row_ok[:, None, :, None] & col_ok[None, :, None, :]).reshape(shp))
    rows_sel = rpb[:, np.stack(dr_idx)]
    t = jnp.einsum("hgrwd,dck->hgrcwk", rows_sel, jnp.asarray(col_sel, F32), precision=lax.Precision.HIGHEST)
    t = jnp.where(jnp.asarray(np.stack(valid))[None], t.reshape(NA_HEADS, 3, *shp), NEG_BIG)
    t = t.reshape(NA_HEADS // 2, 2, 3, *shp)
    return jnp.transpose(t, (0, 2, 1, 3, 4))


def _nbr_attn_body(q_ref, k_ref, v_ref, ck_ref, cv_ref, bias_ref, o_ref, *, rows):
    g = pl.program_id(2)
    ngroups = rows // NA_QROWS
    case = jnp.where(g == 0, 0, jnp.where(g == ngroups - 1, 2, 1))
    ws = _na_window_start(g, rows)
    win = pl.ds(pl.multiple_of(ws * GRID_W, GRID_W), NA_WROWS * GRID_W)
    q = q_ref[...] * (NA_HD ** -0.5)
    kw = k_ref[win, :].astype(BF16)
    vw = v_ref[win, :].astype(BF16)
    bias = (bias_ref[case, 0], bias_ref[case, 1])
    o_ref[...] = _pair_softmax_pv(q, [(kw, vw, bias),
                                      (ck_ref[...].astype(BF16), cv_ref[...].astype(BF16), None)])


def _neighbourhood_attention(proj, ck, cv, bias, *, nb, seq_len):
    rows = seq_len // GRID_W
    ngroups = rows // NA_QROWS
    npair = NA_HEADS // 2
    qt = NA_QROWS * GRID_W
    past = ck.shape[1]
    return pl.pallas_call(
        functools.partial(_nbr_attn_body, rows=rows),
        grid=(npair, nb, ngroups),
        in_specs=[
            pl.BlockSpec((qt, LANES), lambda h, b, g: (b * ngroups + g, Q_BLK + h)),
            pl.BlockSpec((seq_len, LANES), lambda h, b, g: (b, K_BLK + h)),
            pl.BlockSpec((seq_len, LANES), lambda h, b, g: (b, V_BLK + h)),
            pl.BlockSpec((None, past, LANES), lambda h, b, g: (b, 0, h)),
            pl.BlockSpec((None, past, LANES), lambda h, b, g: (b, 0, h)),
            pl.BlockSpec((None, 3, 2, qt, NA_WROWS * GRID_W), lambda h, b, g: (h, 0, 0, 0, 0)),
        ],
        out_specs=pl.BlockSpec((qt, LANES), lambda h, b, g: (b * ngroups + g, h)),
        out_shape=jax.ShapeDtypeStruct((nb * seq_len, MIX_B), BF16),
        compiler_params=_cp(("parallel", "parallel", "parallel")),
        name="neighbourhood_attention",
    )(proj, proj, proj, ck, cv, bias)


def _even_out_body(x_ref, y_ref, u_ref, bo_ref, d_ref, wg_ref, bg_ref, wo_ref, gt_ref, o_ref):
    z = _gelu_tanh(y_ref[...] + d_ref[...] * u_ref[...])
    a = z * _sigmoid(_dot(z.astype(BF16), wg_ref[...]) + bg_ref[...])
    cat = jnp.concatenate([a.astype(BF16), bo_ref[...]], axis=-1)
    o_ref[...] = x_ref[...] + gt_ref[...] * _dot(cat, wo_ref[...])


def _even_out(x, y, proj, bo, d_row, wg_bf, bg_row, wo_bf, modt_l, seq_fn, *, tm):
    t = x.shape[0]
    return pl.pallas_call(
        _even_out_body,
        grid=(t // tm,),
        in_specs=[
            pl.BlockSpec((tm, D_MODEL), lambda i: (i, 0)),
            pl.BlockSpec((tm, MIX_A), lambda i: (i, 0)),
            pl.BlockSpec((tm, MIX_A), lambda i: (i, 0)),
            pl.BlockSpec((tm, MIX_B), lambda i: (i, 0)),
            pl.BlockSpec((1, MIX_A), lambda i: (0, 0)),
            pl.BlockSpec((MIX_A, MIX_A), lambda i: (0, 0)),
            pl.BlockSpec((1, MIX_A), lambda i: (0, 0)),
            pl.BlockSpec((D_MODEL, D_MODEL), lambda i: (0, 0)),
            _mod_spec(2, seq_fn, 1),
        ],
        out_specs=pl.BlockSpec((tm, D_MODEL), lambda i: (i, 0)),
        out_shape=jax.ShapeDtypeStruct((t, D_MODEL), F32),
        compiler_params=_cp(("parallel",)),
        name="even_out",
    )(x, y, proj, bo, d_row, wg_bf, bg_row, wo_bf, modt_l)


DN_QKV = 3 * D_MODEL
DN_PROJ_PAD = 4 * D_MODEL + LANES
DN_GATE_BLK = (4 * D_MODEL) // LANES
DN_CB = 512
DN_HALO = SUBLANES


def _gdn_pre_body(x_ref, w_ref, o_ref, pad_ref, *, seq_len, rt):
    kind = pl.program_id(1) // (D_MODEL // DN_CB)
    zeros = jnp.zeros((DN_HALO, DN_CB), F32)
    pad_ref[0:DN_HALO, :] = zeros
    pad_ref[DN_HALO + seq_len:DN_HALO + seq_len + DN_HALO, :] = zeros
    pad_ref[DN_HALO:DN_HALO + seq_len, :] = x_ref[...].astype(F32)
    qscale = jnp.where(kind == 0, DN_DK ** -0.5, 1.0)
    lo = DN_HALO - (DN_CONV - 1) // 2

    def tile(r, carry):
        base = pl.multiple_of(r * rt, rt)
        xt = pad_ref[pl.ds(base, rt + 2 * DN_HALO), :]
        y = None
        for k in range(DN_CONV):
            term = w_ref[k:k + 1, :] * xt[lo + k:lo + k + rt, :]
            y = term if y is None else y + term
        y = _silu(y)
        segs = []
        for h in range(DN_CB // DN_DK):
            seg = y[:, h * DN_DK:(h + 1) * DN_DK]
            nrm = seg * lax.rsqrt(jnp.sum(seg * seg, axis=-1, keepdims=True) + EPS) * qscale
            segs.append(jnp.where(kind == 2, seg, nrm))
        o_ref[pl.ds(base, rt), :] = jnp.concatenate(segs, axis=-1).astype(BF16)
        return carry

    lax.fori_loop(0, seq_len // rt, tile, 0)


def _gdn_pre(proj, conv_w, *, nb, seq_len):
    rt = 256
    return pl.pallas_call(
        functools.partial(_gdn_pre_body, seq_len=seq_len, rt=rt),
        grid=(nb, DN_QKV // DN_CB),
        in_specs=[
            pl.BlockSpec((seq_len, DN_CB), lambda b, c: (b, c)),
            pl.BlockSpec((DN_CONV, DN_CB), lambda b, c: (0, c)),
        ],
        out_specs=pl.BlockSpec((seq_len, DN_CB), lambda b, c: (b, c)),
        out_shape=jax.ShapeDtypeStruct((nb * seq_len, DN_QKV), BF16),
        scratch_shapes=[pltpu.VMEM((seq_len + 2 * DN_HALO, DN_CB), F32)],
        compiler_params=_cp(("parallel", "parallel")),
        name="gdn_pre",
    )(proj, conv_w)


def _split_bf16(x, n):
    parts = []
    r = x
    for _ in range(n):
        p = r.astype(BF16)
        parts.append(p)
        r = r - p.astype(F32)
    return parts


def _mm_exact_lhs(a_bf, b):
    return sum(_dot(a_bf, p) for p in _split_bf16(b, 3))


def _softplus(x):
    return jnp.maximum(x, 0.0) + jnp.log(1.0 + jnp.exp(-jnp.abs(x)))


def _gdn_body(q_ref, k_ref, v_ref, gt_ref, ga_ref, gb_ref, s0_ref, o_ref, sl_ref,
              s_scr, vn_scr, a_scr, t_scr, m_scr, attn_scr, rhs_scr, u_scr, w_scr, qg_scr, kg_scr, *, direction):
    t = pl.program_id(1)
    nt = pl.num_programs(1)
    tile = GDN_TILE
    c_len = DN_CHUNK
    nchunk = tile // c_len
    heads = range(DN_HEADS)

    @pl.when(t == 0)
    def _():
        s_scr[...] = s0_ref[...]
        vn_scr[...] = jnp.zeros_like(vn_scr)

    ri = lax.broadcasted_iota(jnp.int32, (tile, tile), 0)
    ci = lax.broadcasted_iota(jnp.int32, (tile, tile), 1)
    same = (ri // c_len) == (ci // c_len)
    if direction == 0:
        incl = same & (ci <= ri)
        strict = same & (ci < ri)
    else:
        incl = same & (ci >= ri)
        strict = same & (ci > ri)
    tri_bf = jnp.where(incl, 1.0, 0.0).astype(BF16)
    blk_bf = jnp.where(same, 1.0, 0.0).astype(BF16)
    nlev = int(math.log2(c_len))

    def off_mask(lvl, rs=slice(0, tile)):
        nrow = rs.stop - rs.start
        r = lax.broadcasted_iota(jnp.int32, (nrow, tile), 0) + rs.start
        c = lax.broadcasted_iota(jnp.int32, (nrow, tile), 1)
        late, early = (r, c) if direction == 0 else (c, r)
        return (((late >> lvl) & 1) == 1) & (((early >> lvl) & 1) == 0) & ((r >> (lvl + 1)) == (c >> (lvl + 1)))

    def active_blocks(lvl):
        hb = 1 << lvl
        return [slice(rb * hb, (rb + 1) * hb) for rb in range(tile // hb) if rb % 2 == 1 - direction]

    gt = gt_ref[...].astype(F32)
    g_all = ga_ref[...] * _softplus(gt + gb_ref[...])
    beta_all = _sigmoid(gt)
    gc_all = _mm_exact_lhs(tri_bf, g_all)
    gl_all = _mm_exact_lhs(blk_bf, g_all)
    gc_t = jnp.transpose(gc_all)
    chunk_order = range(nchunk) if direction == 0 else range(nchunk - 1, -1, -1)
    eye = jnp.where(ri == ci, 1.0, 0.0)

    def lane(h):
        return direction * DN_HEADS + h

    def hsl(h):
        return slice(h * DN_DK, (h + 1) * DN_DK)

    for h in heads:
        gcol = gc_all[:, lane(h):lane(h) + 1]
        grow = gc_t[lane(h):lane(h) + 1, :]
        glcol = gl_all[:, lane(h):lane(h) + 1]
        bcol = beta_all[:, 2 * DN_HEADS + lane(h):2 * DN_HEADS + lane(h) + 1]
        q_bf, k_bf = q_ref[:, hsl(h)], k_ref[:, hsl(h)]
        qh, kh, vh = q_bf.astype(F32), k_bf.astype(F32), v_ref[:, hsl(h)].astype(F32)
        decay = jnp.exp(jnp.where(incl, gcol - grow, NEG_BIG))
        a_mat = jnp.where(strict, bcol * _dot_nt(k_bf, k_bf) * decay, 0.0)
        a_scr[h] = a_mat
        t_scr[h] = (eye - jnp.where(off_mask(0), a_mat, 0.0)).astype(BF16)
        attn_scr[h] = (_dot_nt(q_bf, k_bf) * decay).astype(BF16)
        egc = jnp.exp(gcol)
        rhs_scr[h] = jnp.concatenate([vh * bcol, kh * (bcol * egc)], axis=-1).astype(BF16)
        qg_scr[h] = (qh * egc).astype(BF16)
        kg_scr[h] = (kh * jnp.exp(glcol - gcol)).astype(BF16)

    for lvl in range(1, nlev):
        hb = 1 << lvl
        if hb % 16:
            for h in heads:
                m_scr[h] = _dot(jnp.where(off_mask(lvl), a_scr[h], 0.0).astype(BF16), t_scr[h]).astype(BF16)
            for h in heads:
                t_scr[h] = t_scr[h] - _dot(t_scr[h], m_scr[h]).astype(BF16)
            continue
        blocks = active_blocks(lvl)
        for h in heads:
            lhs = jnp.concatenate([jnp.where(off_mask(lvl, rs), a_scr[h, rs, :], 0.0) for rs in blocks], axis=0)
            m_act = _dot(lhs.astype(BF16), t_scr[h])
            zero = jnp.zeros((hb, tile), F32)
            pieces = [zero] * (tile // hb)
            for i, rs in enumerate(blocks):
                pieces[rs.start // hb] = m_act[i * hb:(i + 1) * hb, :]
            m_scr[h] = jnp.concatenate(pieces, axis=0).astype(BF16)
        for h in heads:
            t_act = jnp.concatenate([t_scr[h, rs, :].astype(F32) for rs in blocks], axis=0)
            upd = _dot(t_act.astype(BF16), m_scr[h])
            for i, rs in enumerate(blocks):
                t_scr[h, rs, :] = t_scr[h, rs, :] - upd[i * hb:(i + 1) * hb, :].astype(BF16)

    for h in heads:
        x = _dot(t_scr[h], rhs_scr[h])
        u_scr[h] = x[:, :DN_DV]
        w_scr[h] = x[:, DN_DV:].astype(BF16)

    for c in chunk_order:
        rows = slice(c * c_len, (c + 1) * c_len)
        for h in heads:
            s = s_scr[h]
            s_bf = s.astype(BF16)
            v_new = u_scr[h, rows, :] - _dot(w_scr[h, rows, :], s_bf)
            vn_scr[h, rows, :] = v_new.astype(BF16)
            o_ref[rows, hsl(h)] = (_dot(qg_scr[h, rows, :], s_bf)
                                   + _dot(attn_scr[h, rows, :], vn_scr[h])).astype(o_ref.dtype)
            gl_c = gl_all[c * c_len:c * c_len + 1, lane(h):lane(h) + 1]
            s_scr[h] = s * jnp.exp(gl_c) + _dot_tn(kg_scr[h, rows, :], v_new.astype(BF16))

    @pl.when(t == nt - 1)
    def _():
        sl_ref[...] = s_scr[...]


def _gdn_scan(qkv, proj, ga_row, gb_row, s0, *, nb, seq_len, direction):
    nt = seq_len // GDN_TILE
    tile = GDN_TILE
    nq = D_MODEL // D_MODEL

    def row(b, t):
        return b * nt + (t if direction == 0 else nt - 1 - t)

    return pl.pallas_call(
        functools.partial(_gdn_body, direction=direction),
        grid=(nb, nt),
        in_specs=[
            pl.BlockSpec((tile, D_MODEL), lambda b, t: (row(b, t), 0)),
            pl.BlockSpec((tile, D_MODEL), lambda b, t: (row(b, t), nq)),
            pl.BlockSpec((tile, D_MODEL), lambda b, t: (row(b, t), 2 * nq)),
            pl.BlockSpec((tile, LANES), lambda b, t: (row(b, t), DN_GATE_BLK)),
            pl.BlockSpec((1, LANES), lambda b, t: (0, 0)),
            pl.BlockSpec((1, LANES), lambda b, t: (0, 0)),
            pl.BlockSpec((None, DN_HEADS, DN_DK, DN_DV), lambda b, t: (b, 0, 0, 0)),
        ],
        out_specs=[
            pl.BlockSpec((tile, D_MODEL), lambda b, t: (row(b, t), 0)),
            pl.BlockSpec((None, DN_HEADS, DN_DK, DN_DV), lambda b, t: (b, 0, 0, 0)),
        ],
        out_shape=[
            jax.ShapeDtypeStruct((nb * seq_len, D_MODEL), BF16),
            jax.ShapeDtypeStruct((nb, DN_HEADS, DN_DK, DN_DV), F32),
        ],
        scratch_shapes=[
            pltpu.VMEM((DN_HEADS, DN_DK, DN_DV), F32),
            pltpu.VMEM((DN_HEADS, tile, DN_DV), BF16),
            pltpu.VMEM((DN_HEADS, tile, tile), F32),
            pltpu.VMEM((DN_HEADS, tile, tile), BF16),
            pltpu.VMEM((DN_HEADS, tile, tile), BF16),
            pltpu.VMEM((DN_HEADS, tile, tile), BF16),
            pltpu.VMEM((DN_HEADS, tile, 2 * DN_DV), BF16),
            pltpu.VMEM((DN_HEADS, tile, DN_DV), F32),
            pltpu.VMEM((DN_HEADS, tile, DN_DK), BF16),
            pltpu.VMEM((DN_HEADS, tile, DN_DK), BF16),
            pltpu.VMEM((DN_HEADS, tile, DN_DK), BF16),
        ],
        compiler_params=_cp(("parallel", "arbitrary")),
        name="gdn_scan_fwd" if direction == 0 else "gdn_scan_bwd",
    )(qkv, qkv, qkv, proj, ga_row, gb_row, s0)


def _odd_out_body(x_ref, of_ref, ob_ref, z_ref, ng_ref, wo_ref, gt_ref, o_ref):
    o = of_ref[...].astype(F32) + ob_ref[...].astype(F32)
    segs = []
    for h in range(DN_HEADS):
        seg = o[:, h * DN_DV:(h + 1) * DN_DV]
        segs.append(seg * lax.rsqrt(jnp.mean(seg * seg, axis=-1, keepdims=True) + EPS))
    y = jnp.concatenate(segs, axis=-1) * ng_ref[...] * _silu(z_ref[...].astype(F32))
    o_ref[...] = x_ref[...] + gt_ref[...] * _dot(y.astype(BF16), wo_ref[...])


def _odd_out(x, o_f, o_b, proj, ng_row, wo_bf, modt_l, seq_fn, *, tm):
    t = x.shape[0]
    zblk = DN_QKV // D_MODEL
    return pl.pallas_call(
        _odd_out_body,
        grid=(t // tm,),
        in_specs=[
            pl.BlockSpec((tm, D_MODEL), lambda i: (i, 0)),
            pl.BlockSpec((tm, D_MODEL), lambda i: (i, 0)),
            pl.BlockSpec((tm, D_MODEL), lambda i: (i, 0)),
            pl.BlockSpec((tm, D_MODEL), lambda i: (i, zblk)),
            pl.BlockSpec((1, D_MODEL), lambda i: (0, 0)),
            pl.BlockSpec((D_MODEL, D_MODEL), lambda i: (0, 0)),
            _mod_spec(2, seq_fn, 1),
        ],
        out_specs=pl.BlockSpec((tm, D_MODEL), lambda i: (i, 0)),
        out_shape=jax.ShapeDtypeStruct((t, D_MODEL), F32),
        compiler_params=_cp(("parallel",)),
        name="odd_out",
    )(x, o_f, o_b, proj, ng_row, wo_bf, modt_l)


ROW_TILE = 512


def _seq_fn(slot0, seq_len, tm):
    if slot0 == 0:
        return lambda i: 0
    return lambda i: slot0 + (i * tm) // seq_len


def kernel(x_prompt, x_sample, c, cache_na_k, cache_na_v, state_s5_re, state_s5_im, state_dn, c_ctx, norm_mix_g, norm_ff_g, w_mod, b_mod, w_ff1, w_ff2, w_in_e, w_out_e, s5_lam_re, s5_lam_im, s5_log_dt, s5_b_re, s5_b_im, s5_c_re, s5_c_im, s5_d, s5_w_glu, s5_b_glu, na_rpb, w_in_o, dn_conv_w, dn_a_log, dn_dt_bias, dn_norm_g, w_out_o, final_norm_g):
    nbp, lp = x_prompt.shape[:2]
    nbs, ls = x_sample.shape[:2]
    assert nbp % S5_BG == 0 and nbs % S5_BG == 0 and 1 + nbs <= 16
    streams = [
        dict(x=x_prompt.reshape(nbp * lp, D_MODEL), nb=nbp, L=lp, slot0=0),
        dict(x=x_sample.reshape(nbs * ls, D_MODEL), nb=nbs, L=ls, slot0=1),
    ]
    cond16 = jnp.zeros((16, D_MODEL), F32).at[0].set(c_ctx).at[1:1 + nbs].set(c)
    mod = _modulation(cond16, w_mod.astype(BF16), b_mod[:, None, :])
    modt = jnp.transpose(mod.reshape(DEPTH, 16, 6, D_MODEL), (0, 2, 1, 3))[:, :, :, None, :]

    new_k, new_v, new_sre, new_sim, new_dn = [], [], [], [], []
    for l in range(DEPTH):
        e = l // 2
        if l % 2 == 0:
            w_in = w_in_e[e].astype(BF16)
            w_out = w_out_e[e].astype(BF16)
            w_glu = s5_w_glu[e].astype(BF16)
            ops = _s5_operators(s5_lam_re[e], s5_lam_im[e], s5_log_dt[e], s5_b_re[e], s5_b_im[e],
                                s5_c_re[e], s5_c_im[e])
            bias = _na_bias_tables(na_rpb[e], ls // GRID_W)
        else:
            w_in = jnp.pad(w_in_o[e], ((0, 0), (0, DN_PROJ_PAD - w_in_o.shape[-1]))).astype(BF16)
            w_out = w_out_o[e].astype(BF16)
            ga_row = jnp.zeros((1, LANES), F32).at[0, :2 * DN_HEADS].set(-jnp.exp(dn_a_log[e]).reshape(-1))
            gb_row = jnp.zeros((1, LANES), F32).at[0, :2 * DN_HEADS].set(dn_dt_bias[e].reshape(-1))
            ng_row = jnp.tile(dn_norm_g[e], DN_HEADS)[None]
        w1 = w_ff1[l].astype(BF16)
        w2 = w_ff2[l].astype(BF16)
        for st in streams:
            x, nb, sl = st["x"], st["nb"], st["L"]
            seq = _seq_fn(st["slot0"], sl, ROW_TILE)
            is_ctx = st["slot0"] == 0
            if l % 2 == 0:
                proj = _norm_mod_matmul(x, norm_mix_g[l][None], modt[l], w_in, seq, tm=ROW_TILE, tn=1024,
                                        name="in_proj_even")
                nbg = nb // S5_BG
                if is_ctx:
                    x0 = jnp.zeros((nbg, S5_JB, S5_BG, 4 * S5_SW), F32)
                else:
                    x0 = _s5_pack_state(state_s5_re[:, e], state_s5_im[:, e])
                y, fin = _s5_mix(proj, ops, x0, nbg=nbg, seq_len=sl)
                if is_ctx:
                    bo = _context_attention(proj, nb=nb, seq_len=sl)
                    kv = proj.reshape(nb, sl, -1)
                    new_k.append(kv[:, :, MIX_A + MIX_B:MIX_A + 2 * MIX_B].reshape(nb, sl, NA_HEADS, NA_HD))
                    new_v.append(kv[:, :, MIX_A + 2 * MIX_B:].reshape(nb, sl, NA_HEADS, NA_HD))
                    sre, sim = _s5_unpack_state(fin)
                    new_sre.append(sre)
                    new_sim.append(sim)
                else:
                    past = cache_na_k.shape[2]
                    ck = cache_na_k[:, e].reshape(nb, past, MIX_B)
                    cv = cache_na_v[:, e].reshape(nb, past, MIX_B)
                    bo = _neighbourhood_attention(proj, ck, cv, bias, nb=nb, seq_len=sl)
                x = _even_out(x, y.reshape(nb * sl, MIX_A), proj, bo, s5_d[e][None], w_glu, s5_b_glu[e][None],
                              w_out, modt[l], seq, tm=ROW_TILE)
            else:
                proj = _norm_mod_matmul(x, norm_mix_g[l][None], modt[l], w_in, seq, tm=ROW_TILE,
                                        tn=DN_PROJ_PAD // 3, name="in_proj_odd", out_dtype=BF16)
                qkv = _gdn_pre(proj, dn_conv_w[e], nb=nb, seq_len=sl)
                if is_ctx:
                    s0 = jnp.zeros((nb, 2, DN_HEADS, DN_DK, DN_DV), F32)
                else:
                    s0 = state_dn[:, e]
                o_f, s_f = _gdn_scan(qkv, proj, ga_row, gb_row, s0[:, 0], nb=nb, seq_len=sl, direction=0)
                o_b, s_b = _gdn_scan(qkv, proj, ga_row, gb_row, s0[:, 1], nb=nb, seq_len=sl, direction=1)
                if is_ctx:
                    new_dn.append(jnp.stack([s_f, s_b], axis=1))
                x = _odd_out(x, o_f, o_b, proj, ng_row, w_out, modt[l], seq, tm=ROW_TILE)
            st["x"] = _mlp(x, norm_ff_g[l][None], modt[l], w1, w2, seq, final_norm_g[None], tm=ROW_TILE, tf=1024,
                           final_norm=(l == DEPTH - 1))

    y_prompt = streams[0]["x"].reshape(nbp, lp, D_MODEL)
    y_sample = streams[1]["x"].reshape(nbs, ls, D_MODEL)
    return (y_prompt, y_sample, jnp.stack(new_k, axis=1), jnp.stack(new_v, axis=1),
            jnp.stack(new_sre, axis=1), jnp.stack(new_sim, axis=1), jnp.stack(new_dn, axis=1))
```

```python
import functools
import math

import jax
import jax.numpy as jnp
from jax import lax
from jax.experimental import pallas as pl
from jax.experimental.pallas import tpu as pltpu

F32 = jnp.float32
BF16 = jnp.bfloat16

D_MODEL = 1024
DEPTH = 2
GRID_W = 64
MIX_A = D_MODEL // 2
A_GROUP = 16
A_GROUPS = MIX_A // A_GROUP
S5_N = 64
MIX_B = D_MODEL - MIX_A
NA_HEADS = 8
NA_HD = MIX_B // NA_HEADS
NA_KH = 8
NA_KW = 16
DN_HEADS = 8
DN_DK = D_MODEL // DN_HEADS
DN_DV = D_MODEL // DN_HEADS
DN_CONV = 5
DN_CHUNK = 64
D_FF = 4 * D_MODEL
EPS = 1e-6

LANES = 128
SUBLANES = 8
VMEM_LIMIT = 56 * 1024 * 1024
NEG_BIG = -1e30

S5_T = 16
S5_BG = SUBLANES
S5_JB = MIX_A // LANES
S5_GPB = LANES // A_GROUP
S5_SW = S5_GPB * S5_N
GDN_TILE = 256


def _cp(sem, vmem=VMEM_LIMIT):
    return pltpu.CompilerParams(dimension_semantics=sem, vmem_limit_bytes=vmem)


def _rms(x, g):
    return x * lax.rsqrt(jnp.mean(x * x, axis=-1, keepdims=True) + EPS) * g


def _sigmoid(x):
    return 1.0 / (1.0 + jnp.exp(-x))


def _silu(x):
    return x * _sigmoid(x)


def _gelu_tanh(x):
    return 0.5 * x * (1.0 + jnp.tanh(math.sqrt(2.0 / math.pi) * (x + 0.044715 * (x * x * x))))


def _dot(a, b):
    return jnp.dot(a, b, preferred_element_type=F32)


def _dot_nt(a, b):
    return lax.dot_general(a, b, (((1,), (1,)), ((), ())), preferred_element_type=F32)


def _dot_tn(a, b):
    return lax.dot_general(a, b, (((0,), (0,)), ((), ())), preferred_element_type=F32)


def _mod_body(c_ref, w_ref, b_ref, o_ref):
    c = c_ref[...]
    o_ref[...] = _dot(_silu(c).astype(BF16), w_ref[...].astype(BF16)) + b_ref[...]


def _modulation(cond16, w_mod_bf, b_mod):
    tn = 1536
    n = 6 * D_MODEL
    return pl.pallas_call(
        _mod_body,
        grid=(DEPTH, n // tn),
        in_specs=[
            pl.BlockSpec((16, D_MODEL), lambda l, j: (0, 0)),
            pl.BlockSpec((None, D_MODEL, tn), lambda l, j: (l, 0, j)),
            pl.BlockSpec((None, 1, tn), lambda l, j: (l, 0, j)),
        ],
        out_specs=pl.BlockSpec((None, 16, tn), lambda l, j: (l, 0, j)),
        out_shape=jax.ShapeDtypeStruct((DEPTH, 16, n), F32),
        compiler_params=_cp(("parallel", "parallel")),
        name="modulation",
    )(cond16, w_mod_bf, b_mod)


def _mod_spec(which, seq_fn, ngrid):
    if ngrid == 1:
        return pl.BlockSpec((None, None, 1, D_MODEL), lambda i: (which, seq_fn(i), 0, 0))
    return pl.BlockSpec((None, None, 1, D_MODEL), lambda i, j: (which, seq_fn(i), 0, 0))


def _nmm_body(x_ref, g_ref, sh_ref, sc_ref, w_ref, o_ref):
    h = _rms(x_ref[...], g_ref[...]) * (1.0 + sc_ref[...]) + sh_ref[...]
    o_ref[...] = _dot(h.astype(BF16), w_ref[...]).astype(o_ref.dtype)


def _norm_mod_matmul(x, g_row, modt_l, w_bf, seq_fn, *, tm, name, out_dtype=F32):
    t, n = x.shape[0], w_bf.shape[1]
    return pl.pallas_call(
        _nmm_body,
        grid=(t // tm,),
        in_specs=[
            pl.BlockSpec((tm, D_MODEL), lambda i: (i, 0)),
            pl.BlockSpec((1, D_MODEL), lambda i: (0, 0)),
            _mod_spec(0, seq_fn, 1),
            _mod_spec(1, seq_fn, 1),
            pl.BlockSpec((D_MODEL, n), lambda i: (0, 0)),
        ],
        out_specs=pl.BlockSpec((tm, n), lambda i: (i, 0)),
        out_shape=jax.ShapeDtypeStruct((t, n), out_dtype),
        compiler_params=_cp(("parallel",)),
        name=name,
    )(x, g_row, modt_l, modt_l, w_bf)


def _mlp_body(x_ref, g_ref, sh_ref, sc_ref, gt_ref, w1_ref, w2_ref, fg_ref, o_ref, h_ref, acc_ref, *, final_norm):
    k = pl.program_id(1)

    @pl.when(k == 0)
    def _():
        h = _rms(x_ref[...], g_ref[...]) * (1.0 + sc_ref[...]) + sh_ref[...]
        h_ref[...] = h.astype(BF16)
        acc_ref[...] = jnp.zeros_like(acc_ref)

    a = jnp.maximum(_dot(h_ref[...], w1_ref[...]), 0.0)
    acc_ref[...] += _dot((a * a).astype(BF16), w2_ref[...])

    @pl.when(k == pl.num_programs(1) - 1)
    def _():
        y = x_ref[...] + gt_ref[...] * acc_ref[...]
        o_ref[...] = _rms(y, fg_ref[...]) if final_norm else y


def _mlp(x, g_row, modt_l, w1_bf, w2_bf, seq_fn, fg_row, *, tm, tf, final_norm):
    t = x.shape[0]
    return pl.pallas_call(
        functools.partial(_mlp_body, final_norm=final_norm),
        grid=(t // tm, D_FF // tf),
        in_specs=[
            pl.BlockSpec((tm, D_MODEL), lambda i, k: (i, 0)),
            pl.BlockSpec((1, D_MODEL), lambda i, k: (0, 0)),
            _mod_spec(3, seq_fn, 2),
            _mod_spec(4, seq_fn, 2),
            _mod_spec(5, seq_fn, 2),
            pl.BlockSpec((D_MODEL, tf), lambda i, k: (0, k)),
            pl.BlockSpec((tf, D_MODEL), lambda i, k: (k, 0)),
            pl.BlockSpec((1, D_MODEL), lambda i, k: (0, 0)),
        ],
        out_specs=pl.BlockSpec((tm, D_MODEL), lambda i, k: (i, 0)),
        out_shape=jax.ShapeDtypeStruct((t, D_MODEL), F32),
        scratch_shapes=[pltpu.VMEM((tm, D_MODEL), BF16), pltpu.VMEM((tm, D_MODEL), F32)],
        compiler_params=_cp(("parallel", "arbitrary")),
        name="mlp",
    )(x, g_row, modt_l, modt_l, modt_l, w1_bf, w2_bf, fg_row)


def _s5_operators(lam_re, lam_im, log_dt, b_re, b_im, c_re, c_im):
    hp = lax.Precision.HIGHEST
    t = S5_T
    dt = jnp.exp(log_dt)[..., None]
    ar, ai = lam_re * dt, lam_im * dt
    er = jnp.exp(ar)
    lbr, lbi = er * jnp.cos(ai), er * jnp.sin(ai)
    den = lam_re * lam_re + lam_im * lam_im
    fr = ((lbr - 1.0) * lam_re + lbi * lam_im) / den
    fi = (lbi * lam_re - (lbr - 1.0) * lam_im) / den
    bbr = fr[..., None] * b_re - fi[..., None] * b_im
    bbi = fr[..., None] * b_im + fi[..., None] * b_re
    k = jnp.arange(t + 1, dtype=F32)[:, None, None, None]
    pr = jnp.exp(k * ar[None]) * jnp.cos(k * ai[None])
    pi = jnp.exp(k * ar[None]) * jnp.sin(k * ai[None])
    pbr = pr[..., None] * bbr[None] - pi[..., None] * bbi[None]
    pbi = pr[..., None] * bbi[None] + pi[..., None] * bbr[None]
    pcr = c_re[None] * pr[:, :, :, None, :] - c_im[None] * pi[:, :, :, None, :]
    pci = c_re[None] * pi[:, :, :, None, :] + c_im[None] * pr[:, :, :, None, :]
    kern = (jnp.einsum("dgpn,kdgnq->kdgpq", c_re, pbr[:t], precision=hp)
            - jnp.einsum("dgpn,kdgnq->kdgpq", c_im, pbi[:t], precision=hp))
    import numpy as np

    jb, gpb = S5_JB, S5_GPB
    kw, sw4 = t * LANES, 4 * S5_SW

    def expand(compact, col_src, row_group, col_group):
        onehot = jnp.asarray(col_src[None, :] == np.arange(compact.shape[-1])[:, None], BF16)
        full = jnp.dot(compact.astype(BF16), onehot, preferred_element_type=BF16)
        return jnp.where(jnp.asarray(row_group[:, None] == col_group[None, :])[None], full, 0.0)

    state_rows = np.arange(sw4)
    chunk_rows = np.arange(kw)
    state_grp, chunk_grp = (state_rows // S5_N) % gpb, (chunk_rows // A_GROUP) % gpb
    state_src = (state_rows // S5_SW) * S5_N + state_rows % S5_N
    chunk_src = (chunk_rows // LANES) * A_GROUP + chunk_rows % A_GROUP

    sidx = jnp.arange(t)
    secs = jnp.stack([pbr[t - 1 - sidx, 0], pbi[t - 1 - sidx, 0], pbr[sidx, 1], pbi[sidx, 1]], axis=0)
    secs = secs.reshape(4, t, jb, gpb, S5_N, A_GROUP)
    bst = expand(jnp.transpose(secs, (2, 1, 3, 5, 0, 4)).reshape(jb, kw, 4 * S5_N), state_src, chunk_grp, state_grp)
    secs = jnp.stack([pcr[sidx + 1, 0], -pci[sidx + 1, 0], pcr[t - sidx, 1], -pci[t - sidx, 1]], axis=0)
    secs = secs.reshape(4, t, jb, gpb, A_GROUP, S5_N)
    cst = expand(jnp.transpose(secs, (2, 0, 3, 5, 1, 4)).reshape(jb, sw4, t * A_GROUP), chunk_src, state_grp, chunk_grp)
    noff = 2 * t - 1
    kd = jnp.concatenate([kern[1:, 1][::-1], (kern[0, 0] + kern[0, 1])[None], kern[1:, 0]], axis=0)
    kd = jnp.transpose(kd.reshape(noff, jb, gpb, A_GROUP, A_GROUP), (1, 0, 2, 4, 3))
    lane_rows = np.arange(noff * LANES)
    blocks = expand(kd.reshape(jb, noff * LANES, A_GROUP), np.arange(LANES) % A_GROUP,
                    (lane_rows // A_GROUP) % gpb, np.arange(LANES) // A_GROUP)
    off = np.arange(t)[None, :] - np.arange(t)[:, None] + t - 1
    wt = jnp.take(blocks.reshape(jb, noff, LANES, LANES), jnp.asarray(off), axis=1)
    wt = jnp.transpose(wt, (0, 1, 3, 2, 4)).reshape(jb, kw, kw)
    acoef = jnp.stack([pr[t, 0], pi[t, 0], pr[t, 1], pi[t, 1]], axis=0)
    acoef = acoef.reshape(4, jb, S5_SW).transpose(1, 0, 2).reshape(jb, 1, 4 * S5_SW)
    acoef = jnp.broadcast_to(acoef, (jb, S5_BG, 4 * S5_SW))
    return bst.astype(BF16), cst.astype(BF16), wt.astype(BF16), acoef


def _s5_in_body(u_ref, bst_ref, lhs_ref, s_ref, *, seq_len, row_tile):
    nc = seq_len // S5_T

    @pl.when(pl.program_id(2) == 0)
    def _():
        def build(cp, carry):
            base = cp * (2 * S5_T)
            for s in range(S5_T):
                a = u_ref[pl.ds(base + s, S5_BG, stride=seq_len), :]
                b = u_ref[pl.ds(base + S5_T + s, S5_BG, stride=seq_len), :]
                lhs_ref[pl.ds(pl.multiple_of(cp * 16, 16), 16), s * LANES:(s + 1) * LANES] = (
                    jnp.concatenate([a, b], axis=0).astype(BF16))
            return carry

        lax.fori_loop(0, nc // 2, build, 0)

    def mm(r, carry):
        rows = pl.ds(pl.multiple_of(r * row_tile, row_tile), row_tile)
        s_ref[rows, :] = _dot(lhs_ref[rows, :], bst_ref[...])
        return carry

    lax.fori_loop(0, (nc * S5_BG) // row_tile, mm, 0)


def _s5_state_in(proj2d, bst, *, nbg, seq_len):
    nc = seq_len // S5_T
    rows = nc * S5_BG
    kw = S5_T * LANES
    half = 2 * S5_SW
    row_tile = min(rows, 256)
    return pl.pallas_call(
        functools.partial(_s5_in_body, seq_len=seq_len, row_tile=row_tile),
        grid=(nbg, S5_JB, 2),
        in_specs=[
            pl.BlockSpec((S5_BG * seq_len, LANES), lambda g, j, n: (g, j)),
            pl.BlockSpec((None, kw, half), lambda g, j, n: (j, 0, n)),
        ],
        out_specs=[
            pl.BlockSpec((None, None, rows, kw), lambda g, j, n: (g, j, 0, 0)),
            pl.BlockSpec((None, None, rows, half), lambda g, j, n: (g, j, 0, n)),
        ],
        out_shape=[
            jax.ShapeDtypeStruct((nbg, S5_JB, rows, kw), BF16),
            jax.ShapeDtypeStruct((nbg, S5_JB, rows, 4 * S5_SW), F32),
        ],
        compiler_params=_cp(("parallel", "parallel", "arbitrary")),
        name="s5_state_in",
    )(proj2d, bst)


def _s5_scan_body(s_ref, a_ref, x0_ref, x_ref, fin_ref, *, nc):
    w = S5_SW
    afr, afi = a_ref[:, 0:w], a_ref[:, w:2 * w]
    abr, abi = a_ref[:, 2 * w:3 * w], a_ref[:, 3 * w:4 * w]

    def step(c, carry):
        xfr, xfi, xbr, xbi = carry
        rf = pl.ds(pl.multiple_of(c * S5_BG, S5_BG), S5_BG)
        rb = pl.ds(pl.multiple_of((nc - 1 - c) * S5_BG, S5_BG), S5_BG)
        x_ref[rf, 0:w] = xfr
        x_ref[rf, w:2 * w] = xfi
        x_ref[rb, 2 * w:3 * w] = xbr
        x_ref[rb, 3 * w:4 * w] = xbi
        nfr = afr * xfr - afi * xfi + s_ref[rf, 0:w]
        nfi = afr * xfi + afi * xfr + s_ref[rf, w:2 * w]
        nbr = abr * xbr - abi * xbi + s_ref[rb, 2 * w:3 * w]
        nbi = abr * xbi + abi * xbr + s_ref[rb, 3 * w:4 * w]
        return nfr, nfi, nbr, nbi

    init = (x0_ref[:, 0:w], x0_ref[:, w:2 * w], x0_ref[:, 2 * w:3 * w], x0_ref[:, 3 * w:4 * w])
    xfr, xfi, xbr, xbi = lax.fori_loop(0, nc, step, init)
    fin_ref[:, 0:w] = xfr
    fin_ref[:, w:2 * w] = xfi
    fin_ref[:, 2 * w:3 * w] = xbr
    fin_ref[:, 3 * w:4 * w] = xbi


def _s5_scan(s_all, acoef, x0, *, nbg, seq_len):
    nc = seq_len // S5_T
    rows = nc * S5_BG
    sw4 = 4 * S5_SW
    return pl.pallas_call(
        functools.partial(_s5_scan_body, nc=nc),
        grid=(nbg, S5_JB),
        in_specs=[
            pl.BlockSpec((None, None, rows, sw4), lambda g, j: (g, j, 0, 0)),
            pl.BlockSpec((None, S5_BG, sw4), lambda g, j: (j, 0, 0)),
            pl.BlockSpec((None, None, S5_BG, sw4), lambda g, j: (g, j, 0, 0)),
        ],
        out_specs=[
            pl.BlockSpec((None, None, rows, sw4), lambda g, j: (g, j, 0, 0)),
            pl.BlockSpec((None, None, S5_BG, sw4), lambda g, j: (g, j, 0, 0)),
        ],
        out_shape=[
            jax.ShapeDtypeStruct((nbg, S5_JB, rows, sw4), F32),
            jax.ShapeDtypeStruct((nbg, S5_JB, S5_BG, sw4), F32),
        ],
        compiler_params=_cp(("parallel", "parallel")),
        name="s5_scan",
    )(s_all, acoef, x0)


def _s5_out_body(lhs_ref, x_ref, wt_ref, cst_ref, y_ref, acc_ref, *, chunks):
    acc_ref[...] = _dot(lhs_ref[...], wt_ref[...]) + _dot(x_ref[...].astype(BF16), cst_ref[...])

    def unchunk(c, carry):
        rows = pl.ds(pl.multiple_of(c * S5_BG, S5_BG), S5_BG)
        for t in range(S5_T):
            y_ref[:, c * S5_T + t, :] = acc_ref[rows, t * LANES:(t + 1) * LANES]
        return carry

    lax.fori_loop(0, chunks, unchunk, 0)


def _s5_output(lhs, x_all, wt, cst, *, nbg, seq_len):
    nc = seq_len // S5_T
    chunks = min(nc, 32)
    tr = chunks * S5_BG
    kw = S5_T * LANES
    sw4 = 4 * S5_SW
    return pl.pallas_call(
        functools.partial(_s5_out_body, chunks=chunks),
        grid=(nbg, S5_JB, nc // chunks),
        in_specs=[
            pl.BlockSpec((None, None, tr, kw), lambda g, j, r: (g, j, r, 0)),
            pl.BlockSpec((None, None, tr, sw4), lambda g, j, r: (g, j, r, 0)),
            pl.BlockSpec((None, kw, kw), lambda g, j, r: (j, 0, 0)),
            pl.BlockSpec((None, sw4, kw), lambda g, j, r: (j, 0, 0)),
        ],
        out_specs=pl.BlockSpec((S5_BG, chunks * S5_T, LANES), lambda g, j, r: (g, r, j)),
        out_shape=jax.ShapeDtypeStruct((nbg * S5_BG, seq_len, MIX_A), F32),
        scratch_shapes=[pltpu.VMEM((tr, kw), F32)],
        compiler_params=_cp(("parallel", "parallel", "arbitrary")),
        name="s5_output",
    )(lhs, x_all, wt, cst)


def _s5_mix(proj2d, ops, x0, *, nbg, seq_len):
    bst, cst, wt, acoef = ops
    lhs, s_all = _s5_state_in(proj2d, bst, nbg=nbg, seq_len=seq_len)
    x_all, fin = _s5_scan(s_all, acoef, x0, nbg=nbg, seq_len=seq_len)
    y = _s5_output(lhs, x_all, wt, cst, nbg=nbg, seq_len=seq_len)
    return y, fin


def _s5_pack_state(re, im):
    b = re.shape[0]
    x = jnp.stack([re[:, 0], im[:, 0], re[:, 1], im[:, 1]], axis=1)
    x = x.reshape(b // S5_BG, S5_BG, 4, S5_JB, S5_SW)
    return jnp.transpose(x, (0, 3, 1, 2, 4)).reshape(b // S5_BG, S5_JB, S5_BG, 4 * S5_SW)


def _s5_unpack_state(fin):
    nbg = fin.shape[0]
    x = fin.reshape(nbg, S5_JB, S5_BG, 4, S5_SW)
    x = jnp.transpose(x, (0, 2, 3, 1, 4)).reshape(nbg * S5_BG, 4, A_GROUPS, S5_N)
    return jnp.stack([x[:, 0], x[:, 2]], axis=1), jnp.stack([x[:, 1], x[:, 3]], axis=1)


Q_BLK, K_BLK, V_BLK = MIX_A // LANES, (MIX_A + MIX_B) // LANES, (MIX_A + 2 * MIX_B) // LANES


def _pair_softmax_pv(q, parts):
    lane = lax.broadcasted_iota(jnp.int32, (1, LANES), 1)
    outs = []
    for half in range(2):
        sel = (lane < NA_HD) if half == 0 else (lane >= NA_HD)
        qm = jnp.where(sel, q, 0.0).astype(BF16)
        scores = []
        for k, _, bias in parts:
            s = _dot_nt(qm, k)
            if bias is not None:
                s = s + bias[half]
            scores.append(s)
        m = scores[0].max(axis=-1, keepdims=True)
        for s in scores[1:]:
            m = jnp.maximum(m, s.max(axis=-1, keepdims=True))
        l = None
        o = None
        for s, (_, v, _) in zip(scores, parts):
            p = jnp.exp(s - m)
            ls = p.sum(axis=-1, keepdims=True)
            os_ = _dot(p.astype(BF16), v)
            l = ls if l is None else l + ls
            o = os_ if o is None else o + os_
        outs.append(o / l)
    return jnp.where(lane < NA_HD, outs[0], outs[1]).astype(BF16)


def _ctx_attn_body(q_ref, k_ref, v_ref, o_ref):
    q = q_ref[...] * (NA_HD ** -0.5)
    o_ref[...] = _pair_softmax_pv(q, [(k_ref[...].astype(BF16), v_ref[...].astype(BF16), None)])


def _context_attention(proj, *, nb, seq_len):
    npair = NA_HEADS // 2
    return pl.pallas_call(
        _ctx_attn_body,
        grid=(nb, npair),
        in_specs=[
            pl.BlockSpec((seq_len, LANES), lambda b, h: (b, Q_BLK + h)),
            pl.BlockSpec((seq_len, LANES), lambda b, h: (b, K_BLK + h)),
            pl.BlockSpec((seq_len, LANES), lambda b, h: (b, V_BLK + h)),
        ],
        out_specs=pl.BlockSpec((seq_len, LANES), lambda b, h: (b, h)),
        out_shape=jax.ShapeDtypeStruct((nb * seq_len, MIX_B), BF16),
        compiler_params=_cp(("parallel", "parallel")),
        name="context_attention",
    )(proj, proj, proj)


NA_QROWS = 4
NA_WROWS = NA_QROWS + NA_KH


def _na_window_start(g, rows):
    rs = jnp.clip(g * NA_QROWS - NA_KH // 2, 0, rows - NA_KH)
    return jnp.minimum(rs, rows - NA_WROWS)


def _na_bias_tables(rpb, rows):
    import numpy as np

    ngroups = rows // NA_QROWS
    shp = (NA_QROWS * GRID_W, NA_WROWS * GRID_W)
    ndc = 2 * NA_KW - 1
    c, kc = np.meshgrid(np.arange(GRID_W), np.arange(GRID_W), indexing="ij")
    cs = np.clip(c - NA_KW // 2, 0, GRID_W - NA_KW)
    col_ok = (kc >= cs) & (kc < cs + NA_KW)
    col_sel = ((kc - c + NA_KW - 1)[None] == np.arange(ndc)[:, None, None]) & col_ok[None]
    dr_idx, valid = [], []
    for g in (0, 1, ngroups - 1):
        rs0 = int(np.clip(g * NA_QROWS - NA_KH // 2, 0, rows - NA_KH))
        ws = min(rs0, rows - NA_WROWS)
        ri, wi = np.meshgrid(np.arange(NA_QROWS), np.arange(NA_WROWS), indexing="ij")
        r = g * NA_QROWS + ri
        rs = np.clip(r - NA_KH // 2, 0, rows - NA_KH)
        row = ws + wi
        row_ok = (row >= rs) & (row < rs + NA_KH)
        dr_idx.append(np.clip(row - r + NA_KH - 1, 0, 2 * NA_KH - 2))
        valid.append((row_ok[:, None, :, None] & col_ok[None, :, None, :]).reshape(shp))
    rows_sel = rpb[:, np.stack(dr_idx)]
    t = jnp.einsum("hgrwd,dck->hgrcwk", rows_sel, jnp.asarray(col_sel, F32), precision=lax.Precision.HIGHEST)
    t = jnp.where(jnp.asarray(np.stack(valid))[None], t.reshape(NA_HEADS, 3, *shp), NEG_BIG)
    t = t.reshape(NA_HEADS // 2, 2, 3, *shp)
    return jnp.transpose(t, (0, 2, 1, 3, 4))


NA_GPS = 2


def _nbr_attn_body(q_ref, k_ref, v_ref, ck_ref, cv_ref, bias_ref, o_ref, *, rows):
    ngroups = rows // NA_QROWS
    qt = NA_QROWS * GRID_W
    ck, cv = ck_ref[...].astype(BF16), cv_ref[...].astype(BF16)
    for u in range(NA_GPS):
        g = pl.program_id(2) * NA_GPS + u
        case = jnp.where(g == 0, 0, jnp.where(g == ngroups - 1, 2, 1))
        ws = _na_window_start(g, rows)
        win = pl.ds(pl.multiple_of(ws * GRID_W, GRID_W), NA_WROWS * GRID_W)
        qrows = slice(u * qt, (u + 1) * qt)
        q = q_ref[qrows, :] * (NA_HD ** -0.5)
        kw = k_ref[win, :].astype(BF16)
        vw = v_ref[win, :].astype(BF16)
        bias = (bias_ref[case, 0], bias_ref[case, 1])
        o_ref[qrows, :] = _pair_softmax_pv(q, [(kw, vw, bias), (ck, cv, None)])


def _neighbourhood_attention(proj, ck, cv, bias, *, nb, seq_len):
    rows = seq_len // GRID_W
    ngroups = rows // NA_QROWS // NA_GPS
    npair = NA_HEADS // 2
    qt = NA_GPS * NA_QROWS * GRID_W
    past = ck.shape[1]
    return pl.pallas_call(
        functools.partial(_nbr_attn_body, rows=rows),
        grid=(npair, nb, ngroups),
        in_specs=[
            pl.BlockSpec((qt, LANES), lambda h, b, g: (b * ngroups + g, Q_BLK + h)),
            pl.BlockSpec((seq_len, LANES), lambda h, b, g: (b, K_BLK + h)),
            pl.BlockSpec((seq_len, LANES), lambda h, b, g: (b, V_BLK + h)),
            pl.BlockSpec((None, past, LANES), lambda h, b, g: (b, 0, h)),
            pl.BlockSpec((None, past, LANES), lambda h, b, g: (b, 0, h)),
            pl.BlockSpec((None, 3, 2, NA_QROWS * GRID_W, NA_WROWS * GRID_W), lambda h, b, g: (h, 0, 0, 0, 0)),
        ],
        out_specs=pl.BlockSpec((qt, LANES), lambda h, b, g: (b * ngroups + g, h)),
        out_shape=jax.ShapeDtypeStruct((nb * seq_len, MIX_B), BF16),
        compiler_params=_cp(("parallel", "parallel", "parallel")),
        name="neighbourhood_attention",
    )(proj, proj, proj, ck, cv, bias)


def _even_out_body(x_ref, y_ref, u_ref, bo_ref, d_ref, wg_ref, bg_ref, wo_ref, gt_ref, o_ref):
    z = _gelu_tanh(y_ref[...] + d_ref[...] * u_ref[...])
    a = z * _sigmoid(_dot(z.astype(BF16), wg_ref[...]) + bg_ref[...])
    cat = jnp.concatenate([a.astype(BF16), bo_ref[...]], axis=-1)
    o_ref[...] = x_ref[...] + gt_ref[...] * _dot(cat, wo_ref[...])


def _even_out(x, y, proj, bo, d_row, wg_bf, bg_row, wo_bf, modt_l, seq_fn, *, tm):
    t = x.shape[0]
    return pl.pallas_call(
        _even_out_body,
        grid=(t // tm,),
        in_specs=[
            pl.BlockSpec((tm, D_MODEL), lambda i: (i, 0)),
            pl.BlockSpec((tm, MIX_A), lambda i: (i, 0)),
            pl.BlockSpec((tm, MIX_A), lambda i: (i, 0)),
            pl.BlockSpec((tm, MIX_B), lambda i: (i, 0)),
            pl.BlockSpec((1, MIX_A), lambda i: (0, 0)),
            pl.BlockSpec((MIX_A, MIX_A), lambda i: (0, 0)),
            pl.BlockSpec((1, MIX_A), lambda i: (0, 0)),
            pl.BlockSpec((D_MODEL, D_MODEL), lambda i: (0, 0)),
            _mod_spec(2, seq_fn, 1),
        ],
        out_specs=pl.BlockSpec((tm, D_MODEL), lambda i: (i, 0)),
        out_shape=jax.ShapeDtypeStruct((t, D_MODEL), F32),
        compiler_params=_cp(("parallel",)),
        name="even_out",
    )(x, y, proj, bo, d_row, wg_bf, bg_row, wo_bf, modt_l)


DN_QKV = 3 * D_MODEL
DN_PROJ_PAD = 4 * D_MODEL + LANES
DN_GATE_BLK = (4 * D_MODEL) // LANES
DN_CB = 512
DN_HALO = SUBLANES


def _gdn_pre_body(x_ref, w_ref, o_ref, pad_ref, *, seq_len, rt):
    kind = pl.program_id(1) // (D_MODEL // DN_CB)
    zeros = jnp.zeros((DN_HALO, DN_CB), F32)
    pad_ref[0:DN_HALO, :] = zeros
    pad_ref[DN_HALO + seq_len:DN_HALO + seq_len + DN_HALO, :] = zeros
    pad_ref[DN_HALO:DN_HALO + seq_len, :] = x_ref[...].astype(F32)
    qscale = jnp.where(kind == 0, DN_DK ** -0.5, 1.0)
    lo = DN_HALO - (DN_CONV - 1) // 2

    def tile(r, carry):
        base = pl.multiple_of(r * rt, rt)
        xt = pad_ref[pl.ds(base, rt + 2 * DN_HALO), :]
        y = None
        for k in range(DN_CONV):
            term = w_ref[k:k + 1, :] * xt[lo + k:lo + k + rt, :]
            y = term if y is None else y + term
        y = _silu(y)

        @pl.when(kind == 2)
        def _():
            o_ref[pl.ds(base, rt), :] = y.astype(BF16)

        @pl.when(kind != 2)
        def _():
            segs = []
            for h in range(DN_CB // DN_DK):
                seg = y[:, h * DN_DK:(h + 1) * DN_DK]
                segs.append(seg * lax.rsqrt(jnp.sum(seg * seg, axis=-1, keepdims=True) + EPS) * qscale)
            o_ref[pl.ds(base, rt), :] = jnp.concatenate(segs, axis=-1).astype(BF16)

        return carry

    lax.fori_loop(0, seq_len // rt, tile, 0)


def _gdn_pre(proj, conv_w, *, nb, seq_len):
    rt = 256
    return pl.pallas_call(
        functools.partial(_gdn_pre_body, seq_len=seq_len, rt=rt),
        grid=(nb, DN_QKV // DN_CB),
        in_specs=[
            pl.BlockSpec((seq_len, DN_CB), lambda b, c: (b, c)),
            pl.BlockSpec((DN_CONV, DN_CB), lambda b, c: (0, c)),
        ],
        out_specs=pl.BlockSpec((seq_len, DN_CB), lambda b, c: (b, c)),
        out_shape=jax.ShapeDtypeStruct((nb * seq_len, DN_QKV), BF16),
        scratch_shapes=[pltpu.VMEM((seq_len + 2 * DN_HALO, DN_CB), F32)],
        compiler_params=_cp(("parallel", "parallel")),
        name="gdn_pre",
    )(proj, conv_w)


def _split_bf16(x, n):
    parts = []
    r = x
    for _ in range(n):
        p = r.astype(BF16)
        parts.append(p)
        r = r - p.astype(F32)
    return parts


def _mm_exact_lhs(a_bf, b):
    return sum(_dot(a_bf, p) for p in _split_bf16(b, 3))


def _softplus(x):
    return jnp.maximum(x, 0.0) + jnp.log(1.0 + jnp.exp(-jnp.abs(x)))


def _gdn_body(q_ref, k_ref, v_ref, gt_ref, ga_ref, gb_ref, s0_ref, o_ref, sl_ref,
              s_scr, vn_scr, a_scr, t_scr, m_scr, attn_scr, rhs_scr, u_scr, w_scr, qg_scr, kg_scr, *, direction):
    t = pl.program_id(1)
    nt = pl.num_programs(1)
    tile = GDN_TILE
    c_len = DN_CHUNK
    nchunk = tile // c_len
    heads = range(DN_HEADS)

    @pl.when(t == 0)
    def _():
        s_scr[...] = s0_ref[...]
        vn_scr[...] = jnp.zeros_like(vn_scr)

    ri = lax.broadcasted_iota(jnp.int32, (tile, tile), 0)
    ci = lax.broadcasted_iota(jnp.int32, (tile, tile), 1)
    same = (ri // c_len) == (ci // c_len)
    if direction == 0:
        incl = same & (ci <= ri)
        strict = same & (ci < ri)
    else:
        incl = same & (ci >= ri)
        strict = same & (ci > ri)
    tri_bf = jnp.where(incl, 1.0, 0.0).astype(BF16)
    blk_bf = jnp.where(same, 1.0, 0.0).astype(BF16)
    nlev = int(math.log2(c_len))

    def off_mask(lvl, rs=slice(0, tile)):
        nrow = rs.stop - rs.start
        r = lax.broadcasted_iota(jnp.int32, (nrow, tile), 0) + rs.start
        c = lax.broadcasted_iota(jnp.int32, (nrow, tile), 1)
        late, early = (r, c) if direction == 0 else (c, r)
        return (((late >> lvl) & 1) == 1) & (((early >> lvl) & 1) == 0) & ((r >> (lvl + 1)) == (c >> (lvl + 1)))

    def active_blocks(lvl):
        hb = 1 << lvl
        return [slice(rb * hb, (rb + 1) * hb) for rb in range(tile // hb) if rb % 2 == 1 - direction]

    gt = gt_ref[...].astype(F32)
    g_all = ga_ref[...] * _softplus(gt + gb_ref[...])
    beta_all = _sigmoid(gt)
    gc_all = _mm_exact_lhs(tri_bf, g_all)
    gl_all = _mm_exact_lhs(blk_bf, g_all)
    gc_t = jnp.transpose(gc_all)
    chunk_order = range(nchunk) if direction == 0 else range(nchunk - 1, -1, -1)
    eye = jnp.where(ri == ci, 1.0, 0.0)

    def lane(h):
        return direction * DN_HEADS + h

    def hsl(h):
        return slice(h * DN_DK, (h + 1) * DN_DK)

    for h in heads:
        gcol = gc_all[:, lane(h):lane(h) + 1]
        grow = gc_t[lane(h):lane(h) + 1, :]
        glcol = gl_all[:, lane(h):lane(h) + 1]
        bcol = beta_all[:, 2 * DN_HEADS + lane(h):2 * DN_HEADS + lane(h) + 1]
        q_bf, k_bf = q_ref[:, hsl(h)], k_ref[:, hsl(h)]
        qh, kh, vh = q_bf.astype(F32), k_bf.astype(F32), v_ref[:, hsl(h)].astype(F32)
        decay = jnp.exp(jnp.where(incl, gcol - grow, NEG_BIG))
        a_mat = jnp.where(strict, bcol * _dot_nt(k_bf, k_bf) * decay, 0.0)
        a_scr[h] = a_mat
        t_scr[h] = (eye - jnp.where(off_mask(0), a_mat, 0.0)).astype(BF16)
        attn_scr[h] = (_dot_nt(q_bf, k_bf) * decay).astype(BF16)
        egc = jnp.exp(gcol)
        rhs_scr[h] = jnp.concatenate([vh * bcol, kh * (bcol * egc)], axis=-1).astype(BF16)
        qg_scr[h] = (qh * egc).astype(BF16)
        kg_scr[h] = (kh * jnp.exp(glcol - gcol)).astype(BF16)

    for lvl in range(1, nlev):
        hb = 1 << lvl
        if hb % 16:
            for h in heads:
                m_scr[h] = _dot(jnp.where(off_mask(lvl), a_scr[h], 0.0).astype(BF16), t_scr[h]).astype(BF16)
            for h in heads:
                t_scr[h] = t_scr[h] - _dot(t_scr[h], m_scr[h]).astype(BF16)
            continue
        blocks = active_blocks(lvl)
        for h in heads:
            lhs = jnp.concatenate([jnp.where(off_mask(lvl, rs), a_scr[h, rs, :], 0.0) for rs in blocks], axis=0)
            m_act = _dot(lhs.astype(BF16), t_scr[h])
            zero = jnp.zeros((hb, tile), F32)
            pieces = [zero] * (tile // hb)
            for i, rs in enumerate(blocks):
                pieces[rs.start // hb] = m_act[i * hb:(i + 1) * hb, :]
            m_scr[h] = jnp.concatenate(pieces, axis=0).astype(BF16)
        for h in heads:
            t_act = jnp.concatenate([t_scr[h, rs, :].astype(F32) for rs in blocks], axis=0)
            upd = _dot(t_act.astype(BF16), m_scr[h])
            for i, rs in enumerate(blocks):
                t_scr[h, rs, :] = t_scr[h, rs, :] - upd[i * hb:(i + 1) * hb, :].astype(BF16)

    for h in heads:
        x = _dot(t_scr[h], rhs_scr[h])
        u_scr[h] = x[:, :DN_DV]
        w_scr[h] = x[:, DN_DV:].astype(BF16)

    for c in chunk_order:
        rows = slice(c * c_len, (c + 1) * c_len)
        for h in heads:
            s = s_scr[h]
            s_bf = s.astype(BF16)
            v_new = u_scr[h, rows, :] - _dot(w_scr[h, rows, :], s_bf)
            vn_scr[h, rows, :] = v_new.astype(BF16)
            o_ref[rows, hsl(h)] = (_dot(qg_scr[h, rows, :], s_bf)
                                   + _dot(attn_scr[h, rows, :], vn_scr[h])).astype(o_ref.dtype)
            gl_c = gl_all[c * c_len:c * c_len + 1, lane(h):lane(h) + 1]
            s_scr[h] = s * jnp.exp(gl_c) + _dot_tn(kg_scr[h, rows, :], v_new.astype(BF16))

    @pl.when(t == nt - 1)
    def _():
        sl_ref[...] = s_scr[...]


def _gdn_scan(qkv, proj, ga_row, gb_row, s0, *, nb, seq_len, direction):
    nt = seq_len // GDN_TILE
    tile = GDN_TILE
    nq = D_MODEL // D_MODEL

    def row(b, t):
        return b * nt + (t if direction == 0 else nt - 1 - t)

    return pl.pallas_call(
        functools.partial(_gdn_body, direction=direction),
        grid=(nb, nt),
        in_specs=[
            pl.BlockSpec((tile, D_MODEL), lambda b, t: (row(b, t), 0)),
            pl.BlockSpec((tile, D_MODEL), lambda b, t: (row(b, t), nq)),
            pl.BlockSpec((tile, D_MODEL), lambda b, t: (row(b, t), 2 * nq)),
            pl.BlockSpec((tile, LANES), lambda b, t: (row(b, t), DN_GATE_BLK)),
            pl.BlockSpec((1, LANES), lambda b, t: (0, 0)),
            pl.BlockSpec((1, LANES), lambda b, t: (0, 0)),
            pl.BlockSpec((None, DN_HEADS, DN_DK, DN_DV), lambda b, t: (b, 0, 0, 0)),
        ],
        out_specs=[
            pl.BlockSpec((tile, D_MODEL), lambda b, t: (row(b, t), 0)),
            pl.BlockSpec((None, DN_HEADS, DN_DK, DN_DV), lambda b, t: (b, 0, 0, 0)),
        ],
        out_shape=[
            jax.ShapeDtypeStruct((nb * seq_len, D_MODEL), BF16),
            jax.ShapeDtypeStruct((nb, DN_HEADS, DN_DK, DN_DV), F32),
        ],
        scratch_shapes=[
            pltpu.VMEM((DN_HEADS, DN_DK, DN_DV), F32),
            pltpu.VMEM((DN_HEADS, tile, DN_DV), BF16),
            pltpu.VMEM((DN_HEADS, tile, tile), F32),
            pltpu.VMEM((DN_HEADS, tile, tile), BF16),
            pltpu.VMEM((DN_HEADS, tile, tile), BF16),
            pltpu.VMEM((DN_HEADS, tile, tile), BF16),
            pltpu.VMEM((DN_HEADS, tile, 2 * DN_DV), BF16),
            pltpu.VMEM((DN_HEADS, tile, DN_DV), F32),
            pltpu.VMEM((DN_HEADS, tile, DN_DK), BF16),
            pltpu.VMEM((DN_HEADS, tile, DN_DK), BF16),
            pltpu.VMEM((DN_HEADS, tile, DN_DK), BF16),
        ],
        compiler_params=_cp(("parallel", "arbitrary")),
        name="gdn_scan_fwd" if direction == 0 else "gdn_scan_bwd",
    )(qkv, qkv, qkv, proj, ga_row, gb_row, s0)


def _odd_out_body(x_ref, of_ref, ob_ref, z_ref, ng_ref, wo_ref, gt_ref, o_ref):
    o = of_ref[...].astype(F32) + ob_ref[...].astype(F32)
    segs = []
    for h in range(DN_HEADS):
        seg = o[:, h * DN_DV:(h + 1) * DN_DV]
        segs.append(seg * lax.rsqrt(jnp.mean(seg * seg, axis=-1, keepdims=True) + EPS))
    y = jnp.concatenate(segs, axis=-1) * ng_ref[...] * _silu(z_ref[...].astype(F32))
    o_ref[...] = x_ref[...] + gt_ref[...] * _dot(y.astype(BF16), wo_ref[...])


def _odd_out(x, o_f, o_b, proj, ng_row, wo_bf, modt_l, seq_fn, *, tm):
    t = x.shape[0]
    zblk = DN_QKV // D_MODEL
    return pl.pallas_call(
        _odd_out_body,
        grid=(t // tm,),
        in_specs=[
            pl.BlockSpec((tm, D_MODEL), lambda i: (i, 0)),
            pl.BlockSpec((tm, D_MODEL), lambda i: (i, 0)),
            pl.BlockSpec((tm, D_MODEL), lambda i: (i, 0)),
            pl.BlockSpec((tm, D_MODEL), lambda i: (i, zblk)),
            pl.BlockSpec((1, D_MODEL), lambda i: (0, 0)),
            pl.BlockSpec((D_MODEL, D_MODEL), lambda i: (0, 0)),
            _mod_spec(2, seq_fn, 1),
        ],
        out_specs=pl.BlockSpec((tm, D_MODEL), lambda i: (i, 0)),
        out_shape=jax.ShapeDtypeStruct((t, D_MODEL), F32),
        compiler_params=_cp(("parallel",)),
        name="odd_out",
    )(x, o_f, o_b, proj, ng_row, wo_bf, modt_l)


ROW_TILE = 512


def _seq_fn(slot0, seq_len, tm):
    if slot0 == 0:
        return lambda i: 0
    return lambda i: slot0 + (i * tm) // seq_len


def kernel(x_prompt, x_sample, c, cache_na_k, cache_na_v, state_s5_re, state_s5_im, state_dn, c_ctx, norm_mix_g, norm_ff_g, w_mod, b_mod, w_ff1, w_ff2, w_in_e, w_out_e, s5_lam_re, s5_lam_im, s5_log_dt, s5_b_re, s5_b_im, s5_c_re, s5_c_im, s5_d, s5_w_glu, s5_b_glu, na_rpb, w_in_o, dn_conv_w, dn_a_log, dn_dt_bias, dn_norm_g, w_out_o, final_norm_g):
    nbp, lp = x_prompt.shape[:2]
    nbs, ls = x_sample.shape[:2]
    assert nbp % S5_BG == 0 and nbs % S5_BG == 0 and 1 + nbs <= 16
    streams = [
        dict(x=x_prompt.reshape(nbp * lp, D_MODEL), nb=nbp, L=lp, slot0=0),
        dict(x=x_sample.reshape(nbs * ls, D_MODEL), nb=nbs, L=ls, slot0=1),
    ]
    cond16 = jnp.zeros((16, D_MODEL), F32).at[0].set(c_ctx).at[1:1 + nbs].set(c)
    mod = _modulation(cond16, w_mod, b_mod[:, None, :])
    modt = jnp.transpose(mod.reshape(DEPTH, 16, 6, D_MODEL), (0, 2, 1, 3))[:, :, :, None, :]

    new_k, new_v, new_sre, new_sim, new_dn = [], [], [], [], []
    for l in range(DEPTH):
        e = l // 2
        if l % 2 == 0:
            w_in = w_in_e[e].astype(BF16)
            w_out = w_out_e[e].astype(BF16)
            w_glu = s5_w_glu[e].astype(BF16)
            ops = _s5_operators(s5_lam_re[e], s5_lam_im[e], s5_log_dt[e], s5_b_re[e], s5_b_im[e],
                                s5_c_re[e], s5_c_im[e])
            bias = _na_bias_tables(na_rpb[e], ls // GRID_W)
        else:
            w_in = jnp.pad(w_in_o[e], ((0, 0), (0, DN_PROJ_PAD - w_in_o.shape[-1]))).astype(BF16)
            w_out = w_out_o[e].astype(BF16)
            ga_row = jnp.zeros((1, LANES), F32).at[0, :2 * DN_HEADS].set(-jnp.exp(dn_a_log[e]).reshape(-1))
            gb_row = jnp.zeros((1, LANES), F32).at[0, :2 * DN_HEADS].set(dn_dt_bias[e].reshape(-1))
            ng_row = jnp.tile(dn_norm_g[e], DN_HEADS)[None]
        w1 = w_ff1[l].astype(BF16)
        w2 = w_ff2[l].astype(BF16)
        for st in streams:
            x, nb, sl = st["x"], st["nb"], st["L"]
            seq = _seq_fn(st["slot0"], sl, ROW_TILE)
            is_ctx = st["slot0"] == 0
            if l % 2 == 0:
                proj = _norm_mod_matmul(x, norm_mix_g[l][None], modt[l], w_in, seq, tm=ROW_TILE, name="in_proj_even")
                nbg = nb // S5_BG
                if is_ctx:
                    x0 = jnp.zeros((nbg, S5_JB, S5_BG, 4 * S5_SW), F32)
                else:
                    x0 = _s5_pack_state(state_s5_re[:, e], state_s5_im[:, e])
                y, fin = _s5_mix(proj, ops, x0, nbg=nbg, seq_len=sl)
                if is_ctx:
                    bo = _context_attention(proj, nb=nb, seq_len=sl)
                    kv = proj.reshape(nb, sl, -1)
                    new_k.append(kv[:, :, MIX_A + MIX_B:MIX_A + 2 * MIX_B].reshape(nb, sl, NA_HEADS, NA_HD))
                    new_v.append(kv[:, :, MIX_A + 2 * MIX_B:].reshape(nb, sl, NA_HEADS, NA_HD))
                    sre, sim = _s5_unpack_state(fin)
                    new_sre.append(sre)
                    new_sim.append(sim)
                else:
                    past = cache_na_k.shape[2]
                    ck = cache_na_k[:, e].reshape(nb, past, MIX_B)
                    cv = cache_na_v[:, e].reshape(nb, past, MIX_B)
                    bo = _neighbourhood_attention(proj, ck, cv, bias, nb=nb, seq_len=sl)
                x = _even_out(x, y.reshape(nb * sl, MIX_A), proj, bo, s5_d[e][None], w_glu, s5_b_glu[e][None],
                              w_out, modt[l], seq, tm=ROW_TILE)
            else:
                proj = _norm_mod_matmul(x, norm_mix_g[l][None], modt[l], w_in, seq, tm=ROW_TILE, name="in_proj_odd",
                                        out_dtype=BF16)
                qkv = _gdn_pre(proj, dn_conv_w[e], nb=nb, seq_len=sl)
                if is_ctx:
                    s0 = jnp.zeros((nb, 2, DN_HEADS, DN_DK, DN_DV), F32)
                else:
                    s0 = state_dn[:, e]
                o_f, s_f = _gdn_scan(qkv, proj, ga_row, gb_row, s0[:, 0], nb=nb, seq_len=sl, direction=0)
                o_b, s_b = _gdn_scan(qkv, proj, ga_row, gb_row, s0[:, 1], nb=nb, seq_len=sl, direction=1)
                if is_ctx:
                    new_dn.append(jnp.stack([s_f, s_b], axis=1))
                x = _odd_out(x, o_f, o_b, proj, ng_row, w_out, modt[l], seq, tm=ROW_TILE)
            st["x"] = _mlp(x, norm_ff_g[l][None], modt[l], w1, w2, seq, final_norm_g[None], tm=ROW_TILE, tf=1024,
                           final_norm=(l == DEPTH - 1))

    y_prompt = streams[0]["x"].reshape(nbp, lp, D_MODEL)
    y_sample = streams[1]["x"].reshape(nbs, ls, D_MODEL)
    return (y_prompt, y_sample, jnp.stack(new_k, axis=1), jnp.stack(new_v, axis=1),
            jnp.stack(new_sre, axis=1), jnp.stack(new_sim, axis=1), jnp.stack(new_dn, axis=1))
```

```python
import functools
import math

import jax
import jax.numpy as jnp
from jax import lax
from jax.experimental import pallas as pl
from jax.experimental.pallas import tpu as pltpu

F32 = jnp.float32
BF16 = jnp.bfloat16

D_MODEL = 1024
DEPTH = 2
GRID_W = 64
MIX_A = D_MODEL // 2
A_GROUP = 16
A_GROUPS = MIX_A // A_GROUP
S5_N = 64
MIX_B = D_MODEL - MIX_A
NA_HEADS = 8
NA_HD = MIX_B // NA_HEADS
NA_KH = 8
NA_KW = 16
DN_HEADS = 8
DN_DK = D_MODEL // DN_HEADS
DN_DV = D_MODEL // DN_HEADS
DN_CONV = 5
DN_CHUNK = 64
D_FF = 4 * D_MODEL
EPS = 1e-6

LANES = 128
SUBLANES = 8
VMEM_LIMIT = 56 * 1024 * 1024
NEG_BIG = -1e30

S5_T = 16
S5_BG = SUBLANES
S5_JB = MIX_A // LANES
S5_GPB = LANES // A_GROUP
S5_SW = S5_GPB * S5_N
GDN_TILE = 256


def _cp(sem, vmem=VMEM_LIMIT):
    return pltpu.CompilerParams(dimension_semantics=sem, vmem_limit_bytes=vmem)


def _rms(x, g):
    return x * lax.rsqrt(jnp.mean(x * x, axis=-1, keepdims=True) + EPS) * g


def _sigmoid(x):
    return 1.0 / (1.0 + jnp.exp(-x))


def _silu(x):
    return x * _sigmoid(x)


def _gelu_tanh(x):
    return 0.5 * x * (1.0 + jnp.tanh(math.sqrt(2.0 / math.pi) * (x + 0.044715 * (x * x * x))))


def _dot(a, b):
    return jnp.dot(a, b, preferred_element_type=F32)


def _dot_nt(a, b):
    return lax.dot_general(a, b, (((1,), (1,)), ((), ())), preferred_element_type=F32)


def _dot_tn(a, b):
    return lax.dot_general(a, b, (((0,), (0,)), ((), ())), preferred_element_type=F32)


def _mod_body(c_ref, w_ref, b_ref, o_ref):
    c = c_ref[...]
    o_ref[...] = _dot(_silu(c).astype(BF16), w_ref[...].astype(BF16)) + b_ref[...]


def _modulation(cond16, w_mod_bf, b_mod):
    tn = 1536
    n = 6 * D_MODEL
    return pl.pallas_call(
        _mod_body,
        grid=(DEPTH, n // tn),
        in_specs=[
            pl.BlockSpec((16, D_MODEL), lambda l, j: (0, 0)),
            pl.BlockSpec((None, D_MODEL, tn), lambda l, j: (l, 0, j)),
            pl.BlockSpec((None, 1, tn), lambda l, j: (l, 0, j)),
        ],
        out_specs=pl.BlockSpec((None, 16, tn), lambda l, j: (l, 0, j)),
        out_shape=jax.ShapeDtypeStruct((DEPTH, 16, n), F32),
        compiler_params=_cp(("parallel", "parallel")),
        name="modulation",
    )(cond16, w_mod_bf, b_mod)


def _mod_spec(which, seq_fn, ngrid):
    if ngrid == 1:
        return pl.BlockSpec((None, None, 1, D_MODEL), lambda i: (which, seq_fn(i), 0, 0))
    return pl.BlockSpec((None, None, 1, D_MODEL), lambda i, j: (which, seq_fn(i), 0, 0))


def _nmm_body(x_ref, g_ref, sh_ref, sc_ref, w_ref, o_ref):
    h = _rms(x_ref[...], g_ref[...]) * (1.0 + sc_ref[...]) + sh_ref[...]
    o_ref[...] = _dot(h.astype(BF16), w_ref[...]).astype(o_ref.dtype)


def _norm_mod_matmul(x, g_row, modt_l, w_bf, seq_fn, *, tm, name, out_dtype=F32):
    t, n = x.shape[0], w_bf.shape[1]
    return pl.pallas_call(
        _nmm_body,
        grid=(t // tm,),
        in_specs=[
            pl.BlockSpec((tm, D_MODEL), lambda i: (i, 0)),
            pl.BlockSpec((1, D_MODEL), lambda i: (0, 0)),
            _mod_spec(0, seq_fn, 1),
            _mod_spec(1, seq_fn, 1),
            pl.BlockSpec((D_MODEL, n), lambda i: (0, 0)),
        ],
        out_specs=pl.BlockSpec((tm, n), lambda i: (i, 0)),
        out_shape=jax.ShapeDtypeStruct((t, n), out_dtype),
        compiler_params=_cp(("parallel",)),
        name=name,
    )(x, g_row, modt_l, modt_l, w_bf)


def _mlp_body(x_ref, g_ref, sh_ref, sc_ref, gt_ref, w1_ref, w2_ref, fg_ref, o_ref, h_ref, acc_ref, *, final_norm):
    k = pl.program_id(1)

    @pl.when(k == 0)
    def _():
        h = _rms(x_ref[...], g_ref[...]) * (1.0 + sc_ref[...]) + sh_ref[...]
        h_ref[...] = h.astype(BF16)
        acc_ref[...] = jnp.zeros_like(acc_ref)

    a = jnp.maximum(_dot(h_ref[...], w1_ref[...].astype(BF16)), 0.0)
    acc_ref[...] += _dot((a * a).astype(BF16), w2_ref[...].astype(BF16))

    @pl.when(k == pl.num_programs(1) - 1)
    def _():
        y = x_ref[...] + gt_ref[...] * acc_ref[...]
        o_ref[...] = _rms(y, fg_ref[...]) if final_norm else y


def _mlp(x, g_row, modt_l, w1_bf, w2_bf, seq_fn, fg_row, *, tm, tf, final_norm):
    t = x.shape[0]
    return pl.pallas_call(
        functools.partial(_mlp_body, final_norm=final_norm),
        grid=(t // tm, D_FF // tf),
        in_specs=[
            pl.BlockSpec((tm, D_MODEL), lambda i, k: (i, 0)),
            pl.BlockSpec((1, D_MODEL), lambda i, k: (0, 0)),
            _mod_spec(3, seq_fn, 2),
            _mod_spec(4, seq_fn, 2),
            _mod_spec(5, seq_fn, 2),
            pl.BlockSpec((D_MODEL, tf), lambda i, k: (0, k)),
            pl.BlockSpec((tf, D_MODEL), lambda i, k: (k, 0)),
            pl.BlockSpec((1, D_MODEL), lambda i, k: (0, 0)),
        ],
        out_specs=pl.BlockSpec((tm, D_MODEL), lambda i, k: (i, 0)),
        out_shape=jax.ShapeDtypeStruct((t, D_MODEL), F32),
        scratch_shapes=[pltpu.VMEM((tm, D_MODEL), BF16), pltpu.VMEM((tm, D_MODEL), F32)],
        compiler_params=_cp(("parallel", "arbitrary")),
        name="mlp",
    )(x, g_row, modt_l, modt_l, modt_l, w1_bf, w2_bf, fg_row)


def _s5_operators(lam_re, lam_im, log_dt, b_re, b_im, c_re, c_im):
    hp = lax.Precision.HIGHEST
    t = S5_T
    dt = jnp.exp(log_dt)[..., None]
    ar, ai = lam_re * dt, lam_im * dt
    er = jnp.exp(ar)
    lbr, lbi = er * jnp.cos(ai), er * jnp.sin(ai)
    den = lam_re * lam_re + lam_im * lam_im
    fr = ((lbr - 1.0) * lam_re + lbi * lam_im) / den
    fi = (lbi * lam_re - (lbr - 1.0) * lam_im) / den
    bbr = fr[..., None] * b_re - fi[..., None] * b_im
    bbi = fr[..., None] * b_im + fi[..., None] * b_re
    k = jnp.arange(t + 1, dtype=F32)[:, None, None, None]
    pr = jnp.exp(k * ar[None]) * jnp.cos(k * ai[None])
    pi = jnp.exp(k * ar[None]) * jnp.sin(k * ai[None])
    pbr = pr[..., None] * bbr[None] - pi[..., None] * bbi[None]
    pbi = pr[..., None] * bbi[None] + pi[..., None] * bbr[None]
    pcr = c_re[None] * pr[:, :, :, None, :] - c_im[None] * pi[:, :, :, None, :]
    pci = c_re[None] * pi[:, :, :, None, :] + c_im[None] * pr[:, :, :, None, :]
    kern = (jnp.einsum("dgpn,kdgnq->kdgpq", c_re, pbr[:t], precision=hp)
            - jnp.einsum("dgpn,kdgnq->kdgpq", c_im, pbi[:t], precision=hp))
    import numpy as np

    jb, gpb = S5_JB, S5_GPB
    kw, sw4 = t * LANES, 4 * S5_SW

    def expand(compact, col_src, row_group, col_group):
        onehot = jnp.asarray(col_src[None, :] == np.arange(compact.shape[-1])[:, None], BF16)
        full = jnp.dot(compact.astype(BF16), onehot, preferred_element_type=BF16)
        return jnp.where(jnp.asarray(row_group[:, None] == col_group[None, :])[None], full, 0.0)

    state_rows = np.arange(sw4)
    chunk_rows = np.arange(kw)
    state_grp, chunk_grp = (state_rows // S5_N) % gpb, (chunk_rows // A_GROUP) % gpb
    state_src = (state_rows // S5_SW) * S5_N + state_rows % S5_N
    chunk_src = (chunk_rows // LANES) * A_GROUP + chunk_rows % A_GROUP

    sidx = jnp.arange(t)
    secs = jnp.stack([pbr[t - 1 - sidx, 0], pbi[t - 1 - sidx, 0], pbr[sidx, 1], pbi[sidx, 1]], axis=0)
    secs = secs.reshape(4, t, jb, gpb, S5_N, A_GROUP)
    bst = expand(jnp.transpose(secs, (2, 1, 3, 5, 0, 4)).reshape(jb, kw, 4 * S5_N), state_src, chunk_grp, state_grp)
    secs = jnp.stack([pcr[sidx + 1, 0], -pci[sidx + 1, 0], pcr[t - sidx, 1], -pci[t - sidx, 1]], axis=0)
    secs = secs.reshape(4, t, jb, gpb, A_GROUP, S5_N)
    cst = expand(jnp.transpose(secs, (2, 0, 3, 5, 1, 4)).reshape(jb, sw4, t * A_GROUP), chunk_src, state_grp, chunk_grp)
    noff = 2 * t - 1
    kd = jnp.concatenate([kern[1:, 1][::-1], (kern[0, 0] + kern[0, 1])[None], kern[1:, 0]], axis=0)
    kd = jnp.transpose(kd.reshape(noff, jb, gpb, A_GROUP, A_GROUP), (1, 0, 2, 4, 3))
    lane_rows = np.arange(noff * LANES)
    blocks = expand(kd.reshape(jb, noff * LANES, A_GROUP), np.arange(LANES) % A_GROUP,
                    (lane_rows // A_GROUP) % gpb, np.arange(LANES) // A_GROUP)
    off = np.arange(t)[None, :] - np.arange(t)[:, None] + t - 1
    wt = jnp.take(blocks.reshape(jb, noff, LANES, LANES), jnp.asarray(off), axis=1)
    wt = jnp.transpose(wt, (0, 1, 3, 2, 4)).reshape(jb, kw, kw)
    acoef = jnp.stack([pr[t, 0], pi[t, 0], pr[t, 1], pi[t, 1]], axis=0)
    acoef = acoef.reshape(4, jb, S5_SW).transpose(1, 0, 2).reshape(jb, 1, 4 * S5_SW)
    acoef = jnp.broadcast_to(acoef, (jb, S5_BG, 4 * S5_SW))
    return bst.astype(BF16), cst.astype(BF16), wt.astype(BF16), acoef


def _s5_in_body(u_ref, bst_ref, lhs_ref, s_ref, *, seq_len, row_tile):
    nc = seq_len // S5_T

    @pl.when(pl.program_id(2) == 0)
    def _():
        def build(cp, carry):
            base = cp * (2 * S5_T)
            for s in range(S5_T):
                a = u_ref[pl.ds(base + s, S5_BG, stride=seq_len), :]
                b = u_ref[pl.ds(base + S5_T + s, S5_BG, stride=seq_len), :]
                lhs_ref[pl.ds(pl.multiple_of(cp * 16, 16), 16), s * LANES:(s + 1) * LANES] = (
                    jnp.concatenate([a, b], axis=0).astype(BF16))
            return carry

        lax.fori_loop(0, nc // 2, build, 0)

    def mm(r, carry):
        rows = pl.ds(pl.multiple_of(r * row_tile, row_tile), row_tile)
        s_ref[rows, :] = _dot(lhs_ref[rows, :], bst_ref[...])
        return carry

    lax.fori_loop(0, (nc * S5_BG) // row_tile, mm, 0)


def _s5_state_in(proj2d, bst, *, nbg, seq_len):
    nc = seq_len // S5_T
    rows = nc * S5_BG
    kw = S5_T * LANES
    half = 2 * S5_SW
    row_tile = min(rows, 256)
    return pl.pallas_call(
        functools.partial(_s5_in_body, seq_len=seq_len, row_tile=row_tile),
        grid=(nbg, S5_JB, 2),
        in_specs=[
            pl.BlockSpec((S5_BG * seq_len, LANES), lambda g, j, n: (g, j)),
            pl.BlockSpec((None, kw, half), lambda g, j, n: (j, 0, n)),
        ],
        out_specs=[
            pl.BlockSpec((None, None, rows, kw), lambda g, j, n: (g, j, 0, 0)),
            pl.BlockSpec((None, None, rows, half), lambda g, j, n: (g, j, 0, n)),
        ],
        out_shape=[
            jax.ShapeDtypeStruct((nbg, S5_JB, rows, kw), BF16),
            jax.ShapeDtypeStruct((nbg, S5_JB, rows, 4 * S5_SW), F32),
        ],
        compiler_params=_cp(("parallel", "parallel", "arbitrary")),
        name="s5_state_in",
    )(proj2d, bst)


def _s5_scan_body(s_ref, a_ref, x0_ref, x_ref, fin_ref, *, nc):
    w = S5_SW
    afr, afi = a_ref[:, 0:w], a_ref[:, w:2 * w]
    abr, abi = a_ref[:, 2 * w:3 * w], a_ref[:, 3 * w:4 * w]

    def step(c, carry):
        xfr, xfi, xbr, xbi = carry
        rf = pl.ds(pl.multiple_of(c * S5_BG, S5_BG), S5_BG)
        rb = pl.ds(pl.multiple_of((nc - 1 - c) * S5_BG, S5_BG), S5_BG)
        x_ref[rf, 0:w] = xfr
        x_ref[rf, w:2 * w] = xfi
        x_ref[rb, 2 * w:3 * w] = xbr
        x_ref[rb, 3 * w:4 * w] = xbi
        nfr = afr * xfr - afi * xfi + s_ref[rf, 0:w]
        nfi = afr * xfi + afi * xfr + s_ref[rf, w:2 * w]
        nbr = abr * xbr - abi * xbi + s_ref[rb, 2 * w:3 * w]
        nbi = abr * xbi + abi * xbr + s_ref[rb, 3 * w:4 * w]
        return nfr, nfi, nbr, nbi

    init = (x0_ref[:, 0:w], x0_ref[:, w:2 * w], x0_ref[:, 2 * w:3 * w], x0_ref[:, 3 * w:4 * w])
    xfr, xfi, xbr, xbi = lax.fori_loop(0, nc, step, init)
    fin_ref[:, 0:w] = xfr
    fin_ref[:, w:2 * w] = xfi
    fin_ref[:, 2 * w:3 * w] = xbr
    fin_ref[:, 3 * w:4 * w] = xbi


def _s5_scan(s_all, acoef, x0, *, nbg, seq_len):
    nc = seq_len // S5_T
    rows = nc * S5_BG
    sw4 = 4 * S5_SW
    return pl.pallas_call(
        functools.partial(_s5_scan_body, nc=nc),
        grid=(nbg, S5_JB),
        in_specs=[
            pl.BlockSpec((None, None, rows, sw4), lambda g, j: (g, j, 0, 0)),
            pl.BlockSpec((None, S5_BG, sw4), lambda g, j: (j, 0, 0)),
            pl.BlockSpec((None, None, S5_BG, sw4), lambda g, j: (g, j, 0, 0)),
        ],
        out_specs=[
            pl.BlockSpec((None, None, rows, sw4), lambda g, j: (g, j, 0, 0)),
            pl.BlockSpec((None, None, S5_BG, sw4), lambda g, j: (g, j, 0, 0)),
        ],
        out_shape=[
            jax.ShapeDtypeStruct((nbg, S5_JB, rows, sw4), F32),
            jax.ShapeDtypeStruct((nbg, S5_JB, S5_BG, sw4), F32),
        ],
        compiler_params=_cp(("parallel", "parallel")),
        name="s5_scan",
    )(s_all, acoef, x0)


def _s5_out_body(lhs_ref, x_ref, wt_ref, cst_ref, y_ref, acc_ref, *, chunks):
    acc_ref[...] = _dot(lhs_ref[...], wt_ref[...]) + _dot(x_ref[...].astype(BF16), cst_ref[...])

    def unchunk(c, carry):
        rows = pl.ds(pl.multiple_of(c * S5_BG, S5_BG), S5_BG)
        for t in range(S5_T):
            y_ref[:, c * S5_T + t, :] = acc_ref[rows, t * LANES:(t + 1) * LANES]
        return carry

    lax.fori_loop(0, chunks, unchunk, 0)


def _s5_output(lhs, x_all, wt, cst, *, nbg, seq_len):
    nc = seq_len // S5_T
    chunks = min(nc, 32)
    tr = chunks * S5_BG
    kw = S5_T * LANES
    sw4 = 4 * S5_SW
    return pl.pallas_call(
        functools.partial(_s5_out_body, chunks=chunks),
        grid=(nbg, S5_JB, nc // chunks),
        in_specs=[
            pl.BlockSpec((None, None, tr, kw), lambda g, j, r: (g, j, r, 0)),
            pl.BlockSpec((None, None, tr, sw4), lambda g, j, r: (g, j, r, 0)),
            pl.BlockSpec((None, kw, kw), lambda g, j, r: (j, 0, 0)),
            pl.BlockSpec((None, sw4, kw), lambda g, j, r: (j, 0, 0)),
        ],
        out_specs=pl.BlockSpec((S5_BG, chunks * S5_T, LANES), lambda g, j, r: (g, r, j)),
        out_shape=jax.ShapeDtypeStruct((nbg * S5_BG, seq_len, MIX_A), F32),
        scratch_shapes=[pltpu.VMEM((tr, kw), F32)],
        compiler_params=_cp(("parallel", "parallel", "arbitrary")),
        name="s5_output",
    )(lhs, x_all, wt, cst)


def _s5_mix(proj2d, ops, x0, *, nbg, seq_len):
    bst, cst, wt, acoef = ops
    lhs, s_all = _s5_state_in(proj2d, bst, nbg=nbg, seq_len=seq_len)
    x_all, fin = _s5_scan(s_all, acoef, x0, nbg=nbg, seq_len=seq_len)
    y = _s5_output(lhs, x_all, wt, cst, nbg=nbg, seq_len=seq_len)
    return y, fin


def _s5_pack_state(re, im):
    b = re.shape[0]
    x = jnp.stack([re[:, 0], im[:, 0], re[:, 1], im[:, 1]], axis=1)
    x = x.reshape(b // S5_BG, S5_BG, 4, S5_JB, S5_SW)
    return jnp.transpose(x, (0, 3, 1, 2, 4)).reshape(b // S5_BG, S5_JB, S5_BG, 4 * S5_SW)


def _s5_unpack_state(fin):
    nbg = fin.shape[0]
    x = fin.reshape(nbg, S5_JB, S5_BG, 4, S5_SW)
    x = jnp.transpose(x, (0, 2, 3, 1, 4)).reshape(nbg * S5_BG, 4, A_GROUPS, S5_N)
    return jnp.stack([x[:, 0], x[:, 2]], axis=1), jnp.stack([x[:, 1], x[:, 3]], axis=1)


Q_BLK, K_BLK, V_BLK = MIX_A // LANES, (MIX_A + MIX_B) // LANES, (MIX_A + 2 * MIX_B) // LANES


def _pair_softmax_pv(q, parts):
    lane = lax.broadcasted_iota(jnp.int32, (1, LANES), 1)
    outs = []
    for half in range(2):
        sel = (lane < NA_HD) if half == 0 else (lane >= NA_HD)
        qm = jnp.where(sel, q, 0.0).astype(BF16)
        scores = []
        for k, _, bias in parts:
            s = _dot_nt(qm, k)
            if bias is not None:
                s = s + bias[half]
            scores.append(s)
        m = scores[0].max(axis=-1, keepdims=True)
        for s in scores[1:]:
            m = jnp.maximum(m, s.max(axis=-1, keepdims=True))
        l = None
        o = None
        for s, (_, v, _) in zip(scores, parts):
            p = jnp.exp(s - m)
            ls = p.sum(axis=-1, keepdims=True)
            os_ = _dot(p.astype(BF16), v)
            l = ls if l is None else l + ls
            o = os_ if o is None else o + os_
        outs.append(o / l)
    return jnp.where(lane < NA_HD, outs[0], outs[1]).astype(BF16)


def _ctx_attn_body(q_ref, k_ref, v_ref, o_ref):
    q = q_ref[...] * (NA_HD ** -0.5)
    o_ref[...] = _pair_softmax_pv(q, [(k_ref[...].astype(BF16), v_ref[...].astype(BF16), None)])


def _context_attention(proj, *, nb, seq_len):
    npair = NA_HEADS // 2
    return pl.pallas_call(
        _ctx_attn_body,
        grid=(nb, npair),
        in_specs=[
            pl.BlockSpec((seq_len, LANES), lambda b, h: (b, Q_BLK + h)),
            pl.BlockSpec((seq_len, LANES), lambda b, h: (b, K_BLK + h)),
            pl.BlockSpec((seq_len, LANES), lambda b, h: (b, V_BLK + h)),
        ],
        out_specs=pl.BlockSpec((seq_len, LANES), lambda b, h: (b, h)),
        out_shape=jax.ShapeDtypeStruct((nb * seq_len, MIX_B), BF16),
        compiler_params=_cp(("parallel", "parallel")),
        name="context_attention",
    )(proj, proj, proj)


NA_QROWS = 4
NA_WROWS = NA_QROWS + NA_KH


def _na_window_start(g, rows):
    rs = jnp.clip(g * NA_QROWS - NA_KH // 2, 0, rows - NA_KH)
    return jnp.minimum(rs, rows - NA_WROWS)


def _na_bias_tables(rpb, rows):
    import numpy as np

    ngroups = rows // NA_QROWS
    shp = (NA_QROWS * GRID_W, NA_WROWS * GRID_W)
    ndc = 2 * NA_KW - 1
    c, kc = np.meshgrid(np.arange(GRID_W), np.arange(GRID_W), indexing="ij")
    cs = np.clip(c - NA_KW // 2, 0, GRID_W - NA_KW)
    col_ok = (kc >= cs) & (kc < cs + NA_KW)
    col_sel = ((kc - c + NA_KW - 1)[None] == np.arange(ndc)[:, None, None]) & col_ok[None]
    dr_idx, valid = [], []
    for g in (0, 1, ngroups - 1):
        rs0 = int(np.clip(g * NA_QROWS - NA_KH // 2, 0, rows - NA_KH))
        ws = min(rs0, rows - NA_WROWS)
        ri, wi = np.meshgrid(np.arange(NA_QROWS), np.arange(NA_WROWS), indexing="ij")
        r = g * NA_QROWS + ri
        rs = np.clip(r - NA_KH // 2, 0, rows - NA_KH)
        row = ws + wi
        row_ok = (row >= rs) & (row < rs + NA_KH)
        dr_idx.append(np.clip(row - r + NA_KH - 1, 0, 2 * NA_KH - 2))
        valid.append((row_ok[:, None, :, None] & col_ok[None, :, None, :]).reshape(shp))
    rows_sel = rpb[:, np.stack(dr_idx)]
    t = jnp.einsum("hgrwd,dck->hgrcwk", rows_sel, jnp.asarray(col_sel, F32), precision=lax.Precision.HIGHEST)
    t = jnp.where(jnp.asarray(np.stack(valid))[None], t.reshape(NA_HEADS, 3, *shp), NEG_BIG)
    t = t.reshape(NA_HEADS // 2, 2, 3, *shp)
    return jnp.transpose(t, (0, 2, 1, 3, 4))


NA_GPS = 2


def _nbr_attn_body(q_ref, k_ref, v_ref, ck_ref, cv_ref, bias_ref, o_ref, *, rows):
    ngroups = rows // NA_QROWS
    qt = NA_QROWS * GRID_W
    ck, cv = ck_ref[...].astype(BF16), cv_ref[...].astype(BF16)
    for u in range(NA_GPS):
        g = pl.program_id(2) * NA_GPS + u
        case = jnp.where(g == 0, 0, jnp.where(g == ngroups - 1, 2, 1))
        ws = _na_window_start(g, rows)
        win = pl.ds(pl.multiple_of(ws * GRID_W, GRID_W), NA_WROWS * GRID_W)
        qrows = slice(u * qt, (u + 1) * qt)
        q = q_ref[qrows, :] * (NA_HD ** -0.5)
        kw = k_ref[win, :].astype(BF16)
        vw = v_ref[win, :].astype(BF16)
        bias = (bias_ref[case, 0], bias_ref[case, 1])
        o_ref[qrows, :] = _pair_softmax_pv(q, [(kw, vw, bias), (ck, cv, None)])


def _neighbourhood_attention(proj, ck, cv, bias, *, nb, seq_len):
    rows = seq_len // GRID_W
    ngroups = rows // NA_QROWS // NA_GPS
    npair = NA_HEADS // 2
    qt = NA_GPS * NA_QROWS * GRID_W
    past = ck.shape[1]
    return pl.pallas_call(
        functools.partial(_nbr_attn_body, rows=rows),
        grid=(npair, nb, ngroups),
        in_specs=[
            pl.BlockSpec((qt, LANES), lambda h, b, g: (b * ngroups + g, Q_BLK + h)),
            pl.BlockSpec((seq_len, LANES), lambda h, b, g: (b, K_BLK + h)),
            pl.BlockSpec((seq_len, LANES), lambda h, b, g: (b, V_BLK + h)),
            pl.BlockSpec((None, past, LANES), lambda h, b, g: (b, 0, h)),
            pl.BlockSpec((None, past, LANES), lambda h, b, g: (b, 0, h)),
            pl.BlockSpec((None, 3, 2, NA_QROWS * GRID_W, NA_WROWS * GRID_W), lambda h, b, g: (h, 0, 0, 0, 0)),
        ],
        out_specs=pl.BlockSpec((qt, LANES), lambda h, b, g: (b * ngroups + g, h)),
        out_shape=jax.ShapeDtypeStruct((nb * seq_len, MIX_B), BF16),
        compiler_params=_cp(("parallel", "parallel", "parallel")),
        name="neighbourhood_attention",
    )(proj, proj, proj, ck, cv, bias)


def _even_out_body(x_ref, y_ref, u_ref, bo_ref, d_ref, wg_ref, bg_ref, wo_ref, gt_ref, o_ref):
    z = _gelu_tanh(y_ref[...] + d_ref[...] * u_ref[...])
    a = z * _sigmoid(_dot(z.astype(BF16), wg_ref[...]) + bg_ref[...])
    cat = jnp.concatenate([a.astype(BF16), bo_ref[...]], axis=-1)
    o_ref[...] = x_ref[...] + gt_ref[...] * _dot(cat, wo_ref[...])


def _even_out(x, y, proj, bo, d_row, wg_bf, bg_row, wo_bf, modt_l, seq_fn, *, tm):
    t = x.shape[0]
    return pl.pallas_call(
        _even_out_body,
        grid=(t // tm,),
        in_specs=[
            pl.BlockSpec((tm, D_MODEL), lambda i: (i, 0)),
            pl.BlockSpec((tm, MIX_A), lambda i: (i, 0)),
            pl.BlockSpec((tm, MIX_A), lambda i: (i, 0)),
            pl.BlockSpec((tm, MIX_B), lambda i: (i, 0)),
            pl.BlockSpec((1, MIX_A), lambda i: (0, 0)),
            pl.BlockSpec((MIX_A, MIX_A), lambda i: (0, 0)),
            pl.BlockSpec((1, MIX_A), lambda i: (0, 0)),
            pl.BlockSpec((D_MODEL, D_MODEL), lambda i: (0, 0)),
            _mod_spec(2, seq_fn, 1),
        ],
        out_specs=pl.BlockSpec((tm, D_MODEL), lambda i: (i, 0)),
        out_shape=jax.ShapeDtypeStruct((t, D_MODEL), F32),
        compiler_params=_cp(("parallel",)),
        name="even_out",
    )(x, y, proj, bo, d_row, wg_bf, bg_row, wo_bf, modt_l)


DN_QKV = 3 * D_MODEL
DN_PROJ_PAD = 4 * D_MODEL + LANES
DN_GATE_BLK = (4 * D_MODEL) // LANES
DN_CB = 512
DN_HALO = 2 * SUBLANES


def _gdn_pre_body(x_ref, w_ref, o_ref, pad_ref, *, seq_len, rt):
    kind = pl.program_id(1) // (D_MODEL // DN_CB)
    zeros = jnp.zeros((DN_HALO, DN_CB), BF16)
    pad_ref[0:DN_HALO, :] = zeros
    pad_ref[DN_HALO + seq_len:DN_HALO + seq_len + DN_HALO, :] = zeros
    pad_ref[DN_HALO:DN_HALO + seq_len, :] = x_ref[...]
    qscale = jnp.where(kind == 0, DN_DK ** -0.5, 1.0)
    lo = DN_HALO - (DN_CONV - 1) // 2

    def tile(r, carry):
        base = pl.multiple_of(r * rt, rt)
        xt = pad_ref[pl.ds(base, rt + 2 * DN_HALO), :].astype(F32)
        y = None
        for k in range(DN_CONV):
            term = w_ref[k:k + 1, :] * xt[lo + k:lo + k + rt, :]
            y = term if y is None else y + term
        y = _silu(y)
        segs = []
        for h in range(DN_CB // DN_DK):
            seg = y[:, h * DN_DK:(h + 1) * DN_DK]
            nrm = seg * lax.rsqrt(jnp.sum(seg * seg, axis=-1, keepdims=True) + EPS) * qscale
            segs.append(jnp.where(kind == 2, seg, nrm))
        o_ref[pl.ds(base, rt), :] = jnp.concatenate(segs, axis=-1).astype(BF16)
        return carry

    lax.fori_loop(0, seq_len // rt, tile, 0)


def _gdn_pre(proj, conv_w, *, nb, seq_len):
    rt = 256
    return pl.pallas_call(
        functools.partial(_gdn_pre_body, seq_len=seq_len, rt=rt),
        grid=(nb, DN_QKV // DN_CB),
        in_specs=[
            pl.BlockSpec((seq_len, DN_CB), lambda b, c: (b, c)),
            pl.BlockSpec((DN_CONV, DN_CB), lambda b, c: (0, c)),
        ],
        out_specs=pl.BlockSpec((seq_len, DN_CB), lambda b, c: (b, c)),
        out_shape=jax.ShapeDtypeStruct((nb * seq_len, DN_QKV), BF16),
        scratch_shapes=[pltpu.VMEM((seq_len + 2 * DN_HALO, DN_CB), BF16)],
        compiler_params=_cp(("parallel", "parallel")),
        name="gdn_pre",
    )(proj, conv_w)


def _split_bf16(x, n):
    parts = []
    r = x
    for _ in range(n):
        p = r.astype(BF16)
        parts.append(p)
        r = r - p.astype(F32)
    return parts


def _mm_exact_lhs(a_bf, b):
    return sum(_dot(a_bf, p) for p in _split_bf16(b, 3))


def _softplus(x):
    return jnp.maximum(x, 0.0) + jnp.log(1.0 + jnp.exp(-jnp.abs(x)))


def _gdn_body(q_ref, k_ref, v_ref, gt_ref, ga_ref, gb_ref, s0_ref, o_ref, sl_ref,
              s_scr, vn_scr, a_scr, t_scr, m_scr, attn_scr, rhs_scr, u_scr, w_scr, qg_scr, kg_scr, *, direction):
    t = pl.program_id(1)
    nt = pl.num_programs(1)
    tile = GDN_TILE
    c_len = DN_CHUNK
    nchunk = tile // c_len
    heads = range(DN_HEADS)

    @pl.when(t == 0)
    def _():
        s_scr[...] = s0_ref[...]
        vn_scr[...] = jnp.zeros_like(vn_scr)

    ri = lax.broadcasted_iota(jnp.int32, (tile, tile), 0)
    ci = lax.broadcasted_iota(jnp.int32, (tile, tile), 1)
    same = (ri // c_len) == (ci // c_len)
    if direction == 0:
        incl = same & (ci <= ri)
        strict = same & (ci < ri)
    else:
        incl = same & (ci >= ri)
        strict = same & (ci > ri)
    tri_bf = jnp.where(incl, 1.0, 0.0).astype(BF16)
    blk_bf = jnp.where(same, 1.0, 0.0).astype(BF16)
    nlev = int(math.log2(c_len))

    def off_mask(lvl, rs=slice(0, tile)):
        nrow = rs.stop - rs.start
        r = lax.broadcasted_iota(jnp.int32, (nrow, tile), 0) + rs.start
        c = lax.broadcasted_iota(jnp.int32, (nrow, tile), 1)
        late, early = (r, c) if direction == 0 else (c, r)
        return (((late >> lvl) & 1) == 1) & (((early >> lvl) & 1) == 0) & ((r >> (lvl + 1)) == (c >> (lvl + 1)))

    def active_blocks(lvl):
        hb = 1 << lvl
        return [slice(rb * hb, (rb + 1) * hb) for rb in range(tile // hb) if rb % 2 == 1 - direction]

    gt = gt_ref[...].astype(F32)
    g_all = ga_ref[...] * _softplus(gt + gb_ref[...])
    beta_all = _sigmoid(gt)
    gc_all = _mm_exact_lhs(tri_bf, g_all)
    gl_all = _mm_exact_lhs(blk_bf, g_all)
    gc_t = jnp.transpose(gc_all)
    chunk_order = range(nchunk) if direction == 0 else range(nchunk - 1, -1, -1)
    eye = jnp.where(ri == ci, 1.0, 0.0)

    def lane(h):
        return direction * DN_HEADS + h

    def hsl(h):
        return slice(h * DN_DK, (h + 1) * DN_DK)

    for h in heads:
        gcol = gc_all[:, lane(h):lane(h) + 1]
        grow = gc_t[lane(h):lane(h) + 1, :]
        glcol = gl_all[:, lane(h):lane(h) + 1]
        bcol = beta_all[:, 2 * DN_HEADS + lane(h):2 * DN_HEADS + lane(h) + 1]
        q_bf, k_bf = q_ref[:, hsl(h)], k_ref[:, hsl(h)]
        qh, kh, vh = q_bf.astype(F32), k_bf.astype(F32), v_ref[:, hsl(h)].astype(F32)
        decay = jnp.exp(jnp.where(incl, gcol - grow, NEG_BIG))
        a_mat = jnp.where(strict, bcol * _dot_nt(k_bf, k_bf) * decay, 0.0)
        a_scr[h] = a_mat
        t_scr[h] = (eye - jnp.where(off_mask(0), a_mat, 0.0)).astype(BF16)
        attn_scr[h] = (_dot_nt(q_bf, k_bf) * decay).astype(BF16)
        egc = jnp.exp(gcol)
        rhs_scr[h] = jnp.concatenate([vh * bcol, kh * (bcol * egc)], axis=-1).astype(BF16)
        qg_scr[h] = (qh * egc).astype(BF16)
        kg_scr[h] = (kh * jnp.exp(glcol - gcol)).astype(BF16)

    for lvl in range(1, nlev):
        hb = 1 << lvl
        if hb % 16:
            for h in heads:
                m_scr[h] = _dot(jnp.where(off_mask(lvl), a_scr[h], 0.0).astype(BF16), t_scr[h]).astype(BF16)
            for h in heads:
                t_scr[h] = t_scr[h] - _dot(t_scr[h], m_scr[h]).astype(BF16)
            continue
        blocks = active_blocks(lvl)
        for h in heads:
            lhs = jnp.concatenate([jnp.where(off_mask(lvl, rs), a_scr[h, rs, :], 0.0) for rs in blocks], axis=0)
            m_act = _dot(lhs.astype(BF16), t_scr[h])
            zero = jnp.zeros((hb, tile), F32)
            pieces = [zero] * (tile // hb)
            for i, rs in enumerate(blocks):
                pieces[rs.start // hb] = m_act[i * hb:(i + 1) * hb, :]
            m_scr[h] = jnp.concatenate(pieces, axis=0).astype(BF16)
        for h in heads:
            t_act = jnp.concatenate([t_scr[h, rs, :].astype(F32) for rs in blocks], axis=0)
            upd = _dot(t_act.astype(BF16), m_scr[h])
            for i, rs in enumerate(blocks):
                t_scr[h, rs, :] = t_scr[h, rs, :] - upd[i * hb:(i + 1) * hb, :].astype(BF16)

    for h in heads:
        x = _dot(t_scr[h], rhs_scr[h])
        u_scr[h] = x[:, :DN_DV]
        w_scr[h] = x[:, DN_DV:].astype(BF16)

    for c in chunk_order:
        rows = slice(c * c_len, (c + 1) * c_len)
        for h in heads:
            s = s_scr[h]
            s_bf = s.astype(BF16)
            v_new = u_scr[h, rows, :] - _dot(w_scr[h, rows, :], s_bf)
            vn_scr[h, rows, :] = v_new.astype(BF16)
            o_ref[rows, hsl(h)] = (_dot(qg_scr[h, rows, :], s_bf)
                                   + _dot(attn_scr[h, rows, :], vn_scr[h])).astype(o_ref.dtype)
            gl_c = gl_all[c * c_len:c * c_len + 1, lane(h):lane(h) + 1]
            s_scr[h] = s * jnp.exp(gl_c) + _dot_tn(kg_scr[h, rows, :], v_new.astype(BF16))

    @pl.when(t == nt - 1)
    def _():
        sl_ref[...] = s_scr[...]


def _gdn_scan(qkv, proj, ga_row, gb_row, s0, *, nb, seq_len, direction):
    nt = seq_len // GDN_TILE
    tile = GDN_TILE
    nq = D_MODEL // D_MODEL

    def row(b, t):
        return b * nt + (t if direction == 0 else nt - 1 - t)

    return pl.pallas_call(
        functools.partial(_gdn_body, direction=direction),
        grid=(nb, nt),
        in_specs=[
            pl.BlockSpec((tile, D_MODEL), lambda b, t: (row(b, t), 0)),
            pl.BlockSpec((tile, D_MODEL), lambda b, t: (row(b, t), nq)),
            pl.BlockSpec((tile, D_MODEL), lambda b, t: (row(b, t), 2 * nq)),
            pl.BlockSpec((tile, LANES), lambda b, t: (row(b, t), DN_GATE_BLK)),
            pl.BlockSpec((1, LANES), lambda b, t: (0, 0)),
            pl.BlockSpec((1, LANES), lambda b, t: (0, 0)),
            pl.BlockSpec((None, DN_HEADS, DN_DK, DN_DV), lambda b, t: (b, 0, 0, 0)),
        ],
        out_specs=[
            pl.BlockSpec((tile, D_MODEL), lambda b, t: (row(b, t), 0)),
            pl.BlockSpec((None, DN_HEADS, DN_DK, DN_DV), lambda b, t: (b, 0, 0, 0)),
        ],
        out_shape=[
            jax.ShapeDtypeStruct((nb * seq_len, D_MODEL), BF16),
            jax.ShapeDtypeStruct((nb, DN_HEADS, DN_DK, DN_DV), F32),
        ],
        scratch_shapes=[
            pltpu.VMEM((DN_HEADS, DN_DK, DN_DV), F32),
            pltpu.VMEM((DN_HEADS, tile, DN_DV), BF16),
            pltpu.VMEM((DN_HEADS, tile, tile), F32),
            pltpu.VMEM((DN_HEADS, tile, tile), BF16),
            pltpu.VMEM((DN_HEADS, tile, tile), BF16),
            pltpu.VMEM((DN_HEADS, tile, tile), BF16),
            pltpu.VMEM((DN_HEADS, tile, 2 * DN_DV), BF16),
            pltpu.VMEM((DN_HEADS, tile, DN_DV), F32),
            pltpu.VMEM((DN_HEADS, tile, DN_DK), BF16),
            pltpu.VMEM((DN_HEADS, tile, DN_DK), BF16),
            pltpu.VMEM((DN_HEADS, tile, DN_DK), BF16),
        ],
        compiler_params=_cp(("parallel", "arbitrary")),
        name="gdn_scan_fwd" if direction == 0 else "gdn_scan_bwd",
    )(qkv, qkv, qkv, proj, ga_row, gb_row, s0)


def _odd_out_body(x_ref, of_ref, ob_ref, z_ref, ng_ref, wo_ref, gt_ref, o_ref):
    o = of_ref[...].astype(F32) + ob_ref[...].astype(F32)
    segs = []
    for h in range(DN_HEADS):
        seg = o[:, h * DN_DV:(h + 1) * DN_DV]
        segs.append(seg * lax.rsqrt(jnp.mean(seg * seg, axis=-1, keepdims=True) + EPS))
    y = jnp.concatenate(segs, axis=-1) * ng_ref[...] * _silu(z_ref[...].astype(F32))
    o_ref[...] = x_ref[...] + gt_ref[...] * _dot(y.astype(BF16), wo_ref[...])


def _odd_out(x, o_f, o_b, proj, ng_row, wo_bf, modt_l, seq_fn, *, tm):
    t = x.shape[0]
    zblk = DN_QKV // D_MODEL
    return pl.pallas_call(
        _odd_out_body,
        grid=(t // tm,),
        in_specs=[
            pl.BlockSpec((tm, D_MODEL), lambda i: (i, 0)),
            pl.BlockSpec((tm, D_MODEL), lambda i: (i, 0)),
            pl.BlockSpec((tm, D_MODEL), lambda i: (i, 0)),
            pl.BlockSpec((tm, D_MODEL), lambda i: (i, zblk)),
            pl.BlockSpec((1, D_MODEL), lambda i: (0, 0)),
            pl.BlockSpec((D_MODEL, D_MODEL), lambda i: (0, 0)),
            _mod_spec(2, seq_fn, 1),
        ],
        out_specs=pl.BlockSpec((tm, D_MODEL), lambda i: (i, 0)),
        out_shape=jax.ShapeDtypeStruct((t, D_MODEL), F32),
        compiler_params=_cp(("parallel",)),
        name="odd_out",
    )(x, o_f, o_b, proj, ng_row, wo_bf, modt_l)


ROW_TILE = 1024


def _seq_fn(slot0, seq_len, tm):
    if slot0 == 0:
        return lambda i: 0
    return lambda i: slot0 + (i * tm) // seq_len


def kernel(x_prompt, x_sample, c, cache_na_k, cache_na_v, state_s5_re, state_s5_im, state_dn, c_ctx, norm_mix_g, norm_ff_g, w_mod, b_mod, w_ff1, w_ff2, w_in_e, w_out_e, s5_lam_re, s5_lam_im, s5_log_dt, s5_b_re, s5_b_im, s5_c_re, s5_c_im, s5_d, s5_w_glu, s5_b_glu, na_rpb, w_in_o, dn_conv_w, dn_a_log, dn_dt_bias, dn_norm_g, w_out_o, final_norm_g):
    nbp, lp = x_prompt.shape[:2]
    nbs, ls = x_sample.shape[:2]
    assert nbp % S5_BG == 0 and nbs % S5_BG == 0 and 1 + nbs <= 16
    streams = [
        dict(x=x_prompt.reshape(nbp * lp, D_MODEL), nb=nbp, L=lp, slot0=0),
        dict(x=x_sample.reshape(nbs * ls, D_MODEL), nb=nbs, L=ls, slot0=1),
    ]
    cond16 = jnp.zeros((16, D_MODEL), F32).at[0].set(c_ctx).at[1:1 + nbs].set(c)
    mod = _modulation(cond16, w_mod, b_mod[:, None, :])
    modt = jnp.transpose(mod.reshape(DEPTH, 16, 6, D_MODEL), (0, 2, 1, 3))[:, :, :, None, :]

    new_k, new_v, new_sre, new_sim, new_dn = [], [], [], [], []
    for l in range(DEPTH):
        e = l // 2
        if l % 2 == 0:
            w_in = w_in_e[e].astype(BF16)
            w_out = w_out_e[e].astype(BF16)
            w_glu = s5_w_glu[e].astype(BF16)
            ops = _s5_operators(s5_lam_re[e], s5_lam_im[e], s5_log_dt[e], s5_b_re[e], s5_b_im[e],
                                s5_c_re[e], s5_c_im[e])
            bias = _na_bias_tables(na_rpb[e], ls // GRID_W)
        else:
            w_in = jnp.pad(w_in_o[e], ((0, 0), (0, DN_PROJ_PAD - w_in_o.shape[-1]))).astype(BF16)
            w_out = w_out_o[e].astype(BF16)
            ga_row = jnp.zeros((1, LANES), F32).at[0, :2 * DN_HEADS].set(-jnp.exp(dn_a_log[e]).reshape(-1))
            gb_row = jnp.zeros((1, LANES), F32).at[0, :2 * DN_HEADS].set(dn_dt_bias[e].reshape(-1))
            ng_row = jnp.tile(dn_norm_g[e], DN_HEADS)[None]
        w1, w2 = w_ff1[l], w_ff2[l]
        for st in streams:
            x, nb, sl = st["x"], st["nb"], st["L"]
            seq = _seq_fn(st["slot0"], sl, ROW_TILE)
            is_ctx = st["slot0"] == 0
            if l % 2 == 0:
                proj = _norm_mod_matmul(x, norm_mix_g[l][None], modt[l], w_in, seq, tm=ROW_TILE, name="in_proj_even")
                nbg = nb // S5_BG
                if is_ctx:
                    x0 = jnp.zeros((nbg, S5_JB, S5_BG, 4 * S5_SW), F32)
                else:
                    x0 = _s5_pack_state(state_s5_re[:, e], state_s5_im[:, e])
                y, fin = _s5_mix(proj, ops, x0, nbg=nbg, seq_len=sl)
                if is_ctx:
                    bo = _context_attention(proj, nb=nb, seq_len=sl)
                    kv = proj.reshape(nb, sl, -1)
                    new_k.append(kv[:, :, MIX_A + MIX_B:MIX_A + 2 * MIX_B].reshape(nb, sl, NA_HEADS, NA_HD))
                    new_v.append(kv[:, :, MIX_A + 2 * MIX_B:].reshape(nb, sl, NA_HEADS, NA_HD))
                    sre, sim = _s5_unpack_state(fin)
                    new_sre.append(sre)
                    new_sim.append(sim)
                else:
                    past = cache_na_k.shape[2]
                    ck = cache_na_k[:, e].reshape(nb, past, MIX_B)
                    cv = cache_na_v[:, e].reshape(nb, past, MIX_B)
                    bo = _neighbourhood_attention(proj, ck, cv, bias, nb=nb, seq_len=sl)
                x = _even_out(x, y.reshape(nb * sl, MIX_A), proj, bo, s5_d[e][None], w_glu, s5_b_glu[e][None],
                              w_out, modt[l], seq, tm=ROW_TILE)
            else:
                proj = _norm_mod_matmul(x, norm_mix_g[l][None], modt[l], w_in, seq, tm=ROW_TILE, name="in_proj_odd",
                                        out_dtype=BF16)
                qkv = _gdn_pre(proj, dn_conv_w[e], nb=nb, seq_len=sl)
                if is_ctx:
                    s0 = jnp.zeros((nb, 2, DN_HEADS, DN_DK, DN_DV), F32)
                else:
                    s0 = state_dn[:, e]
                o_f, s_f = _gdn_scan(qkv, proj, ga_row, gb_row, s0[:, 0], nb=nb, seq_len=sl, direction=0)
                o_b, s_b = _gdn_scan(qkv, proj, ga_row, gb_row, s0[:, 1], nb=nb, seq_len=sl, direction=1)
                if is_ctx:
                    new_dn.append(jnp.stack([s_f, s_b], axis=1))
                x = _odd_out(x, o_f, o_b, proj, ng_row, w_out, modt[l], seq, tm=ROW_TILE)
            st["x"] = _mlp(x, norm_ff_g[l][None], modt[l], w1, w2, seq, final_norm_g[None], tm=ROW_TILE, tf=1024,
                           final_norm=(l == DEPTH - 1))

    y_prompt = streams[0]["x"].reshape(nbp, lp, D_MODEL)
    y_sample = streams[1]["x"].reshape(nbs, ls, D_MODEL)
    return (y_prompt, y_sample, jnp.stack(new_k, axis=1), jnp.stack(new_v, axis=1),
            jnp.stack(new_sre, axis=1), jnp.stack(new_sim, axis=1), jnp.stack(new_dn, axis=1))
```

```python
import functools
import math

import jax
import jax.numpy as jnp
from jax import lax
from jax.experimental import pallas as pl
from jax.experimental.pallas import tpu as pltpu

F32 = jnp.float32
BF16 = jnp.bfloat16

D_MODEL = 1024
DEPTH = 2
GRID_W = 64
MIX_A = D_MODEL // 2
A_GROUP = 16
A_GROUPS = MIX_A // A_GROUP
S5_N = 64
MIX_B = D_MODEL - MIX_A
NA_HEADS = 8
NA_HD = MIX_B // NA_HEADS
NA_KH = 8
NA_KW = 16
DN_HEADS = 8
DN_DK = D_MODEL // DN_HEADS
DN_DV = D_MODEL // DN_HEADS
DN_CONV = 5
DN_CHUNK = 64
D_FF = 4 * D_MODEL
EPS = 1e-6

LANES = 128
SUBLANES = 8
VMEM_LIMIT = 56 * 1024 * 1024
NEG_BIG = -1e30

S5_T = 16
S5_BG = SUBLANES
S5_JB = MIX_A // LANES
S5_GPB = LANES // A_GROUP
S5_SW = S5_GPB * S5_N
GDN_TILE = 256


def _cp(sem, vmem=VMEM_LIMIT):
    return pltpu.CompilerParams(dimension_semantics=sem, vmem_limit_bytes=vmem)


def _rms(x, g):
    return x * lax.rsqrt(jnp.mean(x * x, axis=-1, keepdims=True) + EPS) * g


def _sigmoid(x):
    return 1.0 / (1.0 + jnp.exp(-x))


def _silu(x):
    return x * _sigmoid(x)


def _gelu_tanh(x):
    return 0.5 * x * (1.0 + jnp.tanh(math.sqrt(2.0 / math.pi) * (x + 0.044715 * (x * x * x))))


def _dot(a, b):
    return jnp.dot(a, b, preferred_element_type=F32)


def _dot_nt(a, b):
    return lax.dot_general(a, b, (((1,), (1,)), ((), ())), preferred_element_type=F32)


def _dot_tn(a, b):
    return lax.dot_general(a, b, (((0,), (0,)), ((), ())), preferred_element_type=F32)


def _mod_body(c_ref, w_ref, b_ref, o_ref):
    c = c_ref[...]
    o_ref[...] = _dot(_silu(c).astype(BF16), w_ref[...].astype(BF16)) + b_ref[...]


def _modulation(cond16, w_mod_bf, b_mod):
    tn = 1536
    n = 6 * D_MODEL
    return pl.pallas_call(
        _mod_body,
        grid=(DEPTH, n // tn),
        in_specs=[
            pl.BlockSpec((16, D_MODEL), lambda l, j: (0, 0)),
            pl.BlockSpec((None, D_MODEL, tn), lambda l, j: (l, 0, j)),
            pl.BlockSpec((None, 1, tn), lambda l, j: (l, 0, j)),
        ],
        out_specs=pl.BlockSpec((None, 16, tn), lambda l, j: (l, 0, j)),
        out_shape=jax.ShapeDtypeStruct((DEPTH, 16, n), F32),
        compiler_params=_cp(("parallel", "parallel")),
        name="modulation",
    )(cond16, w_mod_bf, b_mod)


def _mod_spec(which, seq_fn, ngrid):
    if ngrid == 1:
        return pl.BlockSpec((None, None, 1, D_MODEL), lambda i: (which, seq_fn(i), 0, 0))
    return pl.BlockSpec((None, None, 1, D_MODEL), lambda i, j: (which, seq_fn(i), 0, 0))


def _nmm_body(x_ref, g_ref, sh_ref, sc_ref, w_ref, o_ref):
    h = _rms(x_ref[...], g_ref[...]) * (1.0 + sc_ref[...]) + sh_ref[...]
    o_ref[...] = _dot(h.astype(BF16), w_ref[...]).astype(o_ref.dtype)


def _norm_mod_matmul(x, g_row, modt_l, w_bf, seq_fn, *, tm, name, out_dtype=F32):
    t, n = x.shape[0], w_bf.shape[1]
    return pl.pallas_call(
        _nmm_body,
        grid=(t // tm,),
        in_specs=[
            pl.BlockSpec((tm, D_MODEL), lambda i: (i, 0)),
            pl.BlockSpec((1, D_MODEL), lambda i: (0, 0)),
            _mod_spec(0, seq_fn, 1),
            _mod_spec(1, seq_fn, 1),
            pl.BlockSpec((D_MODEL, n), lambda i: (0, 0)),
        ],
        out_specs=pl.BlockSpec((tm, n), lambda i: (i, 0)),
        out_shape=jax.ShapeDtypeStruct((t, n), out_dtype),
        compiler_params=_cp(("parallel",)),
        name=name,
    )(x, g_row, modt_l, modt_l, w_bf)


def _mlp_body(x_ref, g_ref, sh_ref, sc_ref, gt_ref, w1_ref, w2_ref, fg_ref, o_ref, h_ref, acc_ref, *, final_norm):
    k = pl.program_id(1)

    @pl.when(k == 0)
    def _():
        h = _rms(x_ref[...], g_ref[...]) * (1.0 + sc_ref[...]) + sh_ref[...]
        h_ref[...] = h.astype(BF16)
        acc_ref[...] = jnp.zeros_like(acc_ref)

    a = jnp.maximum(_dot(h_ref[...], w1_ref[...].astype(BF16)), 0.0)
    acc_ref[...] += _dot((a * a).astype(BF16), w2_ref[...].astype(BF16))

    @pl.when(k == pl.num_programs(1) - 1)
    def _():
        y = x_ref[...] + gt_ref[...] * acc_ref[...]
        o_ref[...] = _rms(y, fg_ref[...]) if final_norm else y


def _mlp(x, g_row, modt_l, w1_bf, w2_bf, seq_fn, fg_row, *, tm, tf, final_norm):
    t = x.shape[0]
    return pl.pallas_call(
        functools.partial(_mlp_body, final_norm=final_norm),
        grid=(t // tm, D_FF // tf),
        in_specs=[
            pl.BlockSpec((tm, D_MODEL), lambda i, k: (i, 0)),
            pl.BlockSpec((1, D_MODEL), lambda i, k: (0, 0)),
            _mod_spec(3, seq_fn, 2),
            _mod_spec(4, seq_fn, 2),
            _mod_spec(5, seq_fn, 2),
            pl.BlockSpec((D_MODEL, tf), lambda i, k: (0, k)),
            pl.BlockSpec((tf, D_MODEL), lambda i, k: (k, 0)),
            pl.BlockSpec((1, D_MODEL), lambda i, k: (0, 0)),
        ],
        out_specs=pl.BlockSpec((tm, D_MODEL), lambda i, k: (i, 0)),
        out_shape=jax.ShapeDtypeStruct((t, D_MODEL), F32),
        scratch_shapes=[pltpu.VMEM((tm, D_MODEL), BF16), pltpu.VMEM((tm, D_MODEL), F32)],
        compiler_params=_cp(("parallel", "arbitrary")),
        name="mlp",
    )(x, g_row, modt_l, modt_l, modt_l, w1_bf, w2_bf, fg_row)


def _s5_operators(lam_re, lam_im, log_dt, b_re, b_im, c_re, c_im):
    hp = lax.Precision.HIGHEST
    t = S5_T
    dt = jnp.exp(log_dt)[..., None]
    ar, ai = lam_re * dt, lam_im * dt
    er = jnp.exp(ar)
    lbr, lbi = er * jnp.cos(ai), er * jnp.sin(ai)
    den = lam_re * lam_re + lam_im * lam_im
    fr = ((lbr - 1.0) * lam_re + lbi * lam_im) / den
    fi = (lbi * lam_re - (lbr - 1.0) * lam_im) / den
    bbr = fr[..., None] * b_re - fi[..., None] * b_im
    bbi = fr[..., None] * b_im + fi[..., None] * b_re
    k = jnp.arange(t + 1, dtype=F32)[:, None, None, None]
    pr = jnp.exp(k * ar[None]) * jnp.cos(k * ai[None])
    pi = jnp.exp(k * ar[None]) * jnp.sin(k * ai[None])
    pbr = pr[..., None] * bbr[None] - pi[..., None] * bbi[None]
    pbi = pr[..., None] * bbi[None] + pi[..., None] * bbr[None]
    pcr = c_re[None] * pr[:, :, :, None, :] - c_im[None] * pi[:, :, :, None, :]
    pci = c_re[None] * pi[:, :, :, None, :] + c_im[None] * pr[:, :, :, None, :]
    kern = (jnp.einsum("dgpn,kdgnq->kdgpq", c_re, pbr[:t], precision=hp)
            - jnp.einsum("dgpn,kdgnq->kdgpq", c_im, pbi[:t], precision=hp))
    import numpy as np

    jb, gpb = S5_JB, S5_GPB
    kw, sw4 = t * LANES, 4 * S5_SW

    def expand(compact, col_src, row_group, col_group):
        onehot = jnp.asarray(col_src[None, :] == np.arange(compact.shape[-1])[:, None], BF16)
        full = jnp.dot(compact.astype(BF16), onehot, preferred_element_type=BF16)
        return jnp.where(jnp.asarray(row_group[:, None] == col_group[None, :])[None], full, 0.0)

    state_rows = np.arange(sw4)
    chunk_rows = np.arange(kw)
    state_grp, chunk_grp = (state_rows // S5_N) % gpb, (chunk_rows // A_GROUP) % gpb
    state_src = (state_rows // S5_SW) * S5_N + state_rows % S5_N
    chunk_src = (chunk_rows // LANES) * A_GROUP + chunk_rows % A_GROUP

    sidx = jnp.arange(t)
    secs = jnp.stack([pbr[t - 1 - sidx, 0], pbi[t - 1 - sidx, 0], pbr[sidx, 1], pbi[sidx, 1]], axis=0)
    secs = secs.reshape(4, t, jb, gpb, S5_N, A_GROUP)
    bst = expand(jnp.transpose(secs, (2, 1, 3, 5, 0, 4)).reshape(jb, kw, 4 * S5_N), state_src, chunk_grp, state_grp)
    secs = jnp.stack([pcr[sidx + 1, 0], -pci[sidx + 1, 0], pcr[t - sidx, 1], -pci[t - sidx, 1]], axis=0)
    secs = secs.reshape(4, t, jb, gpb, A_GROUP, S5_N)
    cst = expand(jnp.transpose(secs, (2, 0, 3, 5, 1, 4)).reshape(jb, sw4, t * A_GROUP), chunk_src, state_grp, chunk_grp)
    noff = 2 * t - 1
    kd = jnp.concatenate([kern[1:, 1][::-1], (kern[0, 0] + kern[0, 1])[None], kern[1:, 0]], axis=0)
    kd = jnp.transpose(kd.reshape(noff, jb, gpb, A_GROUP, A_GROUP), (1, 0, 2, 4, 3))
    lane_rows = np.arange(noff * LANES)
    blocks = expand(kd.reshape(jb, noff * LANES, A_GROUP), np.arange(LANES) % A_GROUP,
                    (lane_rows // A_GROUP) % gpb, np.arange(LANES) // A_GROUP)
    off = np.arange(t)[None, :] - np.arange(t)[:, None] + t - 1
    wt = jnp.take(blocks.reshape(jb, noff, LANES, LANES), jnp.asarray(off), axis=1)
    wt = jnp.transpose(wt, (0, 1, 3, 2, 4)).reshape(jb, kw, kw)
    acoef = jnp.stack([pr[t, 0], pi[t, 0], pr[t, 1], pi[t, 1]], axis=0)
    acoef = acoef.reshape(4, jb, S5_SW).transpose(1, 0, 2).reshape(jb, 1, 4 * S5_SW)
    acoef = jnp.broadcast_to(acoef, (jb, S5_BG, 4 * S5_SW))
    return bst.astype(BF16), cst.astype(BF16), wt.astype(BF16), acoef


def _s5_in_body(u_ref, bst_ref, lhs_ref, s_ref, *, seq_len, row_tile):
    nc = seq_len // S5_T

    @pl.when(pl.program_id(2) == 0)
    def _():
        def build(cp, carry):
            base = cp * (2 * S5_T)
            for s in range(S5_T):
                a = u_ref[pl.ds(base + s, S5_BG, stride=seq_len), :]
                b = u_ref[pl.ds(base + S5_T + s, S5_BG, stride=seq_len), :]
                lhs_ref[pl.ds(pl.multiple_of(cp * 16, 16), 16), s * LANES:(s + 1) * LANES] = (
                    jnp.concatenate([a, b], axis=0).astype(BF16))
            return carry

        lax.fori_loop(0, nc // 2, build, 0)

    def mm(r, carry):
        rows = pl.ds(pl.multiple_of(r * row_tile, row_tile), row_tile)
        s_ref[rows, :] = _dot(lhs_ref[rows, :], bst_ref[...])
        return carry

    lax.fori_loop(0, (nc * S5_BG) // row_tile, mm, 0)


def _s5_state_in(proj2d, bst, *, nbg, seq_len):
    nc = seq_len // S5_T
    rows = nc * S5_BG
    kw = S5_T * LANES
    half = 2 * S5_SW
    row_tile = min(rows, 256)
    return pl.pallas_call(
        functools.partial(_s5_in_body, seq_len=seq_len, row_tile=row_tile),
        grid=(nbg, S5_JB, 2),
        in_specs=[
            pl.BlockSpec((S5_BG * seq_len, LANES), lambda g, j, n: (g, j)),
            pl.BlockSpec((None, kw, half), lambda g, j, n: (j, 0, n)),
        ],
        out_specs=[
            pl.BlockSpec((None, None, rows, kw), lambda g, j, n: (g, j, 0, 0)),
            pl.BlockSpec((None, None, rows, half), lambda g, j, n: (g, j, 0, n)),
        ],
        out_shape=[
            jax.ShapeDtypeStruct((nbg, S5_JB, rows, kw), BF16),
            jax.ShapeDtypeStruct((nbg, S5_JB, rows, 4 * S5_SW), F32),
        ],
        compiler_params=_cp(("parallel", "parallel", "arbitrary")),
        name="s5_state_in",
    )(proj2d, bst)


def _s5_scan_body(s_ref, a_ref, x0_ref, x_ref, fin_ref, *, nc):
    w = S5_SW
    afr, afi = a_ref[:, 0:w], a_ref[:, w:2 * w]
    abr, abi = a_ref[:, 2 * w:3 * w], a_ref[:, 3 * w:4 * w]

    def step(c, carry):
        xfr, xfi, xbr, xbi = carry
        rf = pl.ds(pl.multiple_of(c * S5_BG, S5_BG), S5_BG)
        rb = pl.ds(pl.multiple_of((nc - 1 - c) * S5_BG, S5_BG), S5_BG)
        x_ref[rf, 0:w] = xfr
        x_ref[rf, w:2 * w] = xfi
        x_ref[rb, 2 * w:3 * w] = xbr
        x_ref[rb, 3 * w:4 * w] = xbi
        nfr = afr * xfr - afi * xfi + s_ref[rf, 0:w]
        nfi = afr * xfi + afi * xfr + s_ref[rf, w:2 * w]
        nbr = abr * xbr - abi * xbi + s_ref[rb, 2 * w:3 * w]
        nbi = abr * xbi + abi * xbr + s_ref[rb, 3 * w:4 * w]
        return nfr, nfi, nbr, nbi

    init = (x0_ref[:, 0:w], x0_ref[:, w:2 * w], x0_ref[:, 2 * w:3 * w], x0_ref[:, 3 * w:4 * w])
    xfr, xfi, xbr, xbi = lax.fori_loop(0, nc, step, init)
    fin_ref[:, 0:w] = xfr
    fin_ref[:, w:2 * w] = xfi
    fin_ref[:, 2 * w:3 * w] = xbr
    fin_ref[:, 3 * w:4 * w] = xbi


def _s5_scan(s_all, acoef, x0, *, nbg, seq_len):
    nc = seq_len // S5_T
    rows = nc * S5_BG
    sw4 = 4 * S5_SW
    return pl.pallas_call(
        functools.partial(_s5_scan_body, nc=nc),
        grid=(nbg, S5_JB),
        in_specs=[
            pl.BlockSpec((None, None, rows, sw4), lambda g, j: (g, j, 0, 0)),
            pl.BlockSpec((None, S5_BG, sw4), lambda g, j: (j, 0, 0)),
            pl.BlockSpec((None, None, S5_BG, sw4), lambda g, j: (g, j, 0, 0)),
        ],
        out_specs=[
            pl.BlockSpec((None, None, rows, sw4), lambda g, j: (g, j, 0, 0)),
            pl.BlockSpec((None, None, S5_BG, sw4), lambda g, j: (g, j, 0, 0)),
        ],
        out_shape=[
            jax.ShapeDtypeStruct((nbg, S5_JB, rows, sw4), F32),
            jax.ShapeDtypeStruct((nbg, S5_JB, S5_BG, sw4), F32),
        ],
        compiler_params=_cp(("parallel", "parallel")),
        name="s5_scan",
    )(s_all, acoef, x0)


def _s5_out_body(lhs_ref, x_ref, wt_ref, cst_ref, y_ref, acc_ref, *, chunks):
    acc_ref[...] = _dot(lhs_ref[...], wt_ref[...]) + _dot(x_ref[...].astype(BF16), cst_ref[...])

    def unchunk(c, carry):
        rows = pl.ds(pl.multiple_of(c * S5_BG, S5_BG), S5_BG)
        for t in range(S5_T):
            y_ref[:, c * S5_T + t, :] = acc_ref[rows, t * LANES:(t + 1) * LANES]
        return carry

    lax.fori_loop(0, chunks, unchunk, 0)


def _s5_output(lhs, x_all, wt, cst, *, nbg, seq_len):
    nc = seq_len // S5_T
    chunks = min(nc, 32)
    tr = chunks * S5_BG
    kw = S5_T * LANES
    sw4 = 4 * S5_SW
    return pl.pallas_call(
        functools.partial(_s5_out_body, chunks=chunks),
        grid=(nbg, S5_JB, nc // chunks),
        in_specs=[
            pl.BlockSpec((None, None, tr, kw), lambda g, j, r: (g, j, r, 0)),
            pl.BlockSpec((None, None, tr, sw4), lambda g, j, r: (g, j, r, 0)),
            pl.BlockSpec((None, kw, kw), lambda g, j, r: (j, 0, 0)),
            pl.BlockSpec((None, sw4, kw), lambda g, j, r: (j, 0, 0)),
        ],
        out_specs=pl.BlockSpec((S5_BG, chunks * S5_T, LANES), lambda g, j, r: (g, r, j)),
        out_shape=jax.ShapeDtypeStruct((nbg * S5_BG, seq_len, MIX_A), F32),
        scratch_shapes=[pltpu.VMEM((tr, kw), F32)],
        compiler_params=_cp(("parallel", "parallel", "arbitrary")),
        name="s5_output",
    )(lhs, x_all, wt, cst)


def _s5_mix(proj2d, ops, x0, *, nbg, seq_len):
    bst, cst, wt, acoef = ops
    lhs, s_all = _s5_state_in(proj2d, bst, nbg=nbg, seq_len=seq_len)
    x_all, fin = _s5_scan(s_all, acoef, x0, nbg=nbg, seq_len=seq_len)
    y = _s5_output(lhs, x_all, wt, cst, nbg=nbg, seq_len=seq_len)
    return y, fin


def _s5_pack_state(re, im):
    b = re.shape[0]
    x = jnp.stack([re[:, 0], im[:, 0], re[:, 1], im[:, 1]], axis=1)
    x = x.reshape(b // S5_BG, S5_BG, 4, S5_JB, S5_SW)
    return jnp.transpose(x, (0, 3, 1, 2, 4)).reshape(b // S5_BG, S5_JB, S5_BG, 4 * S5_SW)


def _s5_unpack_state(fin):
    nbg = fin.shape[0]
    x = fin.reshape(nbg, S5_JB, S5_BG, 4, S5_SW)
    x = jnp.transpose(x, (0, 2, 3, 1, 4)).reshape(nbg * S5_BG, 4, A_GROUPS, S5_N)
    return jnp.stack([x[:, 0], x[:, 2]], axis=1), jnp.stack([x[:, 1], x[:, 3]], axis=1)


Q_BLK, K_BLK, V_BLK = MIX_A // LANES, (MIX_A + MIX_B) // LANES, (MIX_A + 2 * MIX_B) // LANES


def _pair_softmax_pv(q, parts):
    lane = lax.broadcasted_iota(jnp.int32, (1, LANES), 1)
    outs = []
    for half in range(2):
        sel = (lane < NA_HD) if half == 0 else (lane >= NA_HD)
        qm = jnp.where(sel, q, 0.0).astype(BF16)
        scores = []
        for k, _, bias in parts:
            s = _dot_nt(qm, k)
            if bias is not None:
                s = s + bias[half]
            scores.append(s)
        m = scores[0].max(axis=-1, keepdims=True)
        for s in scores[1:]:
            m = jnp.maximum(m, s.max(axis=-1, keepdims=True))
        l = None
        o = None
        for s, (_, v, _) in zip(scores, parts):
            p = jnp.exp(s - m)
            ls = p.sum(axis=-1, keepdims=True)
            os_ = _dot(p.astype(BF16), v)
            l = ls if l is None else l + ls
            o = os_ if o is None else o + os_
        outs.append(o / l)
    return jnp.where(lane < NA_HD, outs[0], outs[1]).astype(BF16)


def _ctx_attn_body(q_ref, k_ref, v_ref, o_ref):
    q = q_ref[...] * (NA_HD ** -0.5)
    o_ref[...] = _pair_softmax_pv(q, [(k_ref[...].astype(BF16), v_ref[...].astype(BF16), None)])


def _context_attention(proj, *, nb, seq_len):
    npair = NA_HEADS // 2
    return pl.pallas_call(
        _ctx_attn_body,
        grid=(nb, npair),
        in_specs=[
            pl.BlockSpec((seq_len, LANES), lambda b, h: (b, Q_BLK + h)),
            pl.BlockSpec((seq_len, LANES), lambda b, h: (b, K_BLK + h)),
            pl.BlockSpec((seq_len, LANES), lambda b, h: (b, V_BLK + h)),
        ],
        out_specs=pl.BlockSpec((seq_len, LANES), lambda b, h: (b, h)),
        out_shape=jax.ShapeDtypeStruct((nb * seq_len, MIX_B), BF16),
        compiler_params=_cp(("parallel", "parallel")),
        name="context_attention",
    )(proj, proj, proj)


NA_QROWS = 4
NA_WROWS = NA_QROWS + NA_KH


def _na_window_start(g, rows):
    rs = jnp.clip(g * NA_QROWS - NA_KH // 2, 0, rows - NA_KH)
    return jnp.minimum(rs, rows - NA_WROWS)


def _na_bias_tables(rpb, rows):
    import numpy as np

    ngroups = rows // NA_QROWS
    shp = (NA_QROWS * GRID_W, NA_WROWS * GRID_W)
    ndc = 2 * NA_KW - 1
    c, kc = np.meshgrid(np.arange(GRID_W), np.arange(GRID_W), indexing="ij")
    cs = np.clip(c - NA_KW // 2, 0, GRID_W - NA_KW)
    col_ok = (kc >= cs) & (kc < cs + NA_KW)
    col_sel = ((kc - c + NA_KW - 1)[None] == np.arange(ndc)[:, None, None]) & col_ok[None]
    dr_idx, valid = [], []
    for g in (0, 1, ngroups - 1):
        rs0 = int(np.clip(g * NA_QROWS - NA_KH // 2, 0, rows - NA_KH))
        ws = min(rs0, rows - NA_WROWS)
        ri, wi = np.meshgrid(np.arange(NA_QROWS), np.arange(NA_WROWS), indexing="ij")
        r = g * NA_QROWS + ri
        rs = np.clip(r - NA_KH // 2, 0, rows - NA_KH)
        row = ws + wi
        row_ok = (row >= rs) & (row < rs + NA_KH)
        dr_idx.append(np.clip(row - r + NA_KH - 1, 0, 2 * NA_KH - 2))
        valid.append((row_ok[:, None, :, None] & col_ok[None, :, None, :]).reshape(shp))
    rows_sel = rpb[:, np.stack(dr_idx)]
    t = jnp.einsum("hgrwd,dck->hgrcwk", rows_sel, jnp.asarray(col_sel, F32), precision=lax.Precision.HIGHEST)
    t = jnp.where(jnp.asarray(np.stack(valid))[None], t.reshape(NA_HEADS, 3, *shp), NEG_BIG)
    t = t.reshape(NA_HEADS // 2, 2, 3, *shp)
    return jnp.transpose(t, (0, 2, 1, 3, 4))


NA_GPS = 4


def _nbr_attn_body(q_ref, k_ref, v_ref, ck_ref, cv_ref, bias_ref, o_ref, *, rows):
    ngroups = rows // NA_QROWS
    qt = NA_QROWS * GRID_W
    ck, cv = ck_ref[...].astype(BF16), cv_ref[...].astype(BF16)
    for u in range(NA_GPS):
        g = pl.program_id(2) * NA_GPS + u
        case = jnp.where(g == 0, 0, jnp.where(g == ngroups - 1, 2, 1))
        ws = _na_window_start(g, rows)
        win = pl.ds(pl.multiple_of(ws * GRID_W, GRID_W), NA_WROWS * GRID_W)
        qrows = slice(u * qt, (u + 1) * qt)
        q = q_ref[qrows, :] * (NA_HD ** -0.5)
        kw = k_ref[win, :].astype(BF16)
        vw = v_ref[win, :].astype(BF16)
        bias = (bias_ref[case, 0], bias_ref[case, 1])
        o_ref[qrows, :] = _pair_softmax_pv(q, [(kw, vw, bias), (ck, cv, None)])


def _neighbourhood_attention(proj, ck, cv, bias, *, nb, seq_len):
    rows = seq_len // GRID_W
    ngroups = rows // NA_QROWS // NA_GPS
    npair = NA_HEADS // 2
    qt = NA_GPS * NA_QROWS * GRID_W
    past = ck.shape[1]
    return pl.pallas_call(
        functools.partial(_nbr_attn_body, rows=rows),
        grid=(npair, nb, ngroups),
        in_specs=[
            pl.BlockSpec((qt, LANES), lambda h, b, g: (b * ngroups + g, Q_BLK + h)),
            pl.BlockSpec((seq_len, LANES), lambda h, b, g: (b, K_BLK + h)),
            pl.BlockSpec((seq_len, LANES), lambda h, b, g: (b, V_BLK + h)),
            pl.BlockSpec((None, past, LANES), lambda h, b, g: (b, 0, h)),
            pl.BlockSpec((None, past, LANES), lambda h, b, g: (b, 0, h)),
            pl.BlockSpec((None, 3, 2, NA_QROWS * GRID_W, NA_WROWS * GRID_W), lambda h, b, g: (h, 0, 0, 0, 0)),
        ],
        out_specs=pl.BlockSpec((qt, LANES), lambda h, b, g: (b * ngroups + g, h)),
        out_shape=jax.ShapeDtypeStruct((nb * seq_len, MIX_B), BF16),
        compiler_params=_cp(("parallel", "parallel", "parallel")),
        name="neighbourhood_attention",
    )(proj, proj, proj, ck, cv, bias)


def _even_out_body(x_ref, y_ref, u_ref, bo_ref, d_ref, wg_ref, bg_ref, wo_ref, gt_ref, o_ref):
    z = _gelu_tanh(y_ref[...] + d_ref[...] * u_ref[...])
    a = z * _sigmoid(_dot(z.astype(BF16), wg_ref[...]) + bg_ref[...])
    cat = jnp.concatenate([a.astype(BF16), bo_ref[...]], axis=-1)
    o_ref[...] = x_ref[...] + gt_ref[...] * _dot(cat, wo_ref[...])


def _even_out(x, y, proj, bo, d_row, wg_bf, bg_row, wo_bf, modt_l, seq_fn, *, tm):
    t = x.shape[0]
    return pl.pallas_call(
        _even_out_body,
        grid=(t // tm,),
        in_specs=[
            pl.BlockSpec((tm, D_MODEL), lambda i: (i, 0)),
            pl.BlockSpec((tm, MIX_A), lambda i: (i, 0)),
            pl.BlockSpec((tm, MIX_A), lambda i: (i, 0)),
            pl.BlockSpec((tm, MIX_B), lambda i: (i, 0)),
            pl.BlockSpec((1, MIX_A), lambda i: (0, 0)),
            pl.BlockSpec((MIX_A, MIX_A), lambda i: (0, 0)),
            pl.BlockSpec((1, MIX_A), lambda i: (0, 0)),
            pl.BlockSpec((D_MODEL, D_MODEL), lambda i: (0, 0)),
            _mod_spec(2, seq_fn, 1),
        ],
        out_specs=pl.BlockSpec((tm, D_MODEL), lambda i: (i, 0)),
        out_shape=jax.ShapeDtypeStruct((t, D_MODEL), F32),
        compiler_params=_cp(("parallel",)),
        name="even_out",
    )(x, y, proj, bo, d_row, wg_bf, bg_row, wo_bf, modt_l)


DN_QKV = 3 * D_MODEL
DN_PROJ_PAD = 4 * D_MODEL + LANES
DN_GATE_BLK = (4 * D_MODEL) // LANES
DN_CB = 512
DN_HALO = 2 * SUBLANES


def _gdn_pre_body(x_ref, w_ref, o_ref, pad_ref, *, seq_len, rt):
    kind = pl.program_id(1) // (D_MODEL // DN_CB)
    zeros = jnp.zeros((DN_HALO, DN_CB), BF16)
    pad_ref[0:DN_HALO, :] = zeros
    pad_ref[DN_HALO + seq_len:DN_HALO + seq_len + DN_HALO, :] = zeros
    pad_ref[DN_HALO:DN_HALO + seq_len, :] = x_ref[...]
    qscale = jnp.where(kind == 0, DN_DK ** -0.5, 1.0)
    lo = DN_HALO - (DN_CONV - 1) // 2

    def tile(r, carry):
        base = pl.multiple_of(r * rt, rt)
        xt = pad_ref[pl.ds(base, rt + 2 * DN_HALO), :].astype(F32)
        y = None
        for k in range(DN_CONV):
            term = w_ref[k:k + 1, :] * xt[lo + k:lo + k + rt, :]
            y = term if y is None else y + term
        y = _silu(y)
        segs = []
        for h in range(DN_CB // DN_DK):
            seg = y[:, h * DN_DK:(h + 1) * DN_DK]
            nrm = seg * lax.rsqrt(jnp.sum(seg * seg, axis=-1, keepdims=True) + EPS) * qscale
            segs.append(jnp.where(kind == 2, seg, nrm))
        o_ref[pl.ds(base, rt), :] = jnp.concatenate(segs, axis=-1).astype(BF16)
        return carry

    lax.fori_loop(0, seq_len // rt, tile, 0)


def _gdn_pre(proj, conv_w, *, nb, seq_len):
    rt = 256
    return pl.pallas_call(
        functools.partial(_gdn_pre_body, seq_len=seq_len, rt=rt),
        grid=(nb, DN_QKV // DN_CB),
        in_specs=[
            pl.BlockSpec((seq_len, DN_CB), lambda b, c: (b, c)),
            pl.BlockSpec((DN_CONV, DN_CB), lambda b, c: (0, c)),
        ],
        out_specs=pl.BlockSpec((seq_len, DN_CB), lambda b, c: (b, c)),
        out_shape=jax.ShapeDtypeStruct((nb * seq_len, DN_QKV), BF16),
        scratch_shapes=[pltpu.VMEM((seq_len + 2 * DN_HALO, DN_CB), BF16)],
        compiler_params=_cp(("parallel", "parallel")),
        name="gdn_pre",
    )(proj, conv_w)


def _split_bf16(x, n):
    parts = []
    r = x
    for _ in range(n):
        p = r.astype(BF16)
        parts.append(p)
        r = r - p.astype(F32)
    return parts


def _mm_exact_lhs(a_bf, b):
    return sum(_dot(a_bf, p) for p in _split_bf16(b, 3))


def _softplus(x):
    return jnp.maximum(x, 0.0) + jnp.log(1.0 + jnp.exp(-jnp.abs(x)))


def _gdn_body(q_ref, k_ref, v_ref, gt_ref, ga_ref, gb_ref, s0_ref, o_ref, sl_ref,
              s_scr, vn_scr, a_scr, t_scr, m_scr, attn_scr, rhs_scr, u_scr, w_scr, qg_scr, kg_scr, *, direction):
    t = pl.program_id(1)
    nt = pl.num_programs(1)
    tile = GDN_TILE
    c_len = DN_CHUNK
    nchunk = tile // c_len
    heads = range(DN_HEADS)

    @pl.when(t == 0)
    def _():
        s_scr[...] = s0_ref[...]
        vn_scr[...] = jnp.zeros_like(vn_scr)

    ri = lax.broadcasted_iota(jnp.int32, (tile, tile), 0)
    ci = lax.broadcasted_iota(jnp.int32, (tile, tile), 1)
    same = (ri // c_len) == (ci // c_len)
    if direction == 0:
        incl = same & (ci <= ri)
        strict = same & (ci < ri)
    else:
        incl = same & (ci >= ri)
        strict = same & (ci > ri)
    tri_bf = jnp.where(incl, 1.0, 0.0).astype(BF16)
    blk_bf = jnp.where(same, 1.0, 0.0).astype(BF16)
    nlev = int(math.log2(c_len))

    def off_mask(lvl, rs=slice(0, tile)):
        nrow = rs.stop - rs.start
        r = lax.broadcasted_iota(jnp.int32, (nrow, tile), 0) + rs.start
        c = lax.broadcasted_iota(jnp.int32, (nrow, tile), 1)
        late, early = (r, c) if direction == 0 else (c, r)
        return (((late >> lvl) & 1) == 1) & (((early >> lvl) & 1) == 0) & ((r >> (lvl + 1)) == (c >> (lvl + 1)))

    def active_blocks(lvl):
        hb = 1 << lvl
        return [slice(rb * hb, (rb + 1) * hb) for rb in range(tile // hb) if rb % 2 == 1 - direction]

    gt = gt_ref[...].astype(F32)
    g_all = ga_ref[...] * _softplus(gt + gb_ref[...])
    beta_all = _sigmoid(gt)
    gc_all = _mm_exact_lhs(tri_bf, g_all)
    gl_all = _mm_exact_lhs(blk_bf, g_all)
    gc_t = jnp.transpose(gc_all)
    chunk_order = range(nchunk) if direction == 0 else range(nchunk - 1, -1, -1)
    eye = jnp.where(ri == ci, 1.0, 0.0)

    def lane(h):
        return direction * DN_HEADS + h

    def hsl(h):
        return slice(h * DN_DK, (h + 1) * DN_DK)

    for h in heads:
        gcol = gc_all[:, lane(h):lane(h) + 1]
        grow = gc_t[lane(h):lane(h) + 1, :]
        glcol = gl_all[:, lane(h):lane(h) + 1]
        bcol = beta_all[:, 2 * DN_HEADS + lane(h):2 * DN_HEADS + lane(h) + 1]
        q_bf, k_bf = q_ref[:, hsl(h)], k_ref[:, hsl(h)]
        qh, kh, vh = q_bf.astype(F32), k_bf.astype(F32), v_ref[:, hsl(h)].astype(F32)
        decay = jnp.exp(jnp.where(incl, gcol - grow, NEG_BIG))
        a_mat = jnp.where(strict, bcol * _dot_nt(k_bf, k_bf) * decay, 0.0)
        a_scr[h] = a_mat
        t_scr[h] = (eye - jnp.where(off_mask(0), a_mat, 0.0)).astype(BF16)
        attn_scr[h] = (_dot_nt(q_bf, k_bf) * decay).astype(BF16)
        egc = jnp.exp(gcol)
        rhs_scr[h] = jnp.concatenate([vh * bcol, kh * (bcol * egc)], axis=-1).astype(BF16)
        qg_scr[h] = (qh * egc).astype(BF16)
        kg_scr[h] = (kh * jnp.exp(glcol - gcol)).astype(BF16)

    for lvl in range(1, nlev):
        hb = 1 << lvl
        if hb == SUBLANES:
            blocks = active_blocks(lvl)
            zero = jnp.zeros((hb, tile), F32)

            def scatter(act):
                pieces = []
                for i in range(len(blocks)):
                    p = act[i * hb:(i + 1) * hb, :]
                    pieces += [zero, p] if direction == 0 else [p, zero]
                return jnp.concatenate(pieces, axis=0)

            for h in heads:
                lhs = jnp.concatenate([jnp.where(off_mask(lvl, rs), a_scr[h, rs, :], 0.0) for rs in blocks], axis=0)
                m_scr[h] = scatter(_dot(lhs.astype(BF16), t_scr[h])).astype(BF16)
            for h in heads:
                t_full = t_scr[h].astype(F32)
                t_act = jnp.concatenate([t_full[rs, :] for rs in blocks], axis=0)
                t_scr[h] = (t_full - scatter(_dot(t_act.astype(BF16), m_scr[h]))).astype(BF16)
            continue
        if hb < SUBLANES:
            for h in heads:
                m_scr[h] = _dot(jnp.where(off_mask(lvl), a_scr[h], 0.0).astype(BF16), t_scr[h]).astype(BF16)
            for h in heads:
                t_scr[h] = t_scr[h] - _dot(t_scr[h], m_scr[h]).astype(BF16)
            continue
        blocks = active_blocks(lvl)
        for h in heads:
            lhs = jnp.concatenate([jnp.where(off_mask(lvl, rs), a_scr[h, rs, :], 0.0) for rs in blocks], axis=0)
            m_act = _dot(lhs.astype(BF16), t_scr[h])
            zero = jnp.zeros((hb, tile), F32)
            pieces = [zero] * (tile // hb)
            for i, rs in enumerate(blocks):
                pieces[rs.start // hb] = m_act[i * hb:(i + 1) * hb, :]
            m_scr[h] = jnp.concatenate(pieces, axis=0).astype(BF16)
        for h in heads:
            t_act = jnp.concatenate([t_scr[h, rs, :].astype(F32) for rs in blocks], axis=0)
            upd = _dot(t_act.astype(BF16), m_scr[h])
            for i, rs in enumerate(blocks):
                t_scr[h, rs, :] = t_scr[h, rs, :] - upd[i * hb:(i + 1) * hb, :].astype(BF16)

    for h in heads:
        x = _dot(t_scr[h], rhs_scr[h])
        u_scr[h] = x[:, :DN_DV]
        w_scr[h] = x[:, DN_DV:].astype(BF16)

    for c in chunk_order:
        rows = slice(c * c_len, (c + 1) * c_len)
        for h in heads:
            s = s_scr[h]
            s_bf = s.astype(BF16)
            v_new = u_scr[h, rows, :] - _dot(w_scr[h, rows, :], s_bf)
            vn_scr[h, rows, :] = v_new.astype(BF16)
            o_ref[rows, hsl(h)] = (_dot(qg_scr[h, rows, :], s_bf)
                                   + _dot(attn_scr[h, rows, :], vn_scr[h])).astype(o_ref.dtype)
            gl_c = gl_all[c * c_len:c * c_len + 1, lane(h):lane(h) + 1]
            s_scr[h] = s * jnp.exp(gl_c) + _dot_tn(kg_scr[h, rows, :], v_new.astype(BF16))

    @pl.when(t == nt - 1)
    def _():
        sl_ref[...] = s_scr[...]


def _gdn_scan(qkv, proj, ga_row, gb_row, s0, *, nb, seq_len, direction):
    nt = seq_len // GDN_TILE
    tile = GDN_TILE
    nq = D_MODEL // D_MODEL

    def row(b, t):
        return b * nt + (t if direction == 0 else nt - 1 - t)

    return pl.pallas_call(
        functools.partial(_gdn_body, direction=direction),
        grid=(nb, nt),
        in_specs=[
            pl.BlockSpec((tile, D_MODEL), lambda b, t: (row(b, t), 0)),
            pl.BlockSpec((tile, D_MODEL), lambda b, t: (row(b, t), nq)),
            pl.BlockSpec((tile, D_MODEL), lambda b, t: (row(b, t), 2 * nq)),
            pl.BlockSpec((tile, LANES), lambda b, t: (row(b, t), DN_GATE_BLK)),
            pl.BlockSpec((1, LANES), lambda b, t: (0, 0)),
            pl.BlockSpec((1, LANES), lambda b, t: (0, 0)),
            pl.BlockSpec((None, DN_HEADS, DN_DK, DN_DV), lambda b, t: (b, 0, 0, 0)),
        ],
        out_specs=[
            pl.BlockSpec((tile, D_MODEL), lambda b, t: (row(b, t), 0)),
            pl.BlockSpec((None, DN_HEADS, DN_DK, DN_DV), lambda b, t: (b, 0, 0, 0)),
        ],
        out_shape=[
            jax.ShapeDtypeStruct((nb * seq_len, D_MODEL), BF16),
            jax.ShapeDtypeStruct((nb, DN_HEADS, DN_DK, DN_DV), F32),
        ],
        scratch_shapes=[
            pltpu.VMEM((DN_HEADS, DN_DK, DN_DV), F32),
            pltpu.VMEM((DN_HEADS, tile, DN_DV), BF16),
            pltpu.VMEM((DN_HEADS, tile, tile), F32),
            pltpu.VMEM((DN_HEADS, tile, tile), BF16),
            pltpu.VMEM((DN_HEADS, tile, tile), BF16),
            pltpu.VMEM((DN_HEADS, tile, tile), BF16),
            pltpu.VMEM((DN_HEADS, tile, 2 * DN_DV), BF16),
            pltpu.VMEM((DN_HEADS, tile, DN_DV), F32),
            pltpu.VMEM((DN_HEADS, tile, DN_DK), BF16),
            pltpu.VMEM((DN_HEADS, tile, DN_DK), BF16),
            pltpu.VMEM((DN_HEADS, tile, DN_DK), BF16),
        ],
        compiler_params=_cp(("parallel", "arbitrary")),
        name="gdn_scan_fwd" if direction == 0 else "gdn_scan_bwd",
    )(qkv, qkv, qkv, proj, ga_row, gb_row, s0)


def _odd_out_body(x_ref, of_ref, ob_ref, z_ref, ng_ref, wo_ref, gt_ref, o_ref):
    o = of_ref[...].astype(F32) + ob_ref[...].astype(F32)
    segs = []
    for h in range(DN_HEADS):
        seg = o[:, h * DN_DV:(h + 1) * DN_DV]
        segs.append(seg * lax.rsqrt(jnp.mean(seg * seg, axis=-1, keepdims=True) + EPS))
    y = jnp.concatenate(segs, axis=-1) * ng_ref[...] * _silu(z_ref[...].astype(F32))
    o_ref[...] = x_ref[...] + gt_ref[...] * _dot(y.astype(BF16), wo_ref[...])


def _odd_out(x, o_f, o_b, proj, ng_row, wo_bf, modt_l, seq_fn, *, tm):
    t = x.shape[0]
    zblk = DN_QKV // D_MODEL
    return pl.pallas_call(
        _odd_out_body,
        grid=(t // tm,),
        in_specs=[
            pl.BlockSpec((tm, D_MODEL), lambda i: (i, 0)),
            pl.BlockSpec((tm, D_MODEL), lambda i: (i, 0)),
            pl.BlockSpec((tm, D_MODEL), lambda i: (i, 0)),
            pl.BlockSpec((tm, D_MODEL), lambda i: (i, zblk)),
            pl.BlockSpec((1, D_MODEL), lambda i: (0, 0)),
            pl.BlockSpec((D_MODEL, D_MODEL), lambda i: (0, 0)),
            _mod_spec(2, seq_fn, 1),
        ],
        out_specs=pl.BlockSpec((tm, D_MODEL), lambda i: (i, 0)),
        out_shape=jax.ShapeDtypeStruct((t, D_MODEL), F32),
        compiler_params=_cp(("parallel",)),
        name="odd_out",
    )(x, o_f, o_b, proj, ng_row, wo_bf, modt_l)


ROW_TILE = 1024


def _seq_fn(slot0, seq_len, tm):
    if slot0 == 0:
        return lambda i: 0
    return lambda i: slot0 + (i * tm) // seq_len


def kernel(x_prompt, x_sample, c, cache_na_k, cache_na_v, state_s5_re, state_s5_im, state_dn, c_ctx, norm_mix_g, norm_ff_g, w_mod, b_mod, w_ff1, w_ff2, w_in_e, w_out_e, s5_lam_re, s5_lam_im, s5_log_dt, s5_b_re, s5_b_im, s5_c_re, s5_c_im, s5_d, s5_w_glu, s5_b_glu, na_rpb, w_in_o, dn_conv_w, dn_a_log, dn_dt_bias, dn_norm_g, w_out_o, final_norm_g):
    nbp, lp = x_prompt.shape[:2]
    nbs, ls = x_sample.shape[:2]
    assert nbp % S5_BG == 0 and nbs % S5_BG == 0 and 1 + nbs <= 16
    streams = [
        dict(x=x_prompt.reshape(nbp * lp, D_MODEL), nb=nbp, L=lp, slot0=0),
        dict(x=x_sample.reshape(nbs * ls, D_MODEL), nb=nbs, L=ls, slot0=1),
    ]
    cond16 = jnp.zeros((16, D_MODEL), F32).at[0].set(c_ctx).at[1:1 + nbs].set(c)
    mod = _modulation(cond16, w_mod, b_mod[:, None, :])
    modt = jnp.transpose(mod.reshape(DEPTH, 16, 6, D_MODEL), (0, 2, 1, 3))[:, :, :, None, :]

    new_k, new_v, new_sre, new_sim, new_dn = [], [], [], [], []
    for l in range(DEPTH):
        e = l // 2
        if l % 2 == 0:
            w_in = w_in_e[e].astype(BF16)
            w_out = w_out_e[e].astype(BF16)
            w_glu = s5_w_glu[e].astype(BF16)
            ops = _s5_operators(s5_lam_re[e], s5_lam_im[e], s5_log_dt[e], s5_b_re[e], s5_b_im[e],
                                s5_c_re[e], s5_c_im[e])
            bias = _na_bias_tables(na_rpb[e], ls // GRID_W)
        else:
            w_in = jnp.pad(w_in_o[e], ((0, 0), (0, DN_PROJ_PAD - w_in_o.shape[-1]))).astype(BF16)
            w_out = w_out_o[e].astype(BF16)
            ga_row = jnp.zeros((1, LANES), F32).at[0, :2 * DN_HEADS].set(-jnp.exp(dn_a_log[e]).reshape(-1))
            gb_row = jnp.zeros((1, LANES), F32).at[0, :2 * DN_HEADS].set(dn_dt_bias[e].reshape(-1))
            ng_row = jnp.tile(dn_norm_g[e], DN_HEADS)[None]
        w1, w2 = w_ff1[l], w_ff2[l]
        for st in streams:
            x, nb, sl = st["x"], st["nb"], st["L"]
            seq = _seq_fn(st["slot0"], sl, ROW_TILE)
            is_ctx = st["slot0"] == 0
            if l % 2 == 0:
                proj = _norm_mod_matmul(x, norm_mix_g[l][None], modt[l], w_in, seq, tm=ROW_TILE, name="in_proj_even")
                nbg = nb // S5_BG
                if is_ctx:
                    x0 = jnp.zeros((nbg, S5_JB, S5_BG, 4 * S5_SW), F32)
                else:
                    x0 = _s5_pack_state(state_s5_re[:, e], state_s5_im[:, e])
                y, fin = _s5_mix(proj, ops, x0, nbg=nbg, seq_len=sl)
                if is_ctx:
                    bo = _context_attention(proj, nb=nb, seq_len=sl)
                    kv = proj.reshape(nb, sl, -1)
                    new_k.append(kv[:, :, MIX_A + MIX_B:MIX_A + 2 * MIX_B].reshape(nb, sl, NA_HEADS, NA_HD))
                    new_v.append(kv[:, :, MIX_A + 2 * MIX_B:].reshape(nb, sl, NA_HEADS, NA_HD))
                    sre, sim = _s5_unpack_state(fin)
                    new_sre.append(sre)
                    new_sim.append(sim)
                else:
                    past = cache_na_k.shape[2]
                    ck = cache_na_k[:, e].reshape(nb, past, MIX_B)
                    cv = cache_na_v[:, e].reshape(nb, past, MIX_B)
                    bo = _neighbourhood_attention(proj, ck, cv, bias, nb=nb, seq_len=sl)
                x = _even_out(x, y.reshape(nb * sl, MIX_A), proj, bo, s5_d[e][None], w_glu, s5_b_glu[e][None],
                              w_out, modt[l], seq, tm=ROW_TILE)
            else:
                proj = _norm_mod_matmul(x, norm_mix_g[l][None], modt[l], w_in, seq, tm=ROW_TILE, name="in_proj_odd",
                                        out_dtype=BF16)
                qkv = _gdn_pre(proj, dn_conv_w[e], nb=nb, seq_len=sl)
                if is_ctx:
                    s0 = jnp.zeros((nb, 2, DN_HEADS, DN_DK, DN_DV), F32)
                else:
                    s0 = state_dn[:, e]
                o_f, s_f = _gdn_scan(qkv, proj, ga_row, gb_row, s0[:, 0], nb=nb, seq_len=sl, direction=0)
                o_b, s_b = _gdn_scan(qkv, proj, ga_row, gb_row, s0[:, 1], nb=nb, seq_len=sl, direction=1)
                if is_ctx:
                    new_dn.append(jnp.stack([s_f, s_b], axis=1))
                x = _odd_out(x, o_f, o_b, proj, ng_row, w_out, modt[l], seq, tm=ROW_TILE)
            st["x"] = _mlp(x, norm_ff_g[l][None], modt[l], w1, w2, seq, final_norm_g[None], tm=ROW_TILE, tf=1024,
                           final_norm=(l == DEPTH - 1))

    y_prompt = streams[0]["x"].reshape(nbp, lp, D_MODEL)
    y_sample = streams[1]["x"].reshape(nbs, ls, D_MODEL)
    return (y_prompt, y_sample, jnp.stack(new_k, axis=1), jnp.stack(new_v, axis=1),
            jnp.stack(new_sre, axis=1), jnp.stack(new_sim, axis=1), jnp.stack(new_dn, axis=1))
```

```python
import functools
import math

import jax
import jax.numpy as jnp
from jax import lax
from jax.experimental import pallas as pl
from jax.experimental.pallas import tpu as pltpu

F32 = jnp.float32
BF16 = jnp.bfloat16

D_MODEL = 1024
DEPTH = 2
GRID_W = 64
MIX_A = D_MODEL // 2
A_GROUP = 16
A_GROUPS = MIX_A // A_GROUP
S5_N = 64
MIX_B = D_MODEL - MIX_A
NA_HEADS = 8
NA_HD = MIX_B // NA_HEADS
NA_KH = 8
NA_KW = 16
DN_HEADS = 8
DN_DK = D_MODEL // DN_HEADS
DN_DV = D_MODEL // DN_HEADS
DN_CONV = 5
DN_CHUNK = 64
D_FF = 4 * D_MODEL
EPS = 1e-6

LANES = 128
SUBLANES = 8
VMEM_LIMIT = 56 * 1024 * 1024
NEG_BIG = -1e30

S5_T = 16
S5_BG = SUBLANES
S5_JB = MIX_A // LANES
S5_GPB = LANES // A_GROUP
S5_SW = S5_GPB * S5_N
GDN_TILE = 256


def _cp(sem, vmem=VMEM_LIMIT):
    return pltpu.CompilerParams(dimension_semantics=sem, vmem_limit_bytes=vmem)


def _rms(x, g):
    return x * lax.rsqrt(jnp.mean(x * x, axis=-1, keepdims=True) + EPS) * g


def _sigmoid(x):
    return 1.0 / (1.0 + jnp.exp(-x))


def _silu(x):
    return x * _sigmoid(x)


def _gelu_tanh(x):
    return 0.5 * x * (1.0 + jnp.tanh(math.sqrt(2.0 / math.pi) * (x + 0.044715 * (x * x * x))))


def _dot(a, b):
    return jnp.dot(a, b, preferred_element_type=F32)


def _dot_nt(a, b):
    return lax.dot_general(a, b, (((1,), (1,)), ((), ())), preferred_element_type=F32)


def _dot_tn(a, b):
    return lax.dot_general(a, b, (((0,), (0,)), ((), ())), preferred_element_type=F32)


def _mod_body(c_ref, w_ref, b_ref, o_ref):
    c = c_ref[...]
    o_ref[...] = _dot(_silu(c).astype(BF16), w_ref[...].astype(BF16)) + b_ref[...]


def _modulation(cond16, w_mod_bf, b_mod):
    tn = 1536
    n = 6 * D_MODEL
    return pl.pallas_call(
        _mod_body,
        grid=(DEPTH, n // tn),
        in_specs=[
            pl.BlockSpec((16, D_MODEL), lambda l, j: (0, 0)),
            pl.BlockSpec((None, D_MODEL, tn), lambda l, j: (l, 0, j)),
            pl.BlockSpec((None, 1, tn), lambda l, j: (l, 0, j)),
        ],
        out_specs=pl.BlockSpec((None, 16, tn), lambda l, j: (l, 0, j)),
        out_shape=jax.ShapeDtypeStruct((DEPTH, 16, n), F32),
        compiler_params=_cp(("parallel", "parallel")),
        name="modulation",
    )(cond16, w_mod_bf, b_mod)


def _mod_spec(which, seq_fn, ngrid):
    if ngrid == 1:
        return pl.BlockSpec((None, None, 1, D_MODEL), lambda i: (which, seq_fn(i), 0, 0))
    return pl.BlockSpec((None, None, 1, D_MODEL), lambda i, j: (which, seq_fn(i), 0, 0))


def _nmm_body(x_ref, g_ref, sh_ref, sc_ref, w_ref, o_ref):
    h = _rms(x_ref[...], g_ref[...]) * (1.0 + sc_ref[...]) + sh_ref[...]
    o_ref[...] = _dot(h.astype(BF16), w_ref[...]).astype(o_ref.dtype)


def _norm_mod_matmul(x, g_row, modt_l, w_bf, seq_fn, *, tm, name, out_dtype=F32):
    t, n = x.shape[0], w_bf.shape[1]
    return pl.pallas_call(
        _nmm_body,
        grid=(t // tm,),
        in_specs=[
            pl.BlockSpec((tm, D_MODEL), lambda i: (i, 0)),
            pl.BlockSpec((1, D_MODEL), lambda i: (0, 0)),
            _mod_spec(0, seq_fn, 1),
            _mod_spec(1, seq_fn, 1),
            pl.BlockSpec((D_MODEL, n), lambda i: (0, 0)),
        ],
        out_specs=pl.BlockSpec((tm, n), lambda i: (i, 0)),
        out_shape=jax.ShapeDtypeStruct((t, n), out_dtype),
        compiler_params=_cp(("parallel",)),
        name=name,
    )(x, g_row, modt_l, modt_l, w_bf)


def _mlp_body(x_ref, g_ref, sh_ref, sc_ref, gt_ref, w1_ref, w2_ref, fg_ref, o_ref, h_ref, acc_ref, *, final_norm):
    k = pl.program_id(1)

    @pl.when(k == 0)
    def _():
        h = _rms(x_ref[...], g_ref[...]) * (1.0 + sc_ref[...]) + sh_ref[...]
        h_ref[...] = h.astype(BF16)
        acc_ref[...] = jnp.zeros_like(acc_ref)

    a = jnp.maximum(_dot(h_ref[...], w1_ref[...].astype(BF16)), 0.0)
    acc_ref[...] += _dot((a * a).astype(BF16), w2_ref[...].astype(BF16))

    @pl.when(k == pl.num_programs(1) - 1)
    def _():
        y = x_ref[...] + gt_ref[...] * acc_ref[...]
        o_ref[...] = _rms(y, fg_ref[...]) if final_norm else y


def _mlp(x, g_row, modt_l, w1_bf, w2_bf, seq_fn, fg_row, *, tm, tf, final_norm):
    t = x.shape[0]
    return pl.pallas_call(
        functools.partial(_mlp_body, final_norm=final_norm),
        grid=(t // tm, D_FF // tf),
        in_specs=[
            pl.BlockSpec((tm, D_MODEL), lambda i, k: (i, 0)),
            pl.BlockSpec((1, D_MODEL), lambda i, k: (0, 0)),
            _mod_spec(3, seq_fn, 2),
            _mod_spec(4, seq_fn, 2),
            _mod_spec(5, seq_fn, 2),
            pl.BlockSpec((D_MODEL, tf), lambda i, k: (0, k)),
            pl.BlockSpec((tf, D_MODEL), lambda i, k: (k, 0)),
            pl.BlockSpec((1, D_MODEL), lambda i, k: (0, 0)),
        ],
        out_specs=pl.BlockSpec((tm, D_MODEL), lambda i, k: (i, 0)),
        out_shape=jax.ShapeDtypeStruct((t, D_MODEL), F32),
        scratch_shapes=[pltpu.VMEM((tm, D_MODEL), BF16), pltpu.VMEM((tm, D_MODEL), F32)],
        compiler_params=_cp(("parallel", "arbitrary")),
        name="mlp",
    )(x, g_row, modt_l, modt_l, modt_l, w1_bf, w2_bf, fg_row)


def _s5_operators(lam_re, lam_im, log_dt, b_re, b_im, c_re, c_im):
    hp = lax.Precision.HIGHEST
    t = S5_T
    dt = jnp.exp(log_dt)[..., None]
    ar, ai = lam_re * dt, lam_im * dt
    er = jnp.exp(ar)
    lbr, lbi = er * jnp.cos(ai), er * jnp.sin(ai)
    den = lam_re * lam_re + lam_im * lam_im
    fr = ((lbr - 1.0) * lam_re + lbi * lam_im) / den
    fi = (lbi * lam_re - (lbr - 1.0) * lam_im) / den
    bbr = fr[..., None] * b_re - fi[..., None] * b_im
    bbi = fr[..., None] * b_im + fi[..., None] * b_re
    k = jnp.arange(t + 1, dtype=F32)[:, None, None, None]
    pr = jnp.exp(k * ar[None]) * jnp.cos(k * ai[None])
    pi = jnp.exp(k * ar[None]) * jnp.sin(k * ai[None])
    pbr = pr[..., None] * bbr[None] - pi[..., None] * bbi[None]
    pbi = pr[..., None] * bbi[None] + pi[..., None] * bbr[None]
    pcr = c_re[None] * pr[:, :, :, None, :] - c_im[None] * pi[:, :, :, None, :]
    pci = c_re[None] * pi[:, :, :, None, :] + c_im[None] * pr[:, :, :, None, :]
    kern = (jnp.einsum("dgpn,kdgnq->kdgpq", c_re, pbr[:t], precision=hp)
            - jnp.einsum("dgpn,kdgnq->kdgpq", c_im, pbi[:t], precision=hp))
    import numpy as np

    jb, gpb = S5_JB, S5_GPB
    kw, sw4 = t * LANES, 4 * S5_SW

    def expand(compact, col_src, row_group, col_group):
        onehot = jnp.asarray(col_src[None, :] == np.arange(compact.shape[-1])[:, None], BF16)
        full = jnp.dot(compact.astype(BF16), onehot, preferred_element_type=BF16)
        return jnp.where(jnp.asarray(row_group[:, None] == col_group[None, :])[None], full, 0.0)

    state_rows = np.arange(sw4)
    chunk_rows = np.arange(kw)
    state_grp, chunk_grp = (state_rows // S5_N) % gpb, (chunk_rows // A_GROUP) % gpb
    state_src = (state_rows // S5_SW) * S5_N + state_rows % S5_N
    chunk_src = (chunk_rows // LANES) * A_GROUP + chunk_rows % A_GROUP

    sidx = jnp.arange(t)
    secs = jnp.stack([pbr[t - 1 - sidx, 0], pbi[t - 1 - sidx, 0], pbr[sidx, 1], pbi[sidx, 1]], axis=0)
    secs = secs.reshape(4, t, jb, gpb, S5_N, A_GROUP)
    bst = expand(jnp.transpose(secs, (2, 1, 3, 5, 0, 4)).reshape(jb, kw, 4 * S5_N), state_src, chunk_grp, state_grp)
    secs = jnp.stack([pcr[sidx + 1, 0], -pci[sidx + 1, 0], pcr[t - sidx, 1], -pci[t - sidx, 1]], axis=0)
    secs = secs.reshape(4, t, jb, gpb, A_GROUP, S5_N)
    cst = expand(jnp.transpose(secs, (2, 0, 3, 5, 1, 4)).reshape(jb, sw4, t * A_GROUP), chunk_src, state_grp, chunk_grp)
    noff = 2 * t - 1
    kd = jnp.concatenate([kern[1:, 1][::-1], (kern[0, 0] + kern[0, 1])[None], kern[1:, 0]], axis=0)
    kd = jnp.transpose(kd.reshape(noff, jb, gpb, A_GROUP, A_GROUP), (1, 0, 2, 4, 3))
    lane_rows = np.arange(noff * LANES)
    blocks = expand(kd.reshape(jb, noff * LANES, A_GROUP), np.arange(LANES) % A_GROUP,
                    (lane_rows // A_GROUP) % gpb, np.arange(LANES) // A_GROUP)
    off = np.arange(t)[None, :] - np.arange(t)[:, None] + t - 1
    wt = jnp.take(blocks.reshape(jb, noff, LANES, LANES), jnp.asarray(off), axis=1)
    wt = jnp.transpose(wt, (0, 1, 3, 2, 4)).reshape(jb, kw, kw)
    acoef = jnp.stack([pr[t, 0], pi[t, 0], pr[t, 1], pi[t, 1]], axis=0)
    acoef = acoef.reshape(4, jb, S5_SW).transpose(1, 0, 2).reshape(jb, 1, 4 * S5_SW)
    acoef = jnp.broadcast_to(acoef, (jb, S5_BG, 4 * S5_SW))
    return bst.astype(BF16), cst.astype(BF16), wt.astype(BF16), acoef


def _s5_in_body(u_ref, bst_ref, lhs_ref, s_ref, *, seq_len, row_tile):
    nc = seq_len // S5_T

    @pl.when(pl.program_id(2) == 0)
    def _():
        def build(cp, carry):
            base = cp * (2 * S5_T)
            for s in range(S5_T):
                a = u_ref[pl.ds(base + s, S5_BG, stride=seq_len), :]
                b = u_ref[pl.ds(base + S5_T + s, S5_BG, stride=seq_len), :]
                lhs_ref[pl.ds(pl.multiple_of(cp * 16, 16), 16), s * LANES:(s + 1) * LANES] = (
                    jnp.concatenate([a, b], axis=0).astype(BF16))
            return carry

        lax.fori_loop(0, nc // 2, build, 0)

    def mm(r, carry):
        rows = pl.ds(pl.multiple_of(r * row_tile, row_tile), row_tile)
        s_ref[rows, :] = _dot(lhs_ref[rows, :], bst_ref[...])
        return carry

    lax.fori_loop(0, (nc * S5_BG) // row_tile, mm, 0)


def _s5_state_in(proj2d, bst, *, nbg, seq_len):
    nc = seq_len // S5_T
    rows = nc * S5_BG
    kw = S5_T * LANES
    half = 2 * S5_SW
    row_tile = min(rows, 256)
    return pl.pallas_call(
        functools.partial(_s5_in_body, seq_len=seq_len, row_tile=row_tile),
        grid=(nbg, S5_JB, 2),
        in_specs=[
            pl.BlockSpec((S5_BG * seq_len, LANES), lambda g, j, n: (g, j)),
            pl.BlockSpec((None, kw, half), lambda g, j, n: (j, 0, n)),
        ],
        out_specs=[
            pl.BlockSpec((None, None, rows, kw), lambda g, j, n: (g, j, 0, 0)),
            pl.BlockSpec((None, None, rows, half), lambda g, j, n: (g, j, 0, n)),
        ],
        out_shape=[
            jax.ShapeDtypeStruct((nbg, S5_JB, rows, kw), BF16),
            jax.ShapeDtypeStruct((nbg, S5_JB, rows, 4 * S5_SW), F32),
        ],
        compiler_params=_cp(("parallel", "parallel", "arbitrary")),
        name="s5_state_in",
    )(proj2d, bst)


def _s5_scan_body(s_ref, a_ref, x0_ref, x_ref, fin_ref, *, nc):
    w = S5_SW
    afr, afi = a_ref[:, 0:w], a_ref[:, w:2 * w]
    abr, abi = a_ref[:, 2 * w:3 * w], a_ref[:, 3 * w:4 * w]

    def step(c, carry):
        xfr, xfi, xbr, xbi = carry
        rf = pl.ds(pl.multiple_of(c * S5_BG, S5_BG), S5_BG)
        rb = pl.ds(pl.multiple_of((nc - 1 - c) * S5_BG, S5_BG), S5_BG)
        x_ref[rf, 0:w] = xfr
        x_ref[rf, w:2 * w] = xfi
        x_ref[rb, 2 * w:3 * w] = xbr
        x_ref[rb, 3 * w:4 * w] = xbi
        nfr = afr * xfr - afi * xfi + s_ref[rf, 0:w]
        nfi = afr * xfi + afi * xfr + s_ref[rf, w:2 * w]
        nbr = abr * xbr - abi * xbi + s_ref[rb, 2 * w:3 * w]
        nbi = abr * xbi + abi * xbr + s_ref[rb, 3 * w:4 * w]
        return nfr, nfi, nbr, nbi

    init = (x0_ref[:, 0:w], x0_ref[:, w:2 * w], x0_ref[:, 2 * w:3 * w], x0_ref[:, 3 * w:4 * w])
    xfr, xfi, xbr, xbi = lax.fori_loop(0, nc, step, init)
    fin_ref[:, 0:w] = xfr
    fin_ref[:, w:2 * w] = xfi
    fin_ref[:, 2 * w:3 * w] = xbr
    fin_ref[:, 3 * w:4 * w] = xbi


def _s5_scan(s_all, acoef, x0, *, nbg, seq_len):
    nc = seq_len // S5_T
    rows = nc * S5_BG
    sw4 = 4 * S5_SW
    return pl.pallas_call(
        functools.partial(_s5_scan_body, nc=nc),
        grid=(nbg, S5_JB),
        in_specs=[
            pl.BlockSpec((None, None, rows, sw4), lambda g, j: (g, j, 0, 0)),
            pl.BlockSpec((None, S5_BG, sw4), lambda g, j: (j, 0, 0)),
            pl.BlockSpec((None, None, S5_BG, sw4), lambda g, j: (g, j, 0, 0)),
        ],
        out_specs=[
            pl.BlockSpec((None, None, rows, sw4), lambda g, j: (g, j, 0, 0)),
            pl.BlockSpec((None, None, S5_BG, sw4), lambda g, j: (g, j, 0, 0)),
        ],
        out_shape=[
            jax.ShapeDtypeStruct((nbg, S5_JB, rows, sw4), F32),
            jax.ShapeDtypeStruct((nbg, S5_JB, S5_BG, sw4), F32),
        ],
        compiler_params=_cp(("parallel", "parallel")),
        name="s5_scan",
    )(s_all, acoef, x0)


def _s5_out_body(lhs_ref, x_ref, wt_ref, cst_ref, y_ref, acc_ref, *, chunks):
    acc_ref[...] = _dot(lhs_ref[...], wt_ref[...]) + _dot(x_ref[...].astype(BF16), cst_ref[...])

    def unchunk(c, carry):
        rows = pl.ds(pl.multiple_of(c * S5_BG, S5_BG), S5_BG)
        for t in range(S5_T):
            y_ref[:, c * S5_T + t, :] = acc_ref[rows, t * LANES:(t + 1) * LANES]
        return carry

    lax.fori_loop(0, chunks, unchunk, 0)


def _s5_output(lhs, x_all, wt, cst, *, nbg, seq_len):
    nc = seq_len // S5_T
    chunks = min(nc, 32)
    tr = chunks * S5_BG
    kw = S5_T * LANES
    sw4 = 4 * S5_SW
    return pl.pallas_call(
        functools.partial(_s5_out_body, chunks=chunks),
        grid=(nbg, S5_JB, nc // chunks),
        in_specs=[
            pl.BlockSpec((None, None, tr, kw), lambda g, j, r: (g, j, r, 0)),
            pl.BlockSpec((None, None, tr, sw4), lambda g, j, r: (g, j, r, 0)),
            pl.BlockSpec((None, kw, kw), lambda g, j, r: (j, 0, 0)),
            pl.BlockSpec((None, sw4, kw), lambda g, j, r: (j, 0, 0)),
        ],
        out_specs=pl.BlockSpec((S5_BG, chunks * S5_T, LANES), lambda g, j, r: (g, r, j)),
        out_shape=jax.ShapeDtypeStruct((nbg * S5_BG, seq_len, MIX_A), F32),
        scratch_shapes=[pltpu.VMEM((tr, kw), F32)],
        compiler_params=_cp(("parallel", "parallel", "arbitrary")),
        name="s5_output",
    )(lhs, x_all, wt, cst)


def _s5_mix(proj2d, ops, x0, *, nbg, seq_len):
    bst, cst, wt, acoef = ops
    lhs, s_all = _s5_state_in(proj2d, bst, nbg=nbg, seq_len=seq_len)
    x_all, fin = _s5_scan(s_all, acoef, x0, nbg=nbg, seq_len=seq_len)
    y = _s5_output(lhs, x_all, wt, cst, nbg=nbg, seq_len=seq_len)
    return y, fin


def _s5_pack_state(re, im):
    b = re.shape[0]
    x = jnp.stack([re[:, 0], im[:, 0], re[:, 1], im[:, 1]], axis=1)
    x = x.reshape(b // S5_BG, S5_BG, 4, S5_JB, S5_SW)
    return jnp.transpose(x, (0, 3, 1, 2, 4)).reshape(b // S5_BG, S5_JB, S5_BG, 4 * S5_SW)


def _s5_unpack_state(fin):
    nbg = fin.shape[0]
    x = fin.reshape(nbg, S5_JB, S5_BG, 4, S5_SW)
    x = jnp.transpose(x, (0, 2, 3, 1, 4)).reshape(nbg * S5_BG, 4, A_GROUPS, S5_N)
    return jnp.stack([x[:, 0], x[:, 2]], axis=1), jnp.stack([x[:, 1], x[:, 3]], axis=1)


Q_BLK, K_BLK, V_BLK = MIX_A // LANES, (MIX_A + MIX_B) // LANES, (MIX_A + 2 * MIX_B) // LANES


def _pair_softmax_pv(q, parts):
    lane = lax.broadcasted_iota(jnp.int32, (1, LANES), 1)
    outs = []
    for half in range(2):
        sel = (lane < NA_HD) if half == 0 else (lane >= NA_HD)
        qm = jnp.where(sel, q, 0.0).astype(BF16)
        scores = []
        for k, _, bias in parts:
            s = _dot_nt(qm, k)
            if bias is not None:
                s = s + bias[half]
            scores.append(s)
        m = scores[0].max(axis=-1, keepdims=True)
        for s in scores[1:]:
            m = jnp.maximum(m, s.max(axis=-1, keepdims=True))
        l = None
        o = None
        for s, (_, v, _) in zip(scores, parts):
            p = jnp.exp(s - m)
            ls = p.sum(axis=-1, keepdims=True)
            os_ = _dot(p.astype(BF16), v)
            l = ls if l is None else l + ls
            o = os_ if o is None else o + os_
        outs.append(o / l)
    return jnp.where(lane < NA_HD, outs[0], outs[1]).astype(BF16)


def _ctx_attn_body(q_ref, k_ref, v_ref, o_ref):
    q = q_ref[...] * (NA_HD ** -0.5)
    o_ref[...] = _pair_softmax_pv(q, [(k_ref[...].astype(BF16), v_ref[...].astype(BF16), None)])


def _context_attention(proj, *, nb, seq_len):
    npair = NA_HEADS // 2
    return pl.pallas_call(
        _ctx_attn_body,
        grid=(nb, npair),
        in_specs=[
            pl.BlockSpec((seq_len, LANES), lambda b, h: (b, Q_BLK + h)),
            pl.BlockSpec((seq_len, LANES), lambda b, h: (b, K_BLK + h)),
            pl.BlockSpec((seq_len, LANES), lambda b, h: (b, V_BLK + h)),
        ],
        out_specs=pl.BlockSpec((seq_len, LANES), lambda b, h: (b, h)),
        out_shape=jax.ShapeDtypeStruct((nb * seq_len, MIX_B), BF16),
        compiler_params=_cp(("parallel", "parallel")),
        name="context_attention",
    )(proj, proj, proj)


NA_QROWS = 4
NA_WROWS = NA_QROWS + NA_KH


def _na_window_start(g, rows):
    rs = jnp.clip(g * NA_QROWS - NA_KH // 2, 0, rows - NA_KH)
    return jnp.minimum(rs, rows - NA_WROWS)


def _na_bias_tables(rpb, rows):
    import numpy as np

    ngroups = rows // NA_QROWS
    shp = (NA_QROWS * GRID_W, NA_WROWS * GRID_W)
    ndc = 2 * NA_KW - 1
    c, kc = np.meshgrid(np.arange(GRID_W), np.arange(GRID_W), indexing="ij")
    cs = np.clip(c - NA_KW // 2, 0, GRID_W - NA_KW)
    col_ok = (kc >= cs) & (kc < cs + NA_KW)
    col_sel = ((kc - c + NA_KW - 1)[None] == np.arange(ndc)[:, None, None]) & col_ok[None]
    dr_idx, valid = [], []
    for g in (0, 1, ngroups - 1):
        rs0 = int(np.clip(g * NA_QROWS - NA_KH // 2, 0, rows - NA_KH))
        ws = min(rs0, rows - NA_WROWS)
        ri, wi = np.meshgrid(np.arange(NA_QROWS), np.arange(NA_WROWS), indexing="ij")
        r = g * NA_QROWS + ri
        rs = np.clip(r - NA_KH // 2, 0, rows - NA_KH)
        row = ws + wi
        row_ok = (row >= rs) & (row < rs + NA_KH)
        dr_idx.append(np.clip(row - r + NA_KH - 1, 0, 2 * NA_KH - 2))
        valid.append((row_ok[:, None, :, None] & col_ok[None, :, None, :]).reshape(shp))
    rows_sel = rpb[:, np.stack(dr_idx)]
    rows_sel = rows_sel.reshape(NA_HEADS, 3, NA_QROWS, NA_WROWS // 2, 2 * ndc)
    pair_sel = np.zeros((2, ndc, GRID_W, 2, GRID_W), np.float32)
    for wlo in range(2):
        pair_sel[wlo, :, :, wlo, :] = col_sel
    pair_sel = pair_sel.reshape(2 * ndc, GRID_W, 2 * GRID_W)
    t = jnp.einsum("hgrwe,eck->hgrcwk", rows_sel, jnp.asarray(pair_sel), precision=lax.Precision.HIGHEST)
    return jnp.where(jnp.asarray(np.stack(valid))[None], t.reshape(NA_HEADS, 3, *shp), NEG_BIG)


NA_GPS = 4


def _nbr_attn_body(q_ref, k_ref, v_ref, ck_ref, cv_ref, bias_ref, o_ref, *, rows):
    ngroups = rows // NA_QROWS
    qt = NA_QROWS * GRID_W
    ck, cv = ck_ref[...].astype(BF16), cv_ref[...].astype(BF16)
    for u in range(NA_GPS):
        g = pl.program_id(2) * NA_GPS + u
        case = jnp.where(g == 0, 0, jnp.where(g == ngroups - 1, 2, 1))
        ws = _na_window_start(g, rows)
        win = pl.ds(pl.multiple_of(ws * GRID_W, GRID_W), NA_WROWS * GRID_W)
        qrows = slice(u * qt, (u + 1) * qt)
        q = q_ref[qrows, :] * (NA_HD ** -0.5)
        kw = k_ref[win, :].astype(BF16)
        vw = v_ref[win, :].astype(BF16)
        bias = (bias_ref[0, case], bias_ref[1, case])
        o_ref[qrows, :] = _pair_softmax_pv(q, [(kw, vw, bias), (ck, cv, None)])


def _neighbourhood_attention(proj, ck, cv, bias, *, nb, seq_len):
    rows = seq_len // GRID_W
    ngroups = rows // NA_QROWS // NA_GPS
    npair = NA_HEADS // 2
    qt = NA_GPS * NA_QROWS * GRID_W
    past = ck.shape[1]
    return pl.pallas_call(
        functools.partial(_nbr_attn_body, rows=rows),
        grid=(npair, nb, ngroups),
        in_specs=[
            pl.BlockSpec((qt, LANES), lambda h, b, g: (b * ngroups + g, Q_BLK + h)),
            pl.BlockSpec((seq_len, LANES), lambda h, b, g: (b, K_BLK + h)),
            pl.BlockSpec((seq_len, LANES), lambda h, b, g: (b, V_BLK + h)),
            pl.BlockSpec((None, past, LANES), lambda h, b, g: (b, 0, h)),
            pl.BlockSpec((None, past, LANES), lambda h, b, g: (b, 0, h)),
            pl.BlockSpec((2, 3, NA_QROWS * GRID_W, NA_WROWS * GRID_W), lambda h, b, g: (h, 0, 0, 0)),
        ],
        out_specs=pl.BlockSpec((qt, LANES), lambda h, b, g: (b * ngroups + g, h)),
        out_shape=jax.ShapeDtypeStruct((nb * seq_len, MIX_B), BF16),
        compiler_params=_cp(("parallel", "parallel", "parallel")),
        name="neighbourhood_attention",
    )(proj, proj, proj, ck, cv, bias)


def _even_out_body(x_ref, y_ref, u_ref, bo_ref, d_ref, wg_ref, bg_ref, wo_ref, gt_ref, o_ref):
    z = _gelu_tanh(y_ref[...] + d_ref[...] * u_ref[...])
    a = z * _sigmoid(_dot(z.astype(BF16), wg_ref[...]) + bg_ref[...])
    cat = jnp.concatenate([a.astype(BF16), bo_ref[...]], axis=-1)
    o_ref[...] = x_ref[...] + gt_ref[...] * _dot(cat, wo_ref[...])


def _even_out(x, y, proj, bo, d_row, wg_bf, bg_row, wo_bf, modt_l, seq_fn, *, tm):
    t = x.shape[0]
    return pl.pallas_call(
        _even_out_body,
        grid=(t // tm,),
        in_specs=[
            pl.BlockSpec((tm, D_MODEL), lambda i: (i, 0)),
            pl.BlockSpec((tm, MIX_A), lambda i: (i, 0)),
            pl.BlockSpec((tm, MIX_A), lambda i: (i, 0)),
            pl.BlockSpec((tm, MIX_B), lambda i: (i, 0)),
            pl.BlockSpec((1, MIX_A), lambda i: (0, 0)),
            pl.BlockSpec((MIX_A, MIX_A), lambda i: (0, 0)),
            pl.BlockSpec((1, MIX_A), lambda i: (0, 0)),
            pl.BlockSpec((D_MODEL, D_MODEL), lambda i: (0, 0)),
            _mod_spec(2, seq_fn, 1),
        ],
        out_specs=pl.BlockSpec((tm, D_MODEL), lambda i: (i, 0)),
        out_shape=jax.ShapeDtypeStruct((t, D_MODEL), F32),
        compiler_params=_cp(("parallel",)),
        name="even_out",
    )(x, y, proj, bo, d_row, wg_bf, bg_row, wo_bf, modt_l)


DN_QKV = 3 * D_MODEL
DN_PROJ_PAD = 4 * D_MODEL + LANES
DN_GATE_BLK = (4 * D_MODEL) // LANES
DN_CB = 512
DN_HALO = 2 * SUBLANES


def _gdn_pre_body(x_ref, w_ref, o_ref, pad_ref, *, seq_len, rt):
    kind = pl.program_id(1) // (D_MODEL // DN_CB)
    zeros = jnp.zeros((DN_HALO, DN_CB), BF16)
    pad_ref[0:DN_HALO, :] = zeros
    pad_ref[DN_HALO + seq_len:DN_HALO + seq_len + DN_HALO, :] = zeros
    pad_ref[DN_HALO:DN_HALO + seq_len, :] = x_ref[...]
    qscale = jnp.where(kind == 0, DN_DK ** -0.5, 1.0)
    lo = DN_HALO - (DN_CONV - 1) // 2

    def tile(r, carry):
        base = pl.multiple_of(r * rt, rt)
        xt = pad_ref[pl.ds(base, rt + 2 * DN_HALO), :].astype(F32)
        y = None
        for k in range(DN_CONV):
            term = w_ref[k:k + 1, :] * xt[lo + k:lo + k + rt, :]
            y = term if y is None else y + term
        y = _silu(y)
        segs = []
        for h in range(DN_CB // DN_DK):
            seg = y[:, h * DN_DK:(h + 1) * DN_DK]
            nrm = seg * lax.rsqrt(jnp.sum(seg * seg, axis=-1, keepdims=True) + EPS) * qscale
            segs.append(jnp.where(kind == 2, seg, nrm))
        o_ref[pl.ds(base, rt), :] = jnp.concatenate(segs, axis=-1).astype(BF16)
        return carry

    lax.fori_loop(0, seq_len // rt, tile, 0)


def _gdn_pre(proj, conv_w, *, nb, seq_len):
    rt = 256
    return pl.pallas_call(
        functools.partial(_gdn_pre_body, seq_len=seq_len, rt=rt),
        grid=(nb, DN_QKV // DN_CB),
        in_specs=[
            pl.BlockSpec((seq_len, DN_CB), lambda b, c: (b, c)),
            pl.BlockSpec((DN_CONV, DN_CB), lambda b, c: (0, c)),
        ],
        out_specs=pl.BlockSpec((seq_len, DN_CB), lambda b, c: (b, c)),
        out_shape=jax.ShapeDtypeStruct((nb * seq_len, DN_QKV), BF16),
        scratch_shapes=[pltpu.VMEM((seq_len + 2 * DN_HALO, DN_CB), BF16)],
        compiler_params=_cp(("parallel", "parallel")),
        name="gdn_pre",
    )(proj, conv_w)


def _split_bf16(x, n):
    parts = []
    r = x
    for _ in range(n):
        p = r.astype(BF16)
        parts.append(p)
        r = r - p.astype(F32)
    return parts


def _mm_exact_lhs(a_bf, b):
    return sum(_dot(a_bf, p) for p in _split_bf16(b, 3))


def _softplus(x):
    return jnp.maximum(x, 0.0) + jnp.log(1.0 + jnp.exp(-jnp.abs(x)))


def _gdn_body(q_ref, k_ref, v_ref, gt_ref, ga_ref, gb_ref, s0_ref, o_ref, sl_ref,
              s_scr, vn_scr, a_scr, t_scr, m_scr, attn_scr, rhs_scr, u_scr, w_scr, qg_scr, kg_scr, *, direction):
    t = pl.program_id(1)
    nt = pl.num_programs(1)
    tile = GDN_TILE
    c_len = DN_CHUNK
    nchunk = tile // c_len
    heads = range(DN_HEADS)

    @pl.when(t == 0)
    def _():
        s_scr[...] = s0_ref[...]
        vn_scr[...] = jnp.zeros_like(vn_scr)

    ri = lax.broadcasted_iota(jnp.int32, (tile, tile), 0)
    ci = lax.broadcasted_iota(jnp.int32, (tile, tile), 1)
    same = (ri // c_len) == (ci // c_len)
    if direction == 0:
        incl = same & (ci <= ri)
        strict = same & (ci < ri)
    else:
        incl = same & (ci >= ri)
        strict = same & (ci > ri)
    tri_bf = jnp.where(incl, 1.0, 0.0).astype(BF16)
    blk_bf = jnp.where(same, 1.0, 0.0).astype(BF16)
    nlev = int(math.log2(c_len))

    def off_mask(lvl, rs=slice(0, tile)):
        nrow = rs.stop - rs.start
        r = lax.broadcasted_iota(jnp.int32, (nrow, tile), 0) + rs.start
        c = lax.broadcasted_iota(jnp.int32, (nrow, tile), 1)
        late, early = (r, c) if direction == 0 else (c, r)
        return (((late >> lvl) & 1) == 1) & (((early >> lvl) & 1) == 0) & ((r >> (lvl + 1)) == (c >> (lvl + 1)))

    def active_blocks(lvl):
        hb = 1 << lvl
        return [slice(rb * hb, (rb + 1) * hb) for rb in range(tile // hb) if rb % 2 == 1 - direction]

    gt = gt_ref[...].astype(F32)
    g_all = ga_ref[...] * _softplus(gt + gb_ref[...])
    beta_all = _sigmoid(gt)
    gc_all = _mm_exact_lhs(tri_bf, g_all)
    gl_all = _mm_exact_lhs(blk_bf, g_all)
    gc_t = jnp.transpose(gc_all)
    chunk_order = range(nchunk) if direction == 0 else range(nchunk - 1, -1, -1)
    eye = jnp.where(ri == ci, 1.0, 0.0)

    def lane(h):
        return direction * DN_HEADS + h

    def hsl(h):
        return slice(h * DN_DK, (h + 1) * DN_DK)

    for h in heads:
        gcol = gc_all[:, lane(h):lane(h) + 1]
        grow = gc_t[lane(h):lane(h) + 1, :]
        glcol = gl_all[:, lane(h):lane(h) + 1]
        bcol = beta_all[:, 2 * DN_HEADS + lane(h):2 * DN_HEADS + lane(h) + 1]
        q_bf, k_bf = q_ref[:, hsl(h)], k_ref[:, hsl(h)]
        qh, kh, vh = q_bf.astype(F32), k_bf.astype(F32), v_ref[:, hsl(h)].astype(F32)
        decay = jnp.exp(jnp.where(incl, gcol - grow, NEG_BIG))
        a_mat = jnp.where(strict, bcol * _dot_nt(k_bf, k_bf) * decay, 0.0)
        a_scr[h] = a_mat
        t_scr[h] = (eye - jnp.where(off_mask(0), a_mat, 0.0)).astype(BF16)
        attn_scr[h] = (_dot_nt(q_bf, k_bf) * decay).astype(BF16)
        egc = jnp.exp(gcol)
        rhs_scr[h] = jnp.concatenate([vh * bcol, kh * (bcol * egc)], axis=-1).astype(BF16)
        qg_scr[h] = (qh * egc).astype(BF16)
        kg_scr[h] = (kh * jnp.exp(glcol - gcol)).astype(BF16)

    for lvl in range(1, nlev):
        hb = 1 << lvl
        if hb == SUBLANES:
            blocks = active_blocks(lvl)
            zero = jnp.zeros((hb, tile), F32)

            def scatter(act):
                pieces = []
                for i in range(len(blocks)):
                    p = act[i * hb:(i + 1) * hb, :]
                    pieces += [zero, p] if direction == 0 else [p, zero]
                return jnp.concatenate(pieces, axis=0)

            for h in heads:
                lhs = jnp.concatenate([jnp.where(off_mask(lvl, rs), a_scr[h, rs, :], 0.0) for rs in blocks], axis=0)
                m_scr[h] = scatter(_dot(lhs.astype(BF16), t_scr[h])).astype(BF16)
            for h in heads:
                t_full = t_scr[h].astype(F32)
                t_act = jnp.concatenate([t_full[rs, :] for rs in blocks], axis=0)
                t_scr[h] = (t_full - scatter(_dot(t_act.astype(BF16), m_scr[h]))).astype(BF16)
            continue
        if hb < SUBLANES:
            for h in heads:
                m_scr[h] = _dot(jnp.where(off_mask(lvl), a_scr[h], 0.0).astype(BF16), t_scr[h]).astype(BF16)
            for h in heads:
                t_scr[h] = t_scr[h] - _dot(t_scr[h], m_scr[h]).astype(BF16)
            continue
        blocks = active_blocks(lvl)
        for h in heads:
            lhs = jnp.concatenate([jnp.where(off_mask(lvl, rs), a_scr[h, rs, :], 0.0) for rs in blocks], axis=0)
            m_act = _dot(lhs.astype(BF16), t_scr[h])
            zero = jnp.zeros((hb, tile), F32)
            pieces = [zero] * (tile // hb)
            for i, rs in enumerate(blocks):
                pieces[rs.start // hb] = m_act[i * hb:(i + 1) * hb, :]
            m_scr[h] = jnp.concatenate(pieces, axis=0).astype(BF16)
        for h in heads:
            t_act = jnp.concatenate([t_scr[h, rs, :].astype(F32) for rs in blocks], axis=0)
            upd = _dot(t_act.astype(BF16), m_scr[h])
            for i, rs in enumerate(blocks):
                t_scr[h, rs, :] = t_scr[h, rs, :] - upd[i * hb:(i + 1) * hb, :].astype(BF16)

    for h in heads:
        x = _dot(t_scr[h], rhs_scr[h])
        u_scr[h] = x[:, :DN_DV]
        w_scr[h] = x[:, DN_DV:].astype(BF16)

    for c in chunk_order:
        rows = slice(c * c_len, (c + 1) * c_len)
        for h in heads:
            s = s_scr[h]
            s_bf = s.astype(BF16)
            v_new = u_scr[h, rows, :] - _dot(w_scr[h, rows, :], s_bf)
            vn_scr[h, rows, :] = v_new.astype(BF16)
            o_ref[rows, hsl(h)] = (_dot(qg_scr[h, rows, :], s_bf)
                                   + _dot(attn_scr[h, rows, :], vn_scr[h])).astype(o_ref.dtype)
            gl_c = gl_all[c * c_len:c * c_len + 1, lane(h):lane(h) + 1]
            s_scr[h] = s * jnp.exp(gl_c) + _dot_tn(kg_scr[h, rows, :], v_new.astype(BF16))

    @pl.when(t == nt - 1)
    def _():
        sl_ref[...] = s_scr[...]


def _gdn_scan(qkv, proj, ga_row, gb_row, s0, *, nb, seq_len, direction):
    nt = seq_len // GDN_TILE
    tile = GDN_TILE
    nq = D_MODEL // D_MODEL

    def row(b, t):
        return b * nt + (t if direction == 0 else nt - 1 - t)

    return pl.pallas_call(
        functools.partial(_gdn_body, direction=direction),
        grid=(nb, nt),
        in_specs=[
            pl.BlockSpec((tile, D_MODEL), lambda b, t: (row(b, t), 0)),
            pl.BlockSpec((tile, D_MODEL), lambda b, t: (row(b, t), nq)),
            pl.BlockSpec((tile, D_MODEL), lambda b, t: (row(b, t), 2 * nq)),
            pl.BlockSpec((tile, LANES), lambda b, t: (row(b, t), DN_GATE_BLK)),
            pl.BlockSpec((1, LANES), lambda b, t: (0, 0)),
            pl.BlockSpec((1, LANES), lambda b, t: (0, 0)),
            pl.BlockSpec((None, DN_HEADS, DN_DK, DN_DV), lambda b, t: (b, 0, 0, 0)),
        ],
        out_specs=[
            pl.BlockSpec((tile, D_MODEL), lambda b, t: (row(b, t), 0)),
            pl.BlockSpec((None, DN_HEADS, DN_DK, DN_DV), lambda b, t: (b, 0, 0, 0)),
        ],
        out_shape=[
            jax.ShapeDtypeStruct((nb * seq_len, D_MODEL), BF16),
            jax.ShapeDtypeStruct((nb, DN_HEADS, DN_DK, DN_DV), F32),
        ],
        scratch_shapes=[
            pltpu.VMEM((DN_HEADS, DN_DK, DN_DV), F32),
            pltpu.VMEM((DN_HEADS, tile, DN_DV), BF16),
            pltpu.VMEM((DN_HEADS, tile, tile), F32),
            pltpu.VMEM((DN_HEADS, tile, tile), BF16),
            pltpu.VMEM((DN_HEADS, tile, tile), BF16),
            pltpu.VMEM((DN_HEADS, tile, tile), BF16),
            pltpu.VMEM((DN_HEADS, tile, 2 * DN_DV), BF16),
            pltpu.VMEM((DN_HEADS, tile, DN_DV), F32),
            pltpu.VMEM((DN_HEADS, tile, DN_DK), BF16),
            pltpu.VMEM((DN_HEADS, tile, DN_DK), BF16),
            pltpu.VMEM((DN_HEADS, tile, DN_DK), BF16),
        ],
        compiler_params=_cp(("parallel", "arbitrary")),
        name="gdn_scan_fwd" if direction == 0 else "gdn_scan_bwd",
    )(qkv, qkv, qkv, proj, ga_row, gb_row, s0)


def _odd_out_body(x_ref, of_ref, ob_ref, z_ref, ng_ref, wo_ref, gt_ref, o_ref):
    o = of_ref[...].astype(F32) + ob_ref[...].astype(F32)
    segs = []
    for h in range(DN_HEADS):
        seg = o[:, h * DN_DV:(h + 1) * DN_DV]
        segs.append(seg * lax.rsqrt(jnp.mean(seg * seg, axis=-1, keepdims=True) + EPS))
    y = jnp.concatenate(segs, axis=-1) * ng_ref[...] * _silu(z_ref[...].astype(F32))
    o_ref[...] = x_ref[...] + gt_ref[...] * _dot(y.astype(BF16), wo_ref[...])


def _odd_out(x, o_f, o_b, proj, ng_row, wo_bf, modt_l, seq_fn, *, tm):
    t = x.shape[0]
    zblk = DN_QKV // D_MODEL
    return pl.pallas_call(
        _odd_out_body,
        grid=(t // tm,),
        in_specs=[
            pl.BlockSpec((tm, D_MODEL), lambda i: (i, 0)),
            pl.BlockSpec((tm, D_MODEL), lambda i: (i, 0)),
            pl.BlockSpec((tm, D_MODEL), lambda i: (i, 0)),
            pl.BlockSpec((tm, D_MODEL), lambda i: (i, zblk)),
            pl.BlockSpec((1, D_MODEL), lambda i: (0, 0)),
            pl.BlockSpec((D_MODEL, D_MODEL), lambda i: (0, 0)),
            _mod_spec(2, seq_fn, 1),
        ],
        out_specs=pl.BlockSpec((tm, D_MODEL), lambda i: (i, 0)),
        out_shape=jax.ShapeDtypeStruct((t, D_MODEL), F32),
        compiler_params=_cp(("parallel",)),
        name="odd_out",
    )(x, o_f, o_b, proj, ng_row, wo_bf, modt_l)


ROW_TILE = 1024


def _seq_fn(slot0, seq_len, tm):
    if slot0 == 0:
        return lambda i: 0
    return lambda i: slot0 + (i * tm) // seq_len


def kernel(x_prompt, x_sample, c, cache_na_k, cache_na_v, state_s5_re, state_s5_im, state_dn, c_ctx, norm_mix_g, norm_ff_g, w_mod, b_mod, w_ff1, w_ff2, w_in_e, w_out_e, s5_lam_re, s5_lam_im, s5_log_dt, s5_b_re, s5_b_im, s5_c_re, s5_c_im, s5_d, s5_w_glu, s5_b_glu, na_rpb, w_in_o, dn_conv_w, dn_a_log, dn_dt_bias, dn_norm_g, w_out_o, final_norm_g):
    nbp, lp = x_prompt.shape[:2]
    nbs, ls = x_sample.shape[:2]
    assert nbp % S5_BG == 0 and nbs % S5_BG == 0 and 1 + nbs <= 16
    streams = [
        dict(x=x_prompt.reshape(nbp * lp, D_MODEL), nb=nbp, L=lp, slot0=0),
        dict(x=x_sample.reshape(nbs * ls, D_MODEL), nb=nbs, L=ls, slot0=1),
    ]
    cond16 = jnp.zeros((16, D_MODEL), F32).at[0].set(c_ctx).at[1:1 + nbs].set(c)
    mod = _modulation(cond16, w_mod, b_mod[:, None, :])
    modt = jnp.transpose(mod.reshape(DEPTH, 16, 6, D_MODEL), (0, 2, 1, 3))[:, :, :, None, :]

    new_k, new_v, new_sre, new_sim, new_dn = [], [], [], [], []
    for l in range(DEPTH):
        e = l // 2
        if l % 2 == 0:
            w_in = w_in_e[e].astype(BF16)
            w_out = w_out_e[e].astype(BF16)
            w_glu = s5_w_glu[e].astype(BF16)
            ops = _s5_operators(s5_lam_re[e], s5_lam_im[e], s5_log_dt[e], s5_b_re[e], s5_b_im[e],
                                s5_c_re[e], s5_c_im[e])
            bias = _na_bias_tables(na_rpb[e], ls // GRID_W)
        else:
            w_in = jnp.pad(w_in_o[e], ((0, 0), (0, DN_PROJ_PAD - w_in_o.shape[-1]))).astype(BF16)
            w_out = w_out_o[e].astype(BF16)
            ga_row = jnp.zeros((1, LANES), F32).at[0, :2 * DN_HEADS].set(-jnp.exp(dn_a_log[e]).reshape(-1))
            gb_row = jnp.zeros((1, LANES), F32).at[0, :2 * DN_HEADS].set(dn_dt_bias[e].reshape(-1))
            ng_row = jnp.tile(dn_norm_g[e], DN_HEADS)[None]
        w1, w2 = w_ff1[l], w_ff2[l]
        for st in streams:
            x, nb, sl = st["x"], st["nb"], st["L"]
            seq = _seq_fn(st["slot0"], sl, ROW_TILE)
            is_ctx = st["slot0"] == 0
            if l % 2 == 0:
                proj = _norm_mod_matmul(x, norm_mix_g[l][None], modt[l], w_in, seq, tm=ROW_TILE, name="in_proj_even")
                nbg = nb // S5_BG
                if is_ctx:
                    x0 = jnp.zeros((nbg, S5_JB, S5_BG, 4 * S5_SW), F32)
                else:
                    x0 = _s5_pack_state(state_s5_re[:, e], state_s5_im[:, e])
                y, fin = _s5_mix(proj, ops, x0, nbg=nbg, seq_len=sl)
                if is_ctx:
                    bo = _context_attention(proj, nb=nb, seq_len=sl)
                    kv = proj.reshape(nb, sl, -1)
                    new_k.append(kv[:, :, MIX_A + MIX_B:MIX_A + 2 * MIX_B].reshape(nb, sl, NA_HEADS, NA_HD))
                    new_v.append(kv[:, :, MIX_A + 2 * MIX_B:].reshape(nb, sl, NA_HEADS, NA_HD))
                    sre, sim = _s5_unpack_state(fin)
                    new_sre.append(sre)
                    new_sim.append(sim)
                else:
                    past = cache_na_k.shape[2]
                    ck = cache_na_k[:, e].reshape(nb, past, MIX_B)
                    cv = cache_na_v[:, e].reshape(nb, past, MIX_B)
                    bo = _neighbourhood_attention(proj, ck, cv, bias, nb=nb, seq_len=sl)
                x = _even_out(x, y.reshape(nb * sl, MIX_A), proj, bo, s5_d[e][None], w_glu, s5_b_glu[e][None],
                              w_out, modt[l], seq, tm=ROW_TILE)
            else:
                proj = _norm_mod_matmul(x, norm_mix_g[l][None], modt[l], w_in, seq, tm=ROW_TILE, name="in_proj_odd",
                                        out_dtype=BF16)
                qkv = _gdn_pre(proj, dn_conv_w[e], nb=nb, seq_len=sl)
                if is_ctx:
                    s0 = jnp.zeros((nb, 2, DN_HEADS, DN_DK, DN_DV), F32)
                else:
                    s0 = state_dn[:, e]
                o_f, s_f = _gdn_scan(qkv, proj, ga_row, gb_row, s0[:, 0], nb=nb, seq_len=sl, direction=0)
                o_b, s_b = _gdn_scan(qkv, proj, ga_row, gb_row, s0[:, 1], nb=nb, seq_len=sl, direction=1)
                if is_ctx:
                    new_dn.append(jnp.stack([s_f, s_b], axis=1))
                x = _odd_out(x, o_f, o_b, proj, ng_row, w_out, modt[l], seq, tm=ROW_TILE)
            st["x"] = _mlp(x, norm_ff_g[l][None], modt[l], w1, w2, seq, final_norm_g[None], tm=ROW_TILE, tf=1024,
                           final_norm=(l == DEPTH - 1))

    y_prompt = streams[0]["x"].reshape(nbp, lp, D_MODEL)
    y_sample = streams[1]["x"].reshape(nbs, ls, D_MODEL)
    return (y_prompt, y_sample, jnp.stack(new_k, axis=1), jnp.stack(new_v, axis=1),
            jnp.stack(new_sre, axis=1), jnp.stack(new_sim, axis=1), jnp.stack(new_dn, axis=1))
```

```python
import functools
import math

import jax
import jax.numpy as jnp
from jax import lax
from jax.experimental import pallas as pl
from jax.experimental.pallas import tpu as pltpu

F32 = jnp.float32
BF16 = jnp.bfloat16

D_MODEL = 1024
DEPTH = 2
GRID_W = 64
MIX_A = D_MODEL // 2
A_GROUP = 16
A_GROUPS = MIX_A // A_GROUP
S5_N = 64
MIX_B = D_MODEL - MIX_A
NA_HEADS = 8
NA_HD = MIX_B // NA_HEADS
NA_KH = 8
NA_KW = 16
DN_HEADS = 8
DN_DK = D_MODEL // DN_HEADS
DN_DV = D_MODEL // DN_HEADS
DN_CONV = 5
DN_CHUNK = 64
D_FF = 4 * D_MODEL
EPS = 1e-6

LANES = 128
SUBLANES = 8
VMEM_LIMIT = 56 * 1024 * 1024
NEG_BIG = -1e30

S5_T = 16
S5_BG = SUBLANES
S5_JB = MIX_A // LANES
S5_GPB = LANES // A_GROUP
S5_SW = S5_GPB * S5_N
GDN_TILE = 256


def _cp(sem, vmem=VMEM_LIMIT):
    return pltpu.CompilerParams(dimension_semantics=sem, vmem_limit_bytes=vmem)


def _rms(x, g):
    return x * lax.rsqrt(jnp.mean(x * x, axis=-1, keepdims=True) + EPS) * g


def _sigmoid(x):
    return 1.0 / (1.0 + jnp.exp(-x))


def _silu(x):
    return x * _sigmoid(x)


def _gelu_tanh(x):
    return 0.5 * x * (1.0 + jnp.tanh(math.sqrt(2.0 / math.pi) * (x + 0.044715 * (x * x * x))))


def _dot(a, b):
    return jnp.dot(a, b, preferred_element_type=F32)


def _dot_nt(a, b):
    return lax.dot_general(a, b, (((1,), (1,)), ((), ())), preferred_element_type=F32)


def _dot_tn(a, b):
    return lax.dot_general(a, b, (((0,), (0,)), ((), ())), preferred_element_type=F32)


def _mod_body(c_ref, w_ref, b_ref, o_ref):
    c = c_ref[...]
    o_ref[...] = _dot(_silu(c).astype(BF16), w_ref[...].astype(BF16)) + b_ref[...]


def _modulation(cond16, w_mod_bf, b_mod):
    tn = 1536
    n = 6 * D_MODEL
    return pl.pallas_call(
        _mod_body,
        grid=(DEPTH, n // tn),
        in_specs=[
            pl.BlockSpec((16, D_MODEL), lambda l, j: (0, 0)),
            pl.BlockSpec((None, D_MODEL, tn), lambda l, j: (l, 0, j)),
            pl.BlockSpec((None, 1, tn), lambda l, j: (l, 0, j)),
        ],
        out_specs=pl.BlockSpec((None, 16, tn), lambda l, j: (l, 0, j)),
        out_shape=jax.ShapeDtypeStruct((DEPTH, 16, n), F32),
        compiler_params=_cp(("parallel", "parallel")),
        name="modulation",
    )(cond16, w_mod_bf, b_mod)


def _mod_spec(which, seq_fn, ngrid):
    if ngrid == 1:
        return pl.BlockSpec((None, None, 1, D_MODEL), lambda i: (which, seq_fn(i), 0, 0))
    return pl.BlockSpec((None, None, 1, D_MODEL), lambda i, j: (which, seq_fn(i), 0, 0))


def _nmm_body(x_ref, g_ref, sh_ref, sc_ref, w_ref, o_ref):
    h = _rms(x_ref[...], g_ref[...]) * (1.0 + sc_ref[...]) + sh_ref[...]
    o_ref[...] = _dot(h.astype(BF16), w_ref[...]).astype(o_ref.dtype)


def _norm_mod_matmul(x, g_row, modt_l, w_bf, seq_fn, *, tm, name, out_dtype=F32):
    t, n = x.shape[0], w_bf.shape[1]
    return pl.pallas_call(
        _nmm_body,
        grid=(t // tm,),
        in_specs=[
            pl.BlockSpec((tm, D_MODEL), lambda i: (i, 0)),
            pl.BlockSpec((1, D_MODEL), lambda i: (0, 0)),
            _mod_spec(0, seq_fn, 1),
            _mod_spec(1, seq_fn, 1),
            pl.BlockSpec((D_MODEL, n), lambda i: (0, 0)),
        ],
        out_specs=pl.BlockSpec((tm, n), lambda i: (i, 0)),
        out_shape=jax.ShapeDtypeStruct((t, n), out_dtype),
        compiler_params=_cp(("parallel",)),
        name=name,
    )(x, g_row, modt_l, modt_l, w_bf)


def _mlp_body(x_ref, g_ref, sh_ref, sc_ref, gt_ref, w1_ref, w2_ref, fg_ref, o_ref, h_ref, acc_ref, *, final_norm):
    k = pl.program_id(1)

    @pl.when(k == 0)
    def _():
        h = _rms(x_ref[...], g_ref[...]) * (1.0 + sc_ref[...]) + sh_ref[...]
        h_ref[...] = h.astype(BF16)
        acc_ref[...] = jnp.zeros_like(acc_ref)

    a = jnp.maximum(_dot(h_ref[...], w1_ref[...].astype(BF16)), 0.0)
    acc_ref[...] += _dot((a * a).astype(BF16), w2_ref[...].astype(BF16))

    @pl.when(k == pl.num_programs(1) - 1)
    def _():
        y = x_ref[...] + gt_ref[...] * acc_ref[...]
        o_ref[...] = _rms(y, fg_ref[...]) if final_norm else y


def _mlp(x, g_row, modt_l, w1_bf, w2_bf, seq_fn, fg_row, *, tm, tf, final_norm):
    t = x.shape[0]
    return pl.pallas_call(
        functools.partial(_mlp_body, final_norm=final_norm),
        grid=(t // tm, D_FF // tf),
        in_specs=[
            pl.BlockSpec((tm, D_MODEL), lambda i, k: (i, 0)),
            pl.BlockSpec((1, D_MODEL), lambda i, k: (0, 0)),
            _mod_spec(3, seq_fn, 2),
            _mod_spec(4, seq_fn, 2),
            _mod_spec(5, seq_fn, 2),
            pl.BlockSpec((D_MODEL, tf), lambda i, k: (0, k)),
            pl.BlockSpec((tf, D_MODEL), lambda i, k: (k, 0)),
            pl.BlockSpec((1, D_MODEL), lambda i, k: (0, 0)),
        ],
        out_specs=pl.BlockSpec((tm, D_MODEL), lambda i, k: (i, 0)),
        out_shape=jax.ShapeDtypeStruct((t, D_MODEL), F32),
        scratch_shapes=[pltpu.VMEM((tm, D_MODEL), BF16), pltpu.VMEM((tm, D_MODEL), F32)],
        compiler_params=_cp(("parallel", "arbitrary")),
        name="mlp",
    )(x, g_row, modt_l, modt_l, modt_l, w1_bf, w2_bf, fg_row)


def _s5_operators(lam_re, lam_im, log_dt, b_re, b_im, c_re, c_im):
    hp = lax.Precision.HIGHEST
    t = S5_T
    dt = jnp.exp(log_dt)[..., None]
    ar, ai = lam_re * dt, lam_im * dt
    er = jnp.exp(ar)
    lbr, lbi = er * jnp.cos(ai), er * jnp.sin(ai)
    den = lam_re * lam_re + lam_im * lam_im
    fr = ((lbr - 1.0) * lam_re + lbi * lam_im) / den
    fi = (lbi * lam_re - (lbr - 1.0) * lam_im) / den
    bbr = fr[..., None] * b_re - fi[..., None] * b_im
    bbi = fr[..., None] * b_im + fi[..., None] * b_re
    k = jnp.arange(t + 1, dtype=F32)[:, None, None, None]
    pr = jnp.exp(k * ar[None]) * jnp.cos(k * ai[None])
    pi = jnp.exp(k * ar[None]) * jnp.sin(k * ai[None])
    pbr = pr[..., None] * bbr[None] - pi[..., None] * bbi[None]
    pbi = pr[..., None] * bbi[None] + pi[..., None] * bbr[None]
    pcr = c_re[None] * pr[:, :, :, None, :] - c_im[None] * pi[:, :, :, None, :]
    pci = c_re[None] * pi[:, :, :, None, :] + c_im[None] * pr[:, :, :, None, :]
    kern = (jnp.einsum("dgpn,kdgnq->kdgpq", c_re, pbr[:t], precision=hp)
            - jnp.einsum("dgpn,kdgnq->kdgpq", c_im, pbi[:t], precision=hp))
    import numpy as np

    jb, gpb = S5_JB, S5_GPB
    kw, sw4 = t * LANES, 4 * S5_SW

    def expand(compact, col_src, row_group, col_group):
        onehot = jnp.asarray(col_src[None, :] == np.arange(compact.shape[-1])[:, None], BF16)
        full = jnp.dot(compact.astype(BF16), onehot, preferred_element_type=BF16)
        return jnp.where(jnp.asarray(row_group[:, None] == col_group[None, :])[None], full, 0.0)

    state_rows = np.arange(sw4)
    chunk_rows = np.arange(kw)
    state_grp, chunk_grp = (state_rows // S5_N) % gpb, (chunk_rows // A_GROUP) % gpb
    state_src = (state_rows // S5_SW) * S5_N + state_rows % S5_N
    chunk_src = (chunk_rows // LANES) * A_GROUP + chunk_rows % A_GROUP

    sidx = jnp.arange(t)
    secs = jnp.stack([pbr[t - 1 - sidx, 0], pbi[t - 1 - sidx, 0], pbr[sidx, 1], pbi[sidx, 1]], axis=0)
    secs = secs.reshape(4, t, jb, gpb, S5_N, A_GROUP)
    bst = expand(jnp.transpose(secs, (2, 1, 3, 5, 0, 4)).reshape(jb, kw, 4 * S5_N), state_src, chunk_grp, state_grp)
    secs = jnp.stack([pcr[sidx + 1, 0], -pci[sidx + 1, 0], pcr[t - sidx, 1], -pci[t - sidx, 1]], axis=0)
    secs = secs.reshape(4, t, jb, gpb, A_GROUP, S5_N)
    cst = expand(jnp.transpose(secs, (2, 0, 3, 5, 1, 4)).reshape(jb, sw4, t * A_GROUP), chunk_src, state_grp, chunk_grp)
    noff = 2 * t - 1
    kd = jnp.concatenate([kern[1:, 1][::-1], (kern[0, 0] + kern[0, 1])[None], kern[1:, 0]], axis=0)
    kd = jnp.transpose(kd.reshape(noff, jb, gpb, A_GROUP, A_GROUP), (1, 0, 2, 4, 3))
    lane_rows = np.arange(noff * LANES)
    blocks = expand(kd.reshape(jb, noff * LANES, A_GROUP), np.arange(LANES) % A_GROUP,
                    (lane_rows // A_GROUP) % gpb, np.arange(LANES) // A_GROUP)
    off = np.arange(t)[None, :] - np.arange(t)[:, None] + t - 1
    wt = jnp.take(blocks.reshape(jb, noff, LANES, LANES), jnp.asarray(off), axis=1)
    wt = jnp.transpose(wt, (0, 1, 3, 2, 4)).reshape(jb, kw, kw)
    acoef = jnp.stack([pr[t, 0], pi[t, 0], pr[t, 1], pi[t, 1]], axis=0)
    acoef = acoef.reshape(4, jb, S5_SW).transpose(1, 0, 2).reshape(jb, 1, 4 * S5_SW)
    acoef = jnp.broadcast_to(acoef, (jb, S5_BG, 4 * S5_SW))
    return bst.astype(BF16), cst.astype(BF16), wt.astype(BF16), acoef


def _s5_in_body(u_ref, bst_ref, lhs_ref, s_ref, *, seq_len, row_tile):
    nc = seq_len // S5_T

    @pl.when(pl.program_id(2) == 0)
    def _():
        def build(cp, carry):
            base = cp * (2 * S5_T)
            for s in range(S5_T):
                a = u_ref[pl.ds(base + s, S5_BG, stride=seq_len), :]
                b = u_ref[pl.ds(base + S5_T + s, S5_BG, stride=seq_len), :]
                lhs_ref[pl.ds(pl.multiple_of(cp * 16, 16), 16), s * LANES:(s + 1) * LANES] = (
                    jnp.concatenate([a, b], axis=0).astype(BF16))
            return carry

        lax.fori_loop(0, nc // 2, build, 0)

    def mm(r, carry):
        rows = pl.ds(pl.multiple_of(r * row_tile, row_tile), row_tile)
        s_ref[rows, :] = _dot(lhs_ref[rows, :], bst_ref[...])
        return carry

    lax.fori_loop(0, (nc * S5_BG) // row_tile, mm, 0)


def _s5_state_in(proj2d, bst, *, nbg, seq_len):
    nc = seq_len // S5_T
    rows = nc * S5_BG
    kw = S5_T * LANES
    half = 2 * S5_SW
    row_tile = min(rows, 256)
    return pl.pallas_call(
        functools.partial(_s5_in_body, seq_len=seq_len, row_tile=row_tile),
        grid=(nbg, S5_JB, 2),
        in_specs=[
            pl.BlockSpec((S5_BG * seq_len, LANES), lambda g, j, n: (g, j)),
            pl.BlockSpec((None, kw, half), lambda g, j, n: (j, 0, n)),
        ],
        out_specs=[
            pl.BlockSpec((None, None, rows, kw), lambda g, j, n: (g, j, 0, 0)),
            pl.BlockSpec((None, None, rows, half), lambda g, j, n: (g, j, 0, n)),
        ],
        out_shape=[
            jax.ShapeDtypeStruct((nbg, S5_JB, rows, kw), BF16),
            jax.ShapeDtypeStruct((nbg, S5_JB, rows, 4 * S5_SW), F32),
        ],
        compiler_params=_cp(("parallel", "parallel", "arbitrary")),
        name="s5_state_in",
    )(proj2d, bst)


def _s5_scan_body(s_ref, a_ref, x0_ref, x_ref, fin_ref, *, nc):
    w = S5_SW
    afr, afi = a_ref[:, 0:w], a_ref[:, w:2 * w]
    abr, abi = a_ref[:, 2 * w:3 * w], a_ref[:, 3 * w:4 * w]

    def step(c, carry):
        xfr, xfi, xbr, xbi = carry
        rf = pl.ds(pl.multiple_of(c * S5_BG, S5_BG), S5_BG)
        rb = pl.ds(pl.multiple_of((nc - 1 - c) * S5_BG, S5_BG), S5_BG)
        x_ref[rf, 0:w] = xfr
        x_ref[rf, w:2 * w] = xfi
        x_ref[rb, 2 * w:3 * w] = xbr
        x_ref[rb, 3 * w:4 * w] = xbi
        nfr = afr * xfr - afi * xfi + s_ref[rf, 0:w]
        nfi = afr * xfi + afi * xfr + s_ref[rf, w:2 * w]
        nbr = abr * xbr - abi * xbi + s_ref[rb, 2 * w:3 * w]
        nbi = abr * xbi + abi * xbr + s_ref[rb, 3 * w:4 * w]
        return nfr, nfi, nbr, nbi

    init = (x0_ref[:, 0:w], x0_ref[:, w:2 * w], x0_ref[:, 2 * w:3 * w], x0_ref[:, 3 * w:4 * w])
    xfr, xfi, xbr, xbi = lax.fori_loop(0, nc, step, init)
    fin_ref[:, 0:w] = xfr
    fin_ref[:, w:2 * w] = xfi
    fin_ref[:, 2 * w:3 * w] = xbr
    fin_ref[:, 3 * w:4 * w] = xbi


def _s5_scan(s_all, acoef, x0, *, nbg, seq_len):
    nc = seq_len // S5_T
    rows = nc * S5_BG
    sw4 = 4 * S5_SW
    return pl.pallas_call(
        functools.partial(_s5_scan_body, nc=nc),
        grid=(nbg, S5_JB),
        in_specs=[
            pl.BlockSpec((None, None, rows, sw4), lambda g, j: (g, j, 0, 0)),
            pl.BlockSpec((None, S5_BG, sw4), lambda g, j: (j, 0, 0)),
            pl.BlockSpec((None, None, S5_BG, sw4), lambda g, j: (g, j, 0, 0)),
        ],
        out_specs=[
            pl.BlockSpec((None, None, rows, sw4), lambda g, j: (g, j, 0, 0)),
            pl.BlockSpec((None, None, S5_BG, sw4), lambda g, j: (g, j, 0, 0)),
        ],
        out_shape=[
            jax.ShapeDtypeStruct((nbg, S5_JB, rows, sw4), F32),
            jax.ShapeDtypeStruct((nbg, S5_JB, S5_BG, sw4), F32),
        ],
        compiler_params=_cp(("parallel", "parallel")),
        name="s5_scan",
    )(s_all, acoef, x0)


def _s5_out_body(lhs_ref, x_ref, wt_ref, cst_ref, y_ref, acc_ref, *, chunks):
    acc_ref[...] = _dot(lhs_ref[...], wt_ref[...]) + _dot(x_ref[...].astype(BF16), cst_ref[...])

    def unchunk(c, carry):
        rows = pl.ds(pl.multiple_of(c * S5_BG, S5_BG), S5_BG)
        for t in range(S5_T):
            y_ref[:, c * S5_T + t, :] = acc_ref[rows, t * LANES:(t + 1) * LANES]
        return carry

    lax.fori_loop(0, chunks, unchunk, 0)


def _s5_output(lhs, x_all, wt, cst, *, nbg, seq_len):
    nc = seq_len // S5_T
    chunks = min(nc, 32)
    tr = chunks * S5_BG
    kw = S5_T * LANES
    sw4 = 4 * S5_SW
    return pl.pallas_call(
        functools.partial(_s5_out_body, chunks=chunks),
        grid=(nbg, S5_JB, nc // chunks),
        in_specs=[
            pl.BlockSpec((None, None, tr, kw), lambda g, j, r: (g, j, r, 0)),
            pl.BlockSpec((None, None, tr, sw4), lambda g, j, r: (g, j, r, 0)),
            pl.BlockSpec((None, kw, kw), lambda g, j, r: (j, 0, 0)),
            pl.BlockSpec((None, sw4, kw), lambda g, j, r: (j, 0, 0)),
        ],
        out_specs=pl.BlockSpec((S5_BG, chunks * S5_T, LANES), lambda g, j, r: (g, r, j)),
        out_shape=jax.ShapeDtypeStruct((nbg * S5_BG, seq_len, MIX_A), F32),
        scratch_shapes=[pltpu.VMEM((tr, kw), F32)],
        compiler_params=_cp(("parallel", "parallel", "arbitrary")),
        name="s5_output",
    )(lhs, x_all, wt, cst)


def _s5_mix(proj2d, ops, x0, *, nbg, seq_len):
    bst, cst, wt, acoef = ops
    lhs, s_all = _s5_state_in(proj2d, bst, nbg=nbg, seq_len=seq_len)
    x_all, fin = _s5_scan(s_all, acoef, x0, nbg=nbg, seq_len=seq_len)
    y = _s5_output(lhs, x_all, wt, cst, nbg=nbg, seq_len=seq_len)
    return y, fin


def _s5_pack_state(re, im):
    b = re.shape[0]
    x = jnp.stack([re[:, 0], im[:, 0], re[:, 1], im[:, 1]], axis=1)
    x = x.reshape(b // S5_BG, S5_BG, 4, S5_JB, S5_SW)
    return jnp.transpose(x, (0, 3, 1, 2, 4)).reshape(b // S5_BG, S5_JB, S5_BG, 4 * S5_SW)


def _s5_unpack_state(fin):
    nbg = fin.shape[0]
    x = fin.reshape(nbg, S5_JB, S5_BG, 4, S5_SW)
    x = jnp.transpose(x, (0, 2, 3, 1, 4)).reshape(nbg * S5_BG, 4, A_GROUPS, S5_N)
    return jnp.stack([x[:, 0], x[:, 2]], axis=1), jnp.stack([x[:, 1], x[:, 3]], axis=1)


Q_BLK, K_BLK, V_BLK = MIX_A // LANES, (MIX_A + MIX_B) // LANES, (MIX_A + 2 * MIX_B) // LANES


def _pair_softmax_pv(q, parts):
    lane = lax.broadcasted_iota(jnp.int32, (1, LANES), 1)
    outs = []
    for half in range(2):
        sel = (lane < NA_HD) if half == 0 else (lane >= NA_HD)
        qm = jnp.where(sel, q, 0.0).astype(BF16)
        scores = []
        for k, _, bias in parts:
            s = _dot_nt(qm, k)
            if bias is not None:
                s = s + bias[half]
            scores.append(s)
        m = scores[0].max(axis=-1, keepdims=True)
        for s in scores[1:]:
            m = jnp.maximum(m, s.max(axis=-1, keepdims=True))
        l = None
        o = None
        for s, (_, v, _) in zip(scores, parts):
            p = jnp.exp(s - m)
            ls = p.sum(axis=-1, keepdims=True)
            os_ = _dot(p.astype(BF16), v)
            l = ls if l is None else l + ls
            o = os_ if o is None else o + os_
        outs.append(o / l)
    return jnp.where(lane < NA_HD, outs[0], outs[1]).astype(BF16)


def _ctx_attn_body(q_ref, k_ref, v_ref, o_ref):
    q = q_ref[...] * (NA_HD ** -0.5)
    o_ref[...] = _pair_softmax_pv(q, [(k_ref[...].astype(BF16), v_ref[...].astype(BF16), None)])


def _context_attention(proj, *, nb, seq_len):
    npair = NA_HEADS // 2
    return pl.pallas_call(
        _ctx_attn_body,
        grid=(nb, npair),
        in_specs=[
            pl.BlockSpec((seq_len, LANES), lambda b, h: (b, Q_BLK + h)),
            pl.BlockSpec((seq_len, LANES), lambda b, h: (b, K_BLK + h)),
            pl.BlockSpec((seq_len, LANES), lambda b, h: (b, V_BLK + h)),
        ],
        out_specs=pl.BlockSpec((seq_len, LANES), lambda b, h: (b, h)),
        out_shape=jax.ShapeDtypeStruct((nb * seq_len, MIX_B), BF16),
        compiler_params=_cp(("parallel", "parallel")),
        name="context_attention",
    )(proj, proj, proj)


NA_QROWS = 4
NA_WROWS = NA_QROWS + NA_KH


def _na_window_start(g, rows):
    rs = jnp.clip(g * NA_QROWS - NA_KH // 2, 0, rows - NA_KH)
    return jnp.minimum(rs, rows - NA_WROWS)


def _na_bias_tables(rpb, rows):
    import numpy as np

    ngroups = rows // NA_QROWS
    shp = (NA_QROWS * GRID_W, NA_WROWS * GRID_W)
    ndc = 2 * NA_KW - 1
    c, kc = np.meshgrid(np.arange(GRID_W), np.arange(GRID_W), indexing="ij")
    cs = np.clip(c - NA_KW // 2, 0, GRID_W - NA_KW)
    col_ok = (kc >= cs) & (kc < cs + NA_KW)
    col_sel = ((kc - c + NA_KW - 1)[None] == np.arange(ndc)[:, None, None]) & col_ok[None]
    dr_idx, valid = [], []
    for g in (0, 1, ngroups - 1):
        rs0 = int(np.clip(g * NA_QROWS - NA_KH // 2, 0, rows - NA_KH))
        ws = min(rs0, rows - NA_WROWS)
        ri, wi = np.meshgrid(np.arange(NA_QROWS), np.arange(NA_WROWS), indexing="ij")
        r = g * NA_QROWS + ri
        rs = np.clip(r - NA_KH // 2, 0, rows - NA_KH)
        row = ws + wi
        row_ok = (row >= rs) & (row < rs + NA_KH)
        dr_idx.append(np.clip(row - r + NA_KH - 1, 0, 2 * NA_KH - 2))
        valid.append((row_ok[:, None, :, None] & col_ok[None, :, None, :]).reshape(shp))
    rows_sel = rpb[:, np.stack(dr_idx)]
    rows_sel = rows_sel.reshape(NA_HEADS, 3, NA_QROWS, NA_WROWS // 2, 2 * ndc)
    pair_sel = np.zeros((2, ndc, GRID_W, 2, GRID_W), np.float32)
    for wlo in range(2):
        pair_sel[wlo, :, :, wlo, :] = col_sel
    pair_sel = pair_sel.reshape(2 * ndc, GRID_W, 2 * GRID_W)
    t = jnp.einsum("hgrwe,eck->hgrcwk", rows_sel, jnp.asarray(pair_sel), precision=lax.Precision.HIGHEST)
    return jnp.where(jnp.asarray(np.stack(valid))[None], t.reshape(NA_HEADS, 3, *shp), NEG_BIG)


NA_GPS = 4


def _nbr_attn_body(q_ref, k_ref, v_ref, ck_ref, cv_ref, bias_ref, o_ref, *, rows):
    ngroups = rows // NA_QROWS
    qt = NA_QROWS * GRID_W
    ck, cv = ck_ref[...].astype(BF16), cv_ref[...].astype(BF16)
    for u in range(NA_GPS):
        g = pl.program_id(2) * NA_GPS + u
        case = jnp.where(g == 0, 0, jnp.where(g == ngroups - 1, 2, 1))
        ws = _na_window_start(g, rows)
        win = pl.ds(pl.multiple_of(ws * GRID_W, GRID_W), NA_WROWS * GRID_W)
        qrows = slice(u * qt, (u + 1) * qt)
        q = q_ref[qrows, :] * (NA_HD ** -0.5)
        kw = k_ref[win, :].astype(BF16)
        vw = v_ref[win, :].astype(BF16)
        bias = (bias_ref[0, case], bias_ref[1, case])
        o_ref[qrows, :] = _pair_softmax_pv(q, [(kw, vw, bias), (ck, cv, None)])


def _neighbourhood_attention(proj, ck, cv, bias, *, nb, seq_len):
    rows = seq_len // GRID_W
    ngroups = rows // NA_QROWS // NA_GPS
    npair = NA_HEADS // 2
    qt = NA_GPS * NA_QROWS * GRID_W
    past = ck.shape[1]
    return pl.pallas_call(
        functools.partial(_nbr_attn_body, rows=rows),
        grid=(npair, nb, ngroups),
        in_specs=[
            pl.BlockSpec((qt, LANES), lambda h, b, g: (b * ngroups + g, Q_BLK + h)),
            pl.BlockSpec((seq_len, LANES), lambda h, b, g: (b, K_BLK + h)),
            pl.BlockSpec((seq_len, LANES), lambda h, b, g: (b, V_BLK + h)),
            pl.BlockSpec((None, past, LANES), lambda h, b, g: (b, 0, h)),
            pl.BlockSpec((None, past, LANES), lambda h, b, g: (b, 0, h)),
            pl.BlockSpec((2, 3, NA_QROWS * GRID_W, NA_WROWS * GRID_W), lambda h, b, g: (h, 0, 0, 0)),
        ],
        out_specs=pl.BlockSpec((qt, LANES), lambda h, b, g: (b * ngroups + g, h)),
        out_shape=jax.ShapeDtypeStruct((nb * seq_len, MIX_B), BF16),
        compiler_params=_cp(("parallel", "parallel", "parallel")),
        name="neighbourhood_attention",
    )(proj, proj, proj, ck, cv, bias)


def _even_out_body(x_ref, y_ref, u_ref, bo_ref, d_ref, wg_ref, bg_ref, wo_ref, gt_ref, o_ref):
    z = _gelu_tanh(y_ref[...] + d_ref[...] * u_ref[...])
    a = z * _sigmoid(_dot(z.astype(BF16), wg_ref[...]) + bg_ref[...])
    cat = jnp.concatenate([a.astype(BF16), bo_ref[...]], axis=-1)
    o_ref[...] = x_ref[...] + gt_ref[...] * _dot(cat, wo_ref[...])


def _even_out(x, y, proj, bo, d_row, wg_bf, bg_row, wo_bf, modt_l, seq_fn, *, tm):
    t = x.shape[0]
    return pl.pallas_call(
        _even_out_body,
        grid=(t // tm,),
        in_specs=[
            pl.BlockSpec((tm, D_MODEL), lambda i: (i, 0)),
            pl.BlockSpec((tm, MIX_A), lambda i: (i, 0)),
            pl.BlockSpec((tm, MIX_A), lambda i: (i, 0)),
            pl.BlockSpec((tm, MIX_B), lambda i: (i, 0)),
            pl.BlockSpec((1, MIX_A), lambda i: (0, 0)),
            pl.BlockSpec((MIX_A, MIX_A), lambda i: (0, 0)),
            pl.BlockSpec((1, MIX_A), lambda i: (0, 0)),
            pl.BlockSpec((D_MODEL, D_MODEL), lambda i: (0, 0)),
            _mod_spec(2, seq_fn, 1),
        ],
        out_specs=pl.BlockSpec((tm, D_MODEL), lambda i: (i, 0)),
        out_shape=jax.ShapeDtypeStruct((t, D_MODEL), F32),
        compiler_params=_cp(("parallel",)),
        name="even_out",
    )(x, y, proj, bo, d_row, wg_bf, bg_row, wo_bf, modt_l)


DN_QKV = 3 * D_MODEL
DN_PROJ_PAD = 4 * D_MODEL + LANES
DN_GATE_BLK = (4 * D_MODEL) // LANES
DN_CB = 512
DN_HALO = 2 * SUBLANES


def _gdn_pre_body(x_ref, w_ref, o_ref, pad_ref, *, seq_len, rt):
    kind = pl.program_id(1) // (D_MODEL // DN_CB)
    zeros = jnp.zeros((DN_HALO, DN_CB), BF16)
    pad_ref[0:DN_HALO, :] = zeros
    pad_ref[DN_HALO + seq_len:DN_HALO + seq_len + DN_HALO, :] = zeros
    pad_ref[DN_HALO:DN_HALO + seq_len, :] = x_ref[...]
    qscale = jnp.where(kind == 0, DN_DK ** -0.5, 1.0)
    lo = DN_HALO - (DN_CONV - 1) // 2

    def tile(r, carry):
        base = pl.multiple_of(r * rt, rt)
        xt = pad_ref[pl.ds(base, rt + 2 * DN_HALO), :].astype(F32)
        y = None
        for k in range(DN_CONV):
            term = w_ref[k:k + 1, :] * xt[lo + k:lo + k + rt, :]
            y = term if y is None else y + term
        y = _silu(y)
        segs = []
        for h in range(DN_CB // DN_DK):
            seg = y[:, h * DN_DK:(h + 1) * DN_DK]
            nrm = seg * lax.rsqrt(jnp.sum(seg * seg, axis=-1, keepdims=True) + EPS) * qscale
            segs.append(jnp.where(kind == 2, seg, nrm))
        o_ref[pl.ds(base, rt), :] = jnp.concatenate(segs, axis=-1).astype(BF16)
        return carry

    lax.fori_loop(0, seq_len // rt, tile, 0)


def _gdn_pre(proj, conv_w, *, nb, seq_len):
    rt = 256
    return pl.pallas_call(
        functools.partial(_gdn_pre_body, seq_len=seq_len, rt=rt),
        grid=(nb, DN_QKV // DN_CB),
        in_specs=[
            pl.BlockSpec((seq_len, DN_CB), lambda b, c: (b, c)),
            pl.BlockSpec((DN_CONV, DN_CB), lambda b, c: (0, c)),
        ],
        out_specs=pl.BlockSpec((seq_len, DN_CB), lambda b, c: (b, c)),
        out_shape=jax.ShapeDtypeStruct((nb * seq_len, DN_QKV), BF16),
        scratch_shapes=[pltpu.VMEM((seq_len + 2 * DN_HALO, DN_CB), BF16)],
        compiler_params=_cp(("parallel", "parallel")),
        name="gdn_pre",
    )(proj, conv_w)


def _split_bf16(x, n):
    parts = []
    r = x
    for _ in range(n):
        p = r.astype(BF16)
        parts.append(p)
        r = r - p.astype(F32)
    return parts


def _mm_exact_lhs(a_bf, b):
    return sum(_dot(a_bf, p) for p in _split_bf16(b, 3))


def _softplus(x):
    return jnp.maximum(x, 0.0) + jnp.log(1.0 + jnp.exp(-jnp.abs(x)))


def _gdn_body(q_ref, k_ref, v_ref, gt_ref, ga_ref, gb_ref, s0_ref, o_ref, sl_ref,
              s_scr, vn_scr, a_scr, t_scr, m_scr, attn_scr, rhs_scr, u_scr, w_scr, qg_scr, kg_scr, *, direction,
              add_ref=None):
    t = pl.program_id(1)
    nt = pl.num_programs(1)
    tile = GDN_TILE
    c_len = DN_CHUNK
    nchunk = tile // c_len
    heads = range(DN_HEADS)

    @pl.when(t == 0)
    def _():
        s_scr[...] = s0_ref[...]
        vn_scr[...] = jnp.zeros_like(vn_scr)

    ri = lax.broadcasted_iota(jnp.int32, (tile, tile), 0)
    ci = lax.broadcasted_iota(jnp.int32, (tile, tile), 1)
    same = (ri // c_len) == (ci // c_len)
    if direction == 0:
        incl = same & (ci <= ri)
        strict = same & (ci < ri)
    else:
        incl = same & (ci >= ri)
        strict = same & (ci > ri)
    tri_bf = jnp.where(incl, 1.0, 0.0).astype(BF16)
    blk_bf = jnp.where(same, 1.0, 0.0).astype(BF16)
    nlev = int(math.log2(c_len))

    def off_mask(lvl, rs=slice(0, tile)):
        nrow = rs.stop - rs.start
        r = lax.broadcasted_iota(jnp.int32, (nrow, tile), 0) + rs.start
        c = lax.broadcasted_iota(jnp.int32, (nrow, tile), 1)
        late, early = (r, c) if direction == 0 else (c, r)
        return (((late >> lvl) & 1) == 1) & (((early >> lvl) & 1) == 0) & ((r >> (lvl + 1)) == (c >> (lvl + 1)))

    def active_blocks(lvl):
        hb = 1 << lvl
        return [slice(rb * hb, (rb + 1) * hb) for rb in range(tile // hb) if rb % 2 == 1 - direction]

    gt = gt_ref[...].astype(F32)
    g_all = ga_ref[...] * _softplus(gt + gb_ref[...])
    beta_all = _sigmoid(gt)
    gc_all = _mm_exact_lhs(tri_bf, g_all)
    gl_all = _mm_exact_lhs(blk_bf, g_all)
    gc_t = jnp.transpose(gc_all)
    chunk_order = range(nchunk) if direction == 0 else range(nchunk - 1, -1, -1)
    eye = jnp.where(ri == ci, 1.0, 0.0)

    def lane(h):
        return direction * DN_HEADS + h

    def hsl(h):
        return slice(h * DN_DK, (h + 1) * DN_DK)

    for h in heads:
        gcol = gc_all[:, lane(h):lane(h) + 1]
        grow = gc_t[lane(h):lane(h) + 1, :]
        glcol = gl_all[:, lane(h):lane(h) + 1]
        bcol = beta_all[:, 2 * DN_HEADS + lane(h):2 * DN_HEADS + lane(h) + 1]
        q_bf, k_bf = q_ref[:, hsl(h)], k_ref[:, hsl(h)]
        qh, kh, vh = q_bf.astype(F32), k_bf.astype(F32), v_ref[:, hsl(h)].astype(F32)
        decay = jnp.exp(jnp.where(incl, gcol - grow, NEG_BIG))
        a_mat = jnp.where(strict, bcol * _dot_nt(k_bf, k_bf) * decay, 0.0)
        a_scr[h] = a_mat
        t_scr[h] = (eye - jnp.where(off_mask(0), a_mat, 0.0)).astype(BF16)
        attn_scr[h] = (_dot_nt(q_bf, k_bf) * decay).astype(BF16)
        egc = jnp.exp(gcol)
        rhs_scr[h] = jnp.concatenate([vh * bcol, kh * (bcol * egc)], axis=-1).astype(BF16)
        qg_scr[h] = (qh * egc).astype(BF16)
        kg_scr[h] = (kh * jnp.exp(glcol - gcol)).astype(BF16)

    for lvl in range(1, nlev):
        hb = 1 << lvl
        if hb == SUBLANES:
            blocks = active_blocks(lvl)
            zero = jnp.zeros((hb, tile), F32)

            def scatter(act):
                pieces = []
                for i in range(len(blocks)):
                    p = act[i * hb:(i + 1) * hb, :]
                    pieces += [zero, p] if direction == 0 else [p, zero]
                return jnp.concatenate(pieces, axis=0)

            for h in heads:
                lhs = jnp.concatenate([jnp.where(off_mask(lvl, rs), a_scr[h, rs, :], 0.0) for rs in blocks], axis=0)
                m_scr[h] = scatter(_dot(lhs.astype(BF16), t_scr[h])).astype(BF16)
            for h in heads:
                t_full = t_scr[h].astype(F32)
                t_act = jnp.concatenate([t_full[rs, :] for rs in blocks], axis=0)
                t_scr[h] = (t_full - scatter(_dot(t_act.astype(BF16), m_scr[h]))).astype(BF16)
            continue
        if hb < SUBLANES:
            for h in heads:
                m_scr[h] = _dot(jnp.where(off_mask(lvl), a_scr[h], 0.0).astype(BF16), t_scr[h]).astype(BF16)
            for h in heads:
                t_scr[h] = t_scr[h] - _dot(t_scr[h], m_scr[h]).astype(BF16)
            continue
        blocks = active_blocks(lvl)
        for h in heads:
            lhs = jnp.concatenate([jnp.where(off_mask(lvl, rs), a_scr[h, rs, :], 0.0) for rs in blocks], axis=0)
            m_act = _dot(lhs.astype(BF16), t_scr[h])
            zero = jnp.zeros((hb, tile), F32)
            pieces = [zero] * (tile // hb)
            for i, rs in enumerate(blocks):
                pieces[rs.start // hb] = m_act[i * hb:(i + 1) * hb, :]
            m_scr[h] = jnp.concatenate(pieces, axis=0).astype(BF16)
        for h in heads:
            t_act = jnp.concatenate([t_scr[h, rs, :].astype(F32) for rs in blocks], axis=0)
            upd = _dot(t_act.astype(BF16), m_scr[h])
            for i, rs in enumerate(blocks):
                t_scr[h, rs, :] = t_scr[h, rs, :] - upd[i * hb:(i + 1) * hb, :].astype(BF16)

    for h in heads:
        x = _dot(t_scr[h], rhs_scr[h])
        u_scr[h] = x[:, :DN_DV]
        w_scr[h] = x[:, DN_DV:].astype(BF16)

    for c in chunk_order:
        rows = slice(c * c_len, (c + 1) * c_len)
        for h in heads:
            s = s_scr[h]
            s_bf = s.astype(BF16)
            v_new = u_scr[h, rows, :] - _dot(w_scr[h, rows, :], s_bf)
            vn_scr[h, rows, :] = v_new.astype(BF16)
            o_c = _dot(qg_scr[h, rows, :], s_bf) + _dot(attn_scr[h, rows, :], vn_scr[h])
            if add_ref is not None:
                o_c = o_c + add_ref[rows, hsl(h)].astype(F32)
            o_ref[rows, hsl(h)] = o_c.astype(o_ref.dtype)
            gl_c = gl_all[c * c_len:c * c_len + 1, lane(h):lane(h) + 1]
            s_scr[h] = s * jnp.exp(gl_c) + _dot_tn(kg_scr[h, rows, :], v_new.astype(BF16))

    @pl.when(t == nt - 1)
    def _():
        sl_ref[...] = s_scr[...]


def _gdn_body_sum(q_ref, k_ref, v_ref, gt_ref, ga_ref, gb_ref, s0_ref, add_ref, *rest, direction):
    _gdn_body(q_ref, k_ref, v_ref, gt_ref, ga_ref, gb_ref, s0_ref, *rest, direction=direction, add_ref=add_ref)


def _gdn_scan(qkv, proj, ga_row, gb_row, s0, o_prev=None, *, nb, seq_len, direction):
    nt = seq_len // GDN_TILE
    tile = GDN_TILE
    nq = D_MODEL // D_MODEL

    def row(b, t):
        return b * nt + (t if direction == 0 else nt - 1 - t)

    has_prev = o_prev is not None
    prev_specs = [pl.BlockSpec((tile, D_MODEL), lambda b, t: (row(b, t), 0))] if has_prev else []
    prev_args = [o_prev] if has_prev else []
    return pl.pallas_call(
        functools.partial(_gdn_body_sum if has_prev else _gdn_body, direction=direction),
        grid=(nb, nt),
        in_specs=[
            pl.BlockSpec((tile, D_MODEL), lambda b, t: (row(b, t), 0)),
            pl.BlockSpec((tile, D_MODEL), lambda b, t: (row(b, t), nq)),
            pl.BlockSpec((tile, D_MODEL), lambda b, t: (row(b, t), 2 * nq)),
            pl.BlockSpec((tile, LANES), lambda b, t: (row(b, t), DN_GATE_BLK)),
            pl.BlockSpec((1, LANES), lambda b, t: (0, 0)),
            pl.BlockSpec((1, LANES), lambda b, t: (0, 0)),
            pl.BlockSpec((None, DN_HEADS, DN_DK, DN_DV), lambda b, t: (b, 0, 0, 0)),
        ] + prev_specs,
        out_specs=[
            pl.BlockSpec((tile, D_MODEL), lambda b, t: (row(b, t), 0)),
            pl.BlockSpec((None, DN_HEADS, DN_DK, DN_DV), lambda b, t: (b, 0, 0, 0)),
        ],
        out_shape=[
            jax.ShapeDtypeStruct((nb * seq_len, D_MODEL), BF16),
            jax.ShapeDtypeStruct((nb, DN_HEADS, DN_DK, DN_DV), F32),
        ],
        scratch_shapes=[
            pltpu.VMEM((DN_HEADS, DN_DK, DN_DV), F32),
            pltpu.VMEM((DN_HEADS, tile, DN_DV), BF16),
            pltpu.VMEM((DN_HEADS, tile, tile), F32),
            pltpu.VMEM((DN_HEADS, tile, tile), BF16),
            pltpu.VMEM((DN_HEADS, tile, tile), BF16),
            pltpu.VMEM((DN_HEADS, tile, tile), BF16),
            pltpu.VMEM((DN_HEADS, tile, 2 * DN_DV), BF16),
            pltpu.VMEM((DN_HEADS, tile, DN_DV), F32),
            pltpu.VMEM((DN_HEADS, tile, DN_DK), BF16),
            pltpu.VMEM((DN_HEADS, tile, DN_DK), BF16),
            pltpu.VMEM((DN_HEADS, tile, DN_DK), BF16),
        ],
        compiler_params=_cp(("parallel", "arbitrary")),
        name="gdn_scan_fwd" if direction == 0 else "gdn_scan_bwd",
    )(qkv, qkv, qkv, proj, ga_row, gb_row, s0, *prev_args)


def _odd_out_body(x_ref, osum_ref, z_ref, ng_ref, wo_ref, gt_ref, o_ref):
    o = osum_ref[...].astype(F32)
    segs = []
    for h in range(DN_HEADS):
        seg = o[:, h * DN_DV:(h + 1) * DN_DV]
        segs.append(seg * lax.rsqrt(jnp.mean(seg * seg, axis=-1, keepdims=True) + EPS))
    y = jnp.concatenate(segs, axis=-1) * ng_ref[...] * _silu(z_ref[...].astype(F32))
    o_ref[...] = x_ref[...] + gt_ref[...] * _dot(y.astype(BF16), wo_ref[...])


def _odd_out(x, o_sum, proj, ng_row, wo_bf, modt_l, seq_fn, *, tm):
    t = x.shape[0]
    zblk = DN_QKV // D_MODEL
    return pl.pallas_call(
        _odd_out_body,
        grid=(t // tm,),
        in_specs=[
            pl.BlockSpec((tm, D_MODEL), lambda i: (i, 0)),
            pl.BlockSpec((tm, D_MODEL), lambda i: (i, 0)),
            pl.BlockSpec((tm, D_MODEL), lambda i: (i, zblk)),
            pl.BlockSpec((1, D_MODEL), lambda i: (0, 0)),
            pl.BlockSpec((D_MODEL, D_MODEL), lambda i: (0, 0)),
            _mod_spec(2, seq_fn, 1),
        ],
        out_specs=pl.BlockSpec((tm, D_MODEL), lambda i: (i, 0)),
        out_shape=jax.ShapeDtypeStruct((t, D_MODEL), F32),
        compiler_params=_cp(("parallel",)),
        name="odd_out",
    )(x, o_sum, proj, ng_row, wo_bf, modt_l)


ROW_TILE = 1024


def _seq_fn(slot0, seq_len, tm):
    if slot0 == 0:
        return lambda i: 0
    return lambda i: slot0 + (i * tm) // seq_len


def kernel(x_prompt, x_sample, c, cache_na_k, cache_na_v, state_s5_re, state_s5_im, state_dn, c_ctx, norm_mix_g, norm_ff_g, w_mod, b_mod, w_ff1, w_ff2, w_in_e, w_out_e, s5_lam_re, s5_lam_im, s5_log_dt, s5_b_re, s5_b_im, s5_c_re, s5_c_im, s5_d, s5_w_glu, s5_b_glu, na_rpb, w_in_o, dn_conv_w, dn_a_log, dn_dt_bias, dn_norm_g, w_out_o, final_norm_g):
    nbp, lp = x_prompt.shape[:2]
    nbs, ls = x_sample.shape[:2]
    assert nbp % S5_BG == 0 and nbs % S5_BG == 0 and 1 + nbs <= 16
    streams = [
        dict(x=x_prompt.reshape(nbp * lp, D_MODEL), nb=nbp, L=lp, slot0=0),
        dict(x=x_sample.reshape(nbs * ls, D_MODEL), nb=nbs, L=ls, slot0=1),
    ]
    cond16 = jnp.zeros((16, D_MODEL), F32).at[0].set(c_ctx).at[1:1 + nbs].set(c)
    mod = _modulation(cond16, w_mod, b_mod[:, None, :])
    modt = jnp.transpose(mod.reshape(DEPTH, 16, 6, D_MODEL), (0, 2, 1, 3))[:, :, :, None, :]

    new_k, new_v, new_sre, new_sim, new_dn = [], [], [], [], []
    for l in range(DEPTH):
        e = l // 2
        if l % 2 == 0:
            w_in = w_in_e[e].astype(BF16)
            w_out = w_out_e[e].astype(BF16)
            w_glu = s5_w_glu[e].astype(BF16)
            ops = _s5_operators(s5_lam_re[e], s5_lam_im[e], s5_log_dt[e], s5_b_re[e], s5_b_im[e],
                                s5_c_re[e], s5_c_im[e])
            bias = _na_bias_tables(na_rpb[e], ls // GRID_W)
        else:
            w_in = jnp.pad(w_in_o[e], ((0, 0), (0, DN_PROJ_PAD - w_in_o.shape[-1]))).astype(BF16)
            w_out = w_out_o[e].astype(BF16)
            ga_row = jnp.zeros((1, LANES), F32).at[0, :2 * DN_HEADS].set(-jnp.exp(dn_a_log[e]).reshape(-1))
            gb_row = jnp.zeros((1, LANES), F32).at[0, :2 * DN_HEADS].set(dn_dt_bias[e].reshape(-1))
            ng_row = jnp.tile(dn_norm_g[e], DN_HEADS)[None]
        w1, w2 = w_ff1[l], w_ff2[l]
        for st in streams:
            x, nb, sl = st["x"], st["nb"], st["L"]
            seq = _seq_fn(st["slot0"], sl, ROW_TILE)
            is_ctx = st["slot0"] == 0
            if l % 2 == 0:
                proj = _norm_mod_matmul(x, norm_mix_g[l][None], modt[l], w_in, seq, tm=ROW_TILE, name="in_proj_even")
                nbg = nb // S5_BG
                if is_ctx:
                    x0 = jnp.zeros((nbg, S5_JB, S5_BG, 4 * S5_SW), F32)
                else:
                    x0 = _s5_pack_state(state_s5_re[:, e], state_s5_im[:, e])
                y, fin = _s5_mix(proj, ops, x0, nbg=nbg, seq_len=sl)
                if is_ctx:
                    bo = _context_attention(proj, nb=nb, seq_len=sl)
                    kv = proj.reshape(nb, sl, -1)
                    new_k.append(kv[:, :, MIX_A + MIX_B:MIX_A + 2 * MIX_B].reshape(nb, sl, NA_HEADS, NA_HD))
                    new_v.append(kv[:, :, MIX_A + 2 * MIX_B:].reshape(nb, sl, NA_HEADS, NA_HD))
                    sre, sim = _s5_unpack_state(fin)
                    new_sre.append(sre)
                    new_sim.append(sim)
                else:
                    past = cache_na_k.shape[2]
                    ck = cache_na_k[:, e].reshape(nb, past, MIX_B)
                    cv = cache_na_v[:, e].reshape(nb, past, MIX_B)
                    bo = _neighbourhood_attention(proj, ck, cv, bias, nb=nb, seq_len=sl)
                x = _even_out(x, y.reshape(nb * sl, MIX_A), proj, bo, s5_d[e][None], w_glu, s5_b_glu[e][None],
                              w_out, modt[l], seq, tm=ROW_TILE)
            else:
                proj = _norm_mod_matmul(x, norm_mix_g[l][None], modt[l], w_in, seq, tm=ROW_TILE, name="in_proj_odd",
                                        out_dtype=BF16)
                qkv = _gdn_pre(proj, dn_conv_w[e], nb=nb, seq_len=sl)
                if is_ctx:
                    s0 = jnp.zeros((nb, 2, DN_HEADS, DN_DK, DN_DV), F32)
                else:
                    s0 = state_dn[:, e]
                o_f, s_f = _gdn_scan(qkv, proj, ga_row, gb_row, s0[:, 0], nb=nb, seq_len=sl, direction=0)
                o_sum, s_b = _gdn_scan(qkv, proj, ga_row, gb_row, s0[:, 1], o_f, nb=nb, seq_len=sl, direction=1)
                if is_ctx:
                    new_dn.append(jnp.stack([s_f, s_b], axis=1))
                x = _odd_out(x, o_sum, proj, ng_row, w_out, modt[l], seq, tm=ROW_TILE)
            st["x"] = _mlp(x, norm_ff_g[l][None], modt[l], w1, w2, seq, final_norm_g[None], tm=ROW_TILE, tf=1024,
                           final_norm=(l == DEPTH - 1))

    y_prompt = streams[0]["x"].reshape(nbp, lp, D_MODEL)
    y_sample = streams[1]["x"].reshape(nbs, ls, D_MODEL)
    return (y_prompt, y_sample, jnp.stack(new_k, axis=1), jnp.stack(new_v, axis=1),
            jnp.stack(new_sre, axis=1), jnp.stack(new_sim, axis=1), jnp.stack(new_dn, axis=1))
```
